```python
import math
import jax
import jax.numpy as jnp
from jax import lax
import numpy as np

D_MODEL = 2048
BATCH = 1
SEQ = 16384
DEPTH = 1

PLE_DIM = 256
SSM_WIDTH = D_MODEL // 2
SSM_GROUP = 16
SSM_GROUPS = SSM_WIDTH // SSM_GROUP
SSM_STATE = 64
DT_MIN = 1e-3
DT_MAX = 1e-1
RET_HEADS = 8
RET_HEAD_DIM = D_MODEL // (2 * RET_HEADS)
RET_WIDTH = RET_HEADS * RET_HEAD_DIM
RET_CHUNK = 128
ROPE_BASE = 10000.0
N_EXPERTS = 32
TOP_K = 4
EXPERT_FF = D_MODEL
SWIGLU_LIMIT = 7.0
SWIGLU_ALPHA = 1.702
MOE_BLOCK = 256
NORM_EPS = 1e-6
IN_COLS = SSM_WIDTH + 4 * RET_WIDTH + 2 * D_MODEL

kernel_name = 'hybrid_s5_retention_moe_block'


def rms_norm(x, g):
    xf = x.astype(jnp.float32)
    y = xf * lax.rsqrt(jnp.mean(xf * xf, axis=-1, keepdims=True) + NORM_EPS)
    return (y * g.astype(jnp.float32)).astype(x.dtype)


def _cmul(ar, ai, br, bi):
    return ar * br - ai * bi, ar * bi + ai * br


def s5_branch(u, lam_re, lam_im, log_dt, b_re, b_im, c_re, c_im, d_skip, w_glu, b_glu):
    f32 = jnp.float32
    bsz, seq, _ = u.shape
    uf = u.astype(f32).reshape(bsz, seq, SSM_GROUPS, SSM_GROUP)
    lr, li = lam_re.astype(f32), lam_im.astype(f32)
    dt = jnp.exp(log_dt.astype(f32))[:, None]
    mag = jnp.exp(lr * dt)
    ab_re, ab_im = mag * jnp.cos(li * dt), mag * jnp.sin(li * dt)
    den = lr * lr + li * li
    nr, ni = ab_re - 1.0, ab_im
    f_re = (nr * lr + ni * li) / den
    f_im = (ni * lr - nr * li) / den
    bb_re, bb_im = _cmul(f_re[..., None], f_im[..., None], b_re.astype(f32), b_im.astype(f32))
    bu_re = jnp.einsum('blgc,gnc->blgn', uf, bb_re)
    bu_im = jnp.einsum('blgc,gnc->blgn', uf, bb_im)
    a_re = jnp.broadcast_to(ab_re, bu_re.shape)
    a_im = jnp.broadcast_to(ab_im, bu_im.shape)

    def combine(e1, e2):
        a1r, a1i, b1r, b1i = e1
        a2r, a2i, b2r, b2i = e2
        ar, ai = _cmul(a2r, a2i, a1r, a1i)
        tr, ti = _cmul(a2r, a2i, b1r, b1i)
        return ar, ai, tr + b2r, ti + b2i

    _, _, s_re, s_im = lax.associative_scan(combine, (a_re, a_im, bu_re, bu_im), axis=1)
    y = (jnp.einsum('blgn,gcn->blgc', s_re, c_re.astype(f32))
         - jnp.einsum('blgn,gcn->blgc', s_im, c_im.astype(f32))
         + d_skip.astype(f32) * uf)
    y = jax.nn.gelu(y.reshape(bsz, seq, SSM_WIDTH))
    y = y * jax.nn.sigmoid(y @ w_glu.astype(f32) + b_glu.astype(f32))
    return y.astype(u.dtype)


def rotary(x, positions):
    half = x.shape[-1] // 2
    inv = ROPE_BASE ** (-jnp.arange(half, dtype=jnp.float32) / half)
    ang = positions.astype(jnp.float32)[..., None] * inv
    cos, sin = jnp.cos(ang)[:, :, None, :], jnp.sin(ang)[:, :, None, :]
    x1, x2 = x[..., :half], x[..., half:]
    return jnp.concatenate([x1 * cos - x2 * sin, x1 * sin + x2 * cos], axis=-1)


def retention_branch(q, k, v, gate, positions):
    f32 = jnp.float32
    bsz, seq, _ = q.shape
    H, dk = RET_HEADS, RET_HEAD_DIM
    C = RET_CHUNK
    nc = seq // C
    q = rotary(q.astype(f32).reshape(bsz, seq, H, dk), positions)
    k = rotary(k.astype(f32).reshape(bsz, seq, H, dk), positions) * (dk ** -0.5)
    v = v.astype(f32).reshape(bsz, seq, H, dk)
    log_gamma = jnp.log(1.0 - jnp.exp2(-5.0 - jnp.arange(H, dtype=f32)))
    qc = q.reshape(bsz, nc, C, H, dk)
    kc = k.reshape(bsz, nc, C, H, dk)
    vc = v.reshape(bsz, nc, C, H, dk)
    idx = jnp.arange(C, dtype=f32)
    rel = idx[:, None] - idx[None, :]
    decay = jnp.where(rel >= 0, jnp.exp(jnp.maximum(rel, 0.0)[None] * log_gamma[:, None, None]), 0.0)
    scores = jnp.einsum('bnihd,bnjhd->bnhij', qc, kc) * decay
    inner = jnp.einsum('bnhij,bnjhe->bnihe', scores, vc)
    zeta = jnp.exp((C - 1.0 - idx)[:, None] * log_gamma[None])
    kv = jnp.einsum('bnjhd,bnjhe->bnhde', kc * zeta[..., None], vc)
    chunk_decay = jnp.exp(C * log_gamma)[None, :, None, None]

    def step(state, kv_n):
        return state * chunk_decay + kv_n, state

    _, r_prev = lax.scan(step, jnp.zeros((bsz, H, dk, dk), f32), jnp.moveaxis(kv, 1, 0))
    r_prev = jnp.moveaxis(r_prev, 0, 1)
    xi = jnp.exp((idx + 1.0)[:, None] * log_gamma[None])
    cross = jnp.einsum('bnihd,bnhde->bnihe', qc, r_prev) * xi[..., None]
    o = (inner + cross).reshape(bsz, seq, H, dk)
    mu = jnp.mean(o, axis=-1, keepdims=True)
    oc = o - mu
    o = oc * lax.rsqrt(jnp.mean(oc * oc, axis=-1, keepdims=True) + NORM_EPS)
    o = jax.nn.silu(gate.astype(f32)) * o.reshape(bsz, seq, RET_WIDTH)
    return o.astype(gate.dtype)


def moe(h, router_w, router_b, w_gu, b_gu, w_dn, b_dn):
    bsz, seq, d = h.shape
    n_tok = bsz * seq
    n_asg = n_tok * TOP_K
    xt = h.reshape(n_tok, d)
    logits = (xt @ router_w + router_b).astype(jnp.float32)
    top_val, top_idx = lax.top_k(logits, TOP_K)
    gates = jax.nn.softmax(top_val, axis=-1)
    e_flat = top_idx.reshape(-1).astype(jnp.int32)
    tok_flat = jnp.arange(n_asg, dtype=jnp.int32) // TOP_K
    g_flat = gates.reshape(-1)
    order = jnp.argsort(e_flat)
    e_s, tok_s, g_s = e_flat[order], tok_flat[order], g_flat[order]
    counts = jnp.bincount(e_flat, length=N_EXPERTS).astype(jnp.int32)
    start = jnp.cumsum(counts) - counts
    pcounts = (counts + MOE_BLOCK - 1) // MOE_BLOCK * MOE_BLOCK
    pend = jnp.cumsum(pcounts)
    pstart = pend - pcounts
    dest = pstart[e_s] + (jnp.arange(n_asg, dtype=jnp.int32) - start[e_s])
    n_rows = n_asg + N_EXPERTS * MOE_BLOCK
    n_blocks = n_rows // MOE_BLOCK
    row_tok = jnp.zeros((n_rows,), jnp.int32).at[dest].set(tok_s)
    row_g = jnp.zeros((n_rows,), jnp.float32).at[dest].set(g_s)
    blk_start = jnp.arange(n_blocks, dtype=jnp.int32) * MOE_BLOCK
    blk_e = jnp.minimum(jnp.searchsorted(pend, blk_start, side='right'), N_EXPERTS - 1)

    def run_block(args):
        toks, e = args
        xb = xt[toks]
        gu = xb @ w_gu[e] + b_gu[e]
        g, u = gu[:, 0::2], gu[:, 1::2]
        g = jnp.minimum(g, SWIGLU_LIMIT)
        u = jnp.clip(u, -SWIGLU_LIMIT, SWIGLU_LIMIT)
        a = g * jax.nn.sigmoid(SWIGLU_ALPHA * g) * (u + 1.0)
        return a @ w_dn[e] + b_dn[e]

    yb = lax.map(run_block, (row_tok.reshape(n_blocks, MOE_BLOCK), blk_e))
    y = jnp.zeros((n_tok, d), jnp.float32).at[row_tok].add(
        yb.reshape(n_rows, d).astype(jnp.float32) * row_g[:, None])
    return y.reshape(bsz, seq, d).astype(h.dtype)


def setup_inputs(seed: int = 0) -> dict:
    key = jax.random.key(seed)
    ks = jax.random.split(key, 32)
    f32 = jnp.float32

    def nrm(k, shape, scale):
        return jax.random.normal(k, shape, f32) * scale

    Ld, G, N, E, F = DEPTH, SSM_GROUPS, SSM_STATE, N_EXPERTS, EXPERT_FF
    return {
        'x': nrm(ks[0], (BATCH, SEQ, D_MODEL), 1.0),
        'p': nrm(ks[1], (DEPTH, BATCH, SEQ, PLE_DIM), 1.0),
        'positions': jnp.broadcast_to(jnp.arange(SEQ, dtype=jnp.int32), (BATCH, SEQ)),
        'norm1_g': 1.0 + nrm(ks[2], (Ld, D_MODEL), 0.01),
        'w_in': nrm(ks[3], (Ld, D_MODEL, IN_COLS), D_MODEL ** -0.5),
        'ssm_lam_re': -0.5 + nrm(ks[4], (Ld, G, N), 0.01),
        'ssm_lam_im': math.pi * jnp.arange(N, dtype=f32) + nrm(ks[5], (Ld, G, N), 0.01),
        'ssm_log_dt': jax.random.uniform(ks[6], (Ld, G), f32, math.log(DT_MIN), math.log(DT_MAX)),
        'ssm_b_re': nrm(ks[7], (Ld, G, N, SSM_GROUP), (2 * SSM_GROUP) ** -0.5),
        'ssm_b_im': nrm(ks[8], (Ld, G, N, SSM_GROUP), (2 * SSM_GROUP) ** -0.5),
        'ssm_c_re': nrm(ks[9], (Ld, G, SSM_GROUP, N), (2 * N) ** -0.5),
        'ssm_c_im': nrm(ks[10], (Ld, G, SSM_GROUP, N), (2 * N) ** -0.5),
        'ssm_d': nrm(ks[11], (Ld, G, SSM_GROUP), 1.0),
        'ssm_w_glu': nrm(ks[12], (Ld, SSM_WIDTH, SSM_WIDTH), SSM_WIDTH ** -0.5),
        'ssm_b_glu': nrm(ks[13], (Ld, SSM_WIDTH), 0.01),
        'w_branch_ssm': nrm(ks[14], (Ld, SSM_WIDTH, D_MODEL), SSM_WIDTH ** -0.5),
        'w_branch_ret': nrm(ks[15], (Ld, RET_WIDTH, D_MODEL), RET_WIDTH ** -0.5),
        'w_out': nrm(ks[16], (Ld, D_MODEL, D_MODEL), D_MODEL ** -0.5),
        'norm2_g': 1.0 + nrm(ks[17], (Ld, D_MODEL), 0.01),
        'router_w': nrm(ks[18], (Ld, D_MODEL, E), D_MODEL ** -0.5),
        'router_b': nrm(ks[19], (Ld, E), 0.01),
        'exp_w_gate_up': nrm(ks[20], (Ld, E, D_MODEL, 2 * F), D_MODEL ** -0.5),
        'exp_b_gate_up': nrm(ks[21], (Ld, E, 2 * F), 0.01),
        'exp_w_down': nrm(ks[22], (Ld, E, F, D_MODEL), F ** -0.5),
        'exp_b_down': nrm(ks[23], (Ld, E, D_MODEL), 0.01),
        'norm3_g': 1.0 + nrm(ks[24], (Ld, D_MODEL), 0.01),
        'ple_w_gate': nrm(ks[25], (Ld, D_MODEL, D_MODEL), D_MODEL ** -0.5),
        'ple_w_proj': nrm(ks[26], (Ld, PLE_DIM, D_MODEL), PLE_DIM ** -0.5),
        'final_norm_g': 1.0 + nrm(ks[27], (D_MODEL,), 0.01),
    }


def reference(x, p, positions, norm1_g, w_in, ssm_lam_re, ssm_lam_im, ssm_log_dt,
              ssm_b_re, ssm_b_im, ssm_c_re, ssm_c_im, ssm_d, ssm_w_glu, ssm_b_glu,
              w_branch_ssm, w_branch_ret, w_out, norm2_g, router_w, router_b,
              exp_w_gate_up, exp_b_gate_up, exp_w_down, exp_b_down, norm3_g,
              ple_w_gate, ple_w_proj, final_norm_g):
    cuts = [SSM_WIDTH,
            SSM_WIDTH + RET_WIDTH,
            SSM_WIDTH + 2 * RET_WIDTH,
            SSM_WIDTH + 3 * RET_WIDTH,
            SSM_WIDTH + 4 * RET_WIDTH,
            SSM_WIDTH + 4 * RET_WIDTH + D_MODEL]
    for i in range(DEPTH):
        h = rms_norm(x, norm1_g[i])
        z = h @ w_in[i]
        u_s, q, k, v, g_ret, m_s, m_r = jnp.split(z, cuts, axis=-1)
        ys = s5_branch(u_s, ssm_lam_re[i], ssm_lam_im[i], ssm_log_dt[i], ssm_b_re[i], ssm_b_im[i],
                       ssm_c_re[i], ssm_c_im[i], ssm_d[i], ssm_w_glu[i], ssm_b_glu[i])
        yr = retention_branch(q, k, v, g_ret, positions)
        merged = (jax.nn.sigmoid(m_s) * (ys @ w_branch_ssm[i])
                  + jax.nn.sigmoid(m_r) * (yr @ w_branch_ret[i]))
        x = x + merged @ w_out[i]
        x = x + moe(rms_norm(x, norm2_g[i]), router_w[i], router_b[i], exp_w_gate_up[i],
                    exp_b_gate_up[i], exp_w_down[i], exp_b_down[i])
        ple_gate = jax.nn.sigmoid(rms_norm(x, norm3_g[i]) @ ple_w_gate[i])
        x = x + ple_gate * (p[i] @ ple_w_proj[i])
    return rms_norm(x, final_norm_g)
```

```python
import functools
import math

import jax
import jax.numpy as jnp
from jax import lax
from jax.experimental import pallas as pl
from jax.experimental.pallas import tpu as pltpu

F32 = jnp.float32
BF16 = jnp.bfloat16

D_MODEL = 2048
PLE_DIM = 256
SSM_WIDTH = D_MODEL // 2
SSM_GROUP = 16
SSM_GROUPS = SSM_WIDTH // SSM_GROUP
SSM_STATE = 64
RET_HEADS = 8
RET_HEAD_DIM = 128
RET_WIDTH = RET_HEADS * RET_HEAD_DIM
RET_CHUNK = 128
ROPE_BASE = 10000.0
N_EXPERTS = 32
TOP_K = 4
EXPERT_FF = D_MODEL
SWIGLU_LIMIT = 7.0
SWIGLU_ALPHA = 1.702
NORM_EPS = 1e-6
IN_COLS = SSM_WIDTH + 4 * RET_WIDTH + 2 * D_MODEL

LANES = 128
SUBLANES = 8
ROUTER_PAD = LANES
NEG_BIG = -1e30

MIB = 1024 * 1024


def _cparams(sem, vmem_mib):
    return pltpu.CompilerParams(dimension_semantics=sem, vmem_limit_bytes=vmem_mib * MIB)


def _rms(xf, g):
    ms = jnp.mean(xf * xf, axis=-1, keepdims=True)
    return xf * lax.rsqrt(ms + NORM_EPS) * g


def _sigmoid(x):
    return 1.0 / (1.0 + jnp.exp(-x))


def _inproj_kernel(x_ref, g_ref, w_ref, o_ref, h_scr):
    @pl.when(pl.program_id(1) == 0)
    def _():
        h_scr[...] = _rms(x_ref[...], g_ref[...]).astype(BF16)

    o_ref[...] = jnp.dot(h_scr[...], w_ref[...], preferred_element_type=F32).astype(o_ref.dtype)


def _inproj(x2, g1, w_in_bf, tm, tn):
    L = x2.shape[0]
    n_cols = w_in_bf.shape[1]
    return pl.pallas_call(
        _inproj_kernel,
        grid=(L // tm, n_cols // tn),
        in_specs=[
            pl.BlockSpec((tm, D_MODEL), lambda i, j: (i, 0)),
            pl.BlockSpec((1, D_MODEL), lambda i, j: (0, 0)),
            pl.BlockSpec((D_MODEL, tn), lambda i, j: (0, j)),
        ],
        out_specs=pl.BlockSpec((tm, tn), lambda i, j: (i, j)),
        out_shape=jax.ShapeDtypeStruct((L, n_cols), BF16),
        scratch_shapes=[pltpu.VMEM((tm, D_MODEL), BF16)],
        compiler_params=_cparams(("arbitrary", "arbitrary"), 48),
    )(x2, g1, w_in_bf)


def _s5_prep_kernel(lr_ref, li_ref, ldt_ref, bre_ref, bim_ref, abre_ref, abim_ref, bbre_ref, bbim_ref):
    lr, li = lr_ref[...], li_ref[...]
    dt = jnp.exp(ldt_ref[...])
    mag = jnp.exp(lr * dt)
    ab_re, ab_im = mag * jnp.cos(li * dt), mag * jnp.sin(li * dt)
    den = lr * lr + li * li
    nr, ni = ab_re - 1.0, ab_im
    f_re = (nr * lr + ni * li) / den
    f_im = (ni * lr - nr * li) / den
    bre, bim = bre_ref[...], bim_ref[...]
    abre_ref[...] = ab_re
    abim_ref[...] = ab_im
    bbre_ref[...] = f_re * bre - f_im * bim
    bbim_ref[...] = f_re * bim + f_im * bre


def _s5_prep(lam_re, lam_im, log_dt, b_re, b_im):
    rep = lambda a: jnp.repeat(a, SSM_GROUP, axis=0)
    bt = lambda b: jnp.transpose(b, (0, 2, 1)).reshape(SSM_WIDTH, SSM_STATE)
    shp = jax.ShapeDtypeStruct((SSM_WIDTH, SSM_STATE), F32)
    return pl.pallas_call(_s5_prep_kernel, out_shape=(shp, shp, shp, shp))(
        rep(lam_re), rep(lam_im), rep(log_dt[:, None]), bt(b_re), bt(b_im))


def _gelu_tanh(x):
    return 0.5 * x * (1.0 + jnp.tanh(math.sqrt(2.0 / math.pi) * (x + 0.044715 * (x * x * x))))


N_UBLK = SSM_WIDTH // LANES
ST_PER_UBLK = (LANES // SSM_GROUP) * SSM_STATE
TILES_PER_UBLK = ST_PER_UBLK // LANES
N_SVREG = SSM_GROUPS * SSM_STATE // (LANES * SUBLANES)


def _s5_kernel(u_ref, bblk_ref, cblk_ref, are_ref, aim_ref, d_ref, wglu_ref, bglu_ref, o_ref,
               sre, sim, carry):
    tm = u_ref.shape[0]

    @pl.when(pl.program_id(0) == 0)
    def _():
        carry[...] = jnp.zeros_like(carry)

    u = u_ref[...]
    for b in range(N_UBLK):
        bu = jnp.dot(u[:, b * LANES:(b + 1) * LANES], bblk_ref[b], preferred_element_type=F32)
        v = (b * TILES_PER_UBLK) // SUBLANES
        for q in range(TILES_PER_UBLK):
            k = (b * TILES_PER_UBLK + q) % SUBLANES
            sre[v, pl.ds(k, tm, stride=SUBLANES), :] = bu[:, q * LANES:(q + 1) * LANES]
            sim[v, pl.ds(k, tm, stride=SUBLANES), :] = bu[:, ST_PER_UBLK + q * LANES:ST_PER_UBLK + (q + 1) * LANES]

    are, aim = are_ref[...], aim_ref[...]

    def step(t, c):
        cre, cim = c
        r0 = pl.multiple_of(t * SUBLANES, SUBLANES)
        bre = sre[:, pl.ds(r0, SUBLANES), :]
        bim = sim[:, pl.ds(r0, SUBLANES), :]
        nre = are * cre - aim * cim + bre
        nim = are * cim + aim * cre + bim
        sre[:, pl.ds(r0, SUBLANES), :] = nre
        sim[:, pl.ds(r0, SUBLANES), :] = nim
        return nre, nim

    cre, cim = lax.fori_loop(0, tm, step, (carry[0], carry[1]), unroll=4)
    carry[0] = cre
    carry[1] = cim

    ys = []
    for b in range(N_UBLK):
        v = (b * TILES_PER_UBLK) // SUBLANES
        k0 = (b * TILES_PER_UBLK) % SUBLANES
        parts = [sre[v, pl.ds(k0 + q, tm, stride=SUBLANES), :] for q in range(TILES_PER_UBLK)]
        parts += [sim[v, pl.ds(k0 + q, tm, stride=SUBLANES), :] for q in range(TILES_PER_UBLK)]
        sb = jnp.concatenate(parts, axis=-1).astype(BF16)
        ys.append(jnp.dot(sb, cblk_ref[b], preferred_element_type=F32))
    y = jnp.concatenate(ys, axis=-1) + d_ref[...] * u.astype(F32)
    y = _gelu_tanh(y)
    gl = jnp.dot(y.astype(BF16), wglu_ref[...], preferred_element_type=F32) + bglu_ref[...]
    o_ref[...] = (y * _sigmoid(gl)).astype(o_ref.dtype)


def _s5(z, u_blk, bblk, cblk, are, aim, d_skip, wglu, bglu, tm):
    L = z.shape[0]
    const = lambda *shape: pl.BlockSpec(shape, lambda i: (0,) * len(shape))
    return pl.pallas_call(
        _s5_kernel,
        grid=(L // tm,),
        in_specs=[
            pl.BlockSpec((tm, SSM_WIDTH), lambda i: (i, u_blk)),
            const(N_UBLK, LANES, 2 * ST_PER_UBLK),
            const(N_UBLK, 2 * ST_PER_UBLK, LANES),
            const(N_SVREG, SUBLANES, LANES),
            const(N_SVREG, SUBLANES, LANES),
            const(1, SSM_WIDTH),
            const(SSM_WIDTH, SSM_WIDTH),
            const(1, SSM_WIDTH),
        ],
        out_specs=pl.BlockSpec((tm, SSM_WIDTH), lambda i: (i, 0)),
        out_shape=jax.ShapeDtypeStruct((L, SSM_WIDTH), BF16),
        scratch_shapes=[
            pltpu.VMEM((N_SVREG, tm * SUBLANES, LANES), F32),
            pltpu.VMEM((N_SVREG, tm * SUBLANES, LANES), F32),
            pltpu.VMEM((2, N_SVREG, SUBLANES, LANES), F32),
        ],
        compiler_params=_cparams(("arbitrary",), 48),
    )(z, bblk, cblk, are, aim, d_skip, wglu, bglu)


def _ret_kernel(q_ref, k_ref, v_ref, g_ref, pos_ref, inv_ref, decay_ref, zeta_ref, xi_ref, cd_ref,
                o_ref, r_scr):
    tm = q_ref.shape[0]
    C, dk = RET_CHUNK, RET_HEAD_DIM

    @pl.when(pl.program_id(0) == 0)
    def _():
        r_scr[...] = jnp.zeros_like(r_scr)

    ang = pos_ref[...].astype(F32) * inv_ref[...]
    cs, sn = jnp.cos(ang), jnp.sin(ang)
    cf = jnp.concatenate([cs, cs], axis=-1)
    sf = jnp.concatenate([-sn, sn], axis=-1)

    def rot(xh):
        return xh * cf + pltpu.roll(xh, dk // 2, axis=1) * sf

    nt = (((1,), (1,)), ((), ()))
    for h in range(RET_HEADS):
        cols = slice(h * dk, (h + 1) * dk)
        qr = rot(q_ref[:, cols].astype(F32))
        kr = rot(k_ref[:, cols].astype(F32))
        for n in range(tm // C):
            rows = slice(n * C, (n + 1) * C)
            qc = qr[rows].astype(BF16)
            kcf = kr[rows]
            vc = v_ref[rows, cols]
            sc = lax.dot_general(qc, kcf.astype(BF16), nt, preferred_element_type=F32) * decay_ref[h]
            inner = jnp.dot(sc.astype(BF16), vc, preferred_element_type=F32)
            r_prev = r_scr[h]
            cross = jnp.dot(qc, r_prev.astype(BF16), preferred_element_type=F32) * xi_ref[h]
            o = inner + cross
            kzt = jnp.transpose(kcf * zeta_ref[h]).astype(BF16)
            r_scr[h] = r_prev * cd_ref[h] + jnp.dot(kzt, vc, preferred_element_type=F32)
            mu = jnp.mean(o, axis=-1, keepdims=True)
            oc = o - mu
            on = oc * lax.rsqrt(jnp.mean(oc * oc, axis=-1, keepdims=True) + NORM_EPS)
            gate = g_ref[rows, cols].astype(F32)
            o_ref[rows, cols] = (gate * _sigmoid(gate) * on).astype(o_ref.dtype)


def _retention(z, blks, pos, tm):
    L = z.shape[0]
    H, C, dk = RET_HEADS, RET_CHUNK, RET_HEAD_DIM
    half = dk // 2
    inv = (ROPE_BASE ** (-jnp.arange(half, dtype=F32) / half))[None, :]
    log_gamma = jnp.log(1.0 - jnp.exp2(-5.0 - jnp.arange(H, dtype=F32)))
    idx = jnp.arange(C, dtype=F32)
    rel = idx[:, None] - idx[None, :]
    scale = dk ** -0.5
    decay = jnp.where(rel >= 0, jnp.exp(jnp.maximum(rel, 0.0)[None] * log_gamma[:, None, None]), 0.0) * scale
    zeta = jnp.exp((C - 1.0 - idx)[None, :] * log_gamma[:, None]) * scale
    xi = jnp.exp((idx + 1.0)[None, :] * log_gamma[:, None])
    bc = lambda a: jnp.broadcast_to(a[:, :, None], (H, C, dk))
    cd = jnp.broadcast_to(jnp.exp(C * log_gamma)[:, None, None], (H, 1, dk))
    const = lambda *shape: pl.BlockSpec(shape, lambda i: (0,) * len(shape))
    qb, kb, vb, gb = blks
    return pl.pallas_call(
        _ret_kernel,
        grid=(L // tm,),
        in_specs=[
            pl.BlockSpec((tm, RET_WIDTH), lambda i: (i, qb)),
            pl.BlockSpec((tm, RET_WIDTH), lambda i: (i, kb)),
            pl.BlockSpec((tm, RET_WIDTH), lambda i: (i, vb)),
            pl.BlockSpec((tm, RET_WIDTH), lambda i: (i, gb)),
            pl.BlockSpec((tm, 1), lambda i: (i, 0)),
            const(1, half),
            const(H, C, C),
            const(H, C, dk),
            const(H, C, dk),
            const(H, 1, dk),
        ],
        out_specs=pl.BlockSpec((tm, RET_WIDTH), lambda i: (i, 0)),
        out_shape=jax.ShapeDtypeStruct((L, RET_WIDTH), BF16),
        scratch_shapes=[pltpu.VMEM((H, dk, dk), F32)],
        compiler_params=_cparams(("arbitrary",), 48),
    )(z, z, z, z, pos, inv, decay, bc(zeta), bc(xi), cd)


def _merge_kernel(ys_ref, yr_ref, ms_ref, mr_ref, x_ref, ps_ref, pr_ref, wo_ref, g2_ref,
                  rwh_ref, rwl_ref, rb_ref, tril_ref,
                  x1_ref, h2_ref, idx_ref, gate_ref, rank_ref, cnt_ref, carry):
    tm = x_ref.shape[0]

    @pl.when(pl.program_id(0) == 0)
    def _():
        carry[...] = jnp.zeros_like(carry)

    a = jnp.dot(ys_ref[...], ps_ref[...], preferred_element_type=F32)
    b = jnp.dot(yr_ref[...], pr_ref[...], preferred_element_type=F32)
    merged = _sigmoid(ms_ref[...].astype(F32)) * a + _sigmoid(mr_ref[...].astype(F32)) * b
    x1 = x_ref[...] + jnp.dot(merged.astype(BF16), wo_ref[...], preferred_element_type=F32)
    x1_ref[...] = x1
    h2 = _rms(x1, g2_ref[...])
    h2_ref[...] = h2

    hh = h2.astype(BF16)
    hl = (h2 - hh.astype(F32)).astype(BF16)
    logits = (jnp.dot(hh, rwh_ref[...], preferred_element_type=F32)
              + jnp.dot(hh, rwl_ref[...], preferred_element_type=F32)
              + jnp.dot(hl, rwh_ref[...], preferred_element_type=F32)) + rb_ref[...]

    lane = lax.broadcasted_iota(jnp.int32, (tm, ROUTER_PAD), 1)
    work = logits
    vals, idxs = [], []
    for _ in range(TOP_K):
        m = jnp.max(work, axis=-1, keepdims=True)
        ix = jnp.min(jnp.where(work == m, lane, ROUTER_PAD), axis=-1, keepdims=True)
        vals.append(m)
        idxs.append(ix)
        work = jnp.where(lane == ix, -jnp.inf, work)
    es = [jnp.exp(v - vals[0]) for v in vals]
    den = es[0] + es[1] + es[2] + es[3]

    onehot = jnp.zeros((tm, ROUTER_PAD), F32)
    for ix in idxs:
        onehot = onehot + (lane == ix).astype(F32)
    cum = jnp.dot(tril_ref[...], onehot.astype(BF16), preferred_element_type=F32) + carry[...]
    carry[...] = carry[...] + jnp.sum(onehot, axis=0, keepdims=True)
    cnt_ref[...] = jnp.broadcast_to(carry[...], cnt_ref.shape).astype(jnp.int32)

    idx_o = jnp.zeros((tm, ROUTER_PAD), jnp.int32)
    gate_o = jnp.zeros((tm, ROUTER_PAD), F32)
    rank_o = jnp.zeros((tm, ROUTER_PAD), jnp.int32)
    for k in range(TOP_K):
        rk = jnp.sum(jnp.where(lane == idxs[k], cum, 0.0), axis=-1, keepdims=True).astype(jnp.int32)
        idx_o = jnp.where(lane == k, idxs[k], idx_o)
        gate_o = jnp.where(lane == k, es[k] / den, gate_o)
        rank_o = jnp.where(lane == k, rk, rank_o)
    idx_ref[...] = idx_o
    gate_ref[...] = gate_o
    rank_ref[...] = rank_o


def _merge(ys, yr, z, ms_blk, mr_blk, x2, ps, pr, wo, g2, rwh, rwl, rb, tm):
    L = x2.shape[0]
    tril = (jnp.arange(tm)[:, None] > jnp.arange(tm)[None, :]).astype(BF16)
    const = lambda *shape: pl.BlockSpec(shape, lambda i: (0,) * len(shape), pipeline_mode=pl.Buffered(1))
    row = lambda w: pl.BlockSpec((tm, w), lambda i: (i, 0))
    return pl.pallas_call(
        _merge_kernel,
        grid=(L // tm,),
        in_specs=[
            row(SSM_WIDTH), row(RET_WIDTH),
            pl.BlockSpec((tm, D_MODEL), lambda i: (i, ms_blk)),
            pl.BlockSpec((tm, D_MODEL), lambda i: (i, mr_blk)),
            row(D_MODEL),
            const(SSM_WIDTH, D_MODEL), const(RET_WIDTH, D_MODEL), const(D_MODEL, D_MODEL),
            const(1, D_MODEL),
            const(D_MODEL, ROUTER_PAD), const(D_MODEL, ROUTER_PAD), const(1, ROUTER_PAD),
            const(tm, tm),
        ],
        out_specs=[
            row(D_MODEL), row(D_MODEL), row(ROUTER_PAD), row(ROUTER_PAD), row(ROUTER_PAD),
            pl.BlockSpec((SUBLANES, ROUTER_PAD), lambda i: (0, 0)),
        ],
        out_shape=[
            jax.ShapeDtypeStruct((L, D_MODEL), F32),
            jax.ShapeDtypeStruct((L, D_MODEL), F32),
            jax.ShapeDtypeStruct((L, ROUTER_PAD), jnp.int32),
            jax.ShapeDtypeStruct((L, ROUTER_PAD), F32),
            jax.ShapeDtypeStruct((L, ROUTER_PAD), jnp.int32),
            jax.ShapeDtypeStruct((SUBLANES, ROUTER_PAD), jnp.int32),
        ],
        scratch_shapes=[pltpu.VMEM((1, ROUTER_PAD), F32)],
        compiler_params=_cparams(("arbitrary",), 56),
    )(ys, yr, z, z, x2, ps, pr, wo, g2, rwh, rwl, rb, tril)


def _expert_kernel(blk_e_ref, nused_ref, rowtok_ref, h2_hbm, wg_ref, wu_ref, wd_ref, bg_ref, bu_ref,
                   bd_ref, o_ref, xbuf, xb, acc, sem):
    b = pl.program_id(0)
    f = pl.program_id(1)
    n_f = pl.num_programs(1)
    tm = xbuf.shape[0]
    n_used = nused_ref[0]
    share = tm // n_f

    def issue(blk, lo, n):
        base = blk * tm

        def body(r, c):
            tok = rowtok_ref[base + r]
            pltpu.make_async_copy(h2_hbm.at[pl.ds(tok, 1), :], xbuf.at[pl.ds(r, 1), :], sem.at[0]).start()
            return c

        lax.fori_loop(lo, lo + n, body, 0)

    @pl.when(b < n_used)
    def _():
        @pl.when((b == 0) & (f == 0))
        def _():
            issue(0, 0, tm)

        @pl.when(f == 0)
        def _():
            pltpu.make_async_copy(h2_hbm.at[pl.ds(0, tm), :], xbuf, sem.at[0]).wait()
            xb[...] = xbuf[...].astype(BF16)
            acc[...] = jnp.zeros_like(acc)

        @pl.when(b + 1 < n_used)
        def _():
            issue(b + 1, f * share, share)

        x = xb[...]
        g = jnp.dot(x, wg_ref[0], preferred_element_type=F32) + bg_ref[0]
        u = jnp.dot(x, wu_ref[0], preferred_element_type=F32) + bu_ref[0]
        g = jnp.minimum(g, SWIGLU_LIMIT)
        u = jnp.clip(u, -SWIGLU_LIMIT, SWIGLU_LIMIT)
        a = g * _sigmoid(SWIGLU_ALPHA * g) * (u + 1.0)
        acc[...] += jnp.dot(a.astype(BF16), wd_ref[0], preferred_element_type=F32)

        @pl.when(f == n_f - 1)
        def _():
            o_ref[...] = acc[...] + bd_ref[0]

    @pl.when((b >= n_used) & (f == n_f - 1))
    def _():
        o_ref[...] = jnp.zeros_like(o_ref)


def _experts(blk_e, n_used, row_tok, h2, wg, wu, wd, bg, bu, bd, tm, tf):
    n_rows = row_tok.shape[0]
    n_blocks = n_rows // tm
    F = EXPERT_FF
    n_f = F // tf

    def f_eff(b, f, nu):
        return jnp.where(b < nu[0], f, n_f - 1)

    grid_spec = pltpu.PrefetchScalarGridSpec(
        num_scalar_prefetch=3,
        grid=(n_blocks, n_f),
        in_specs=[
            pl.BlockSpec(memory_space=pl.ANY),
            pl.BlockSpec((1, D_MODEL, tf), lambda b, f, be, nu, rt: (be[b], 0, f_eff(b, f, nu))),
            pl.BlockSpec((1, D_MODEL, tf), lambda b, f, be, nu, rt: (be[b], 0, f_eff(b, f, nu))),
            pl.BlockSpec((1, tf, D_MODEL), lambda b, f, be, nu, rt: (be[b], f_eff(b, f, nu), 0)),
            pl.BlockSpec((1, 1, tf), lambda b, f, be, nu, rt: (be[b], 0, f_eff(b, f, nu))),
            pl.BlockSpec((1, 1, tf), lambda b, f, be, nu, rt: (be[b], 0, f_eff(b, f, nu))),
            pl.BlockSpec((1, 1, D_MODEL), lambda b, f, be, nu, rt: (be[b], 0, 0)),
        ],
        out_specs=pl.BlockSpec((tm, D_MODEL), lambda b, f, be, nu, rt: (b, 0)),
        scratch_shapes=[
            pltpu.VMEM((tm, D_MODEL), F32),
            pltpu.VMEM((tm, D_MODEL), BF16),
            pltpu.VMEM((tm, D_MODEL), F32),
            pltpu.SemaphoreType.DMA((1,)),
        ],
    )
    return pl.pallas_call(
        _expert_kernel,
        grid_spec=grid_spec,
        out_shape=jax.ShapeDtypeStruct((n_rows, D_MODEL), F32),
        compiler_params=_cparams(("arbitrary", "arbitrary"), 56),
    )(blk_e, n_used, row_tok, h2, wg, wu, wd, bg, bu, bd)


def _combine_kernel(dest_ref, yb_hbm, gate_ref, x1_ref, p_ref, wpg_ref, wpp_ref, g3_ref, gf_ref,
                    o_ref, gbuf, sem):
    i = pl.program_id(0)
    n = pl.num_programs(0)
    tm = x1_ref.shape[0]

    def issue(tile):
        base = tile * tm * TOP_K

        def body(r, c):
            for k in range(TOP_K):
                row = dest_ref[base + r * TOP_K + k]
                pltpu.make_async_copy(yb_hbm.at[pl.ds(row, 1), :], gbuf.at[k, pl.ds(r, 1), :], sem.at[0]).start()
            return c

        lax.fori_loop(0, tm, body, 0)

    @pl.when(i == 0)
    def _():
        issue(0)

    for k in range(TOP_K):
        pltpu.make_async_copy(yb_hbm.at[pl.ds(0, tm), :], gbuf.at[k], sem.at[0]).wait()
    gates = gate_ref[...]
    moe = gates[:, 0:1] * gbuf[0]
    for k in range(1, TOP_K):
        moe = moe + gates[:, k:k + 1] * gbuf[k]
    x2 = x1_ref[...] + moe

    @pl.when(i + 1 < n)
    def _():
        issue(i + 1)

    h3 = _rms(x2, g3_ref[...]).astype(BF16)
    pg = _sigmoid(jnp.dot(h3, wpg_ref[...], preferred_element_type=F32))
    pp = jnp.dot(p_ref[...].astype(BF16), wpp_ref[...], preferred_element_type=F32)
    x3 = x2 + pg * pp
    o_ref[...] = _rms(x3, gf_ref[...])


def _combine(dest, yb, gates, x1, p2, wpg, wpp, g3, gf, tm):
    L = x1.shape[0]
    const = lambda *shape: pl.BlockSpec(shape, lambda i, d: (0,) * len(shape), pipeline_mode=pl.Buffered(1))
    row = lambda w: pl.BlockSpec((tm, w), lambda i, d: (i, 0))
    grid_spec = pltpu.PrefetchScalarGridSpec(
        num_scalar_prefetch=1,
        grid=(L // tm,),
        in_specs=[
            pl.BlockSpec(memory_space=pl.ANY),
            row(ROUTER_PAD), row(D_MODEL), row(PLE_DIM),
            const(D_MODEL, D_MODEL), const(PLE_DIM, D_MODEL), const(1, D_MODEL), const(1, D_MODEL),
        ],
        out_specs=row(D_MODEL),
        scratch_shapes=[
            pltpu.VMEM((TOP_K, tm, D_MODEL), F32),
            pltpu.SemaphoreType.DMA((1,)),
        ],
    )
    return pl.pallas_call(
        _combine_kernel,
        grid_spec=grid_spec,
        out_shape=jax.ShapeDtypeStruct((L, D_MODEL), F32),
        compiler_params=_cparams(("arbitrary",), 48),
    )(dest, yb, gates, x1, p2, wpg, wpp, g3, gf)


def _tile(L, pref):
    return min(pref, L)


def _layer(x2, p2, pos, norm1_g, w_in, lam_re, lam_im, log_dt, b_re, b_im, c_re, c_im, d_skip, w_glu,
           b_glu, w_bs, w_br, w_out, norm2_g, router_w, router_b, w_gu, b_gu, w_dn, b_dn, norm3_g,
           w_pg, w_pp, out_g):
    L = x2.shape[0]
    G, N, E = SSM_GROUPS, SSM_STATE, N_EXPERTS

    cuts = SSM_WIDTH + 4 * RET_WIDTH
    w_in_bf = jnp.concatenate([w_in[:, cuts:], w_in[:, :cuts]], axis=1).astype(BF16)
    z = _inproj(x2, norm1_g[None, :], w_in_bf, _tile(L, 1024), 1024)
    ms_blk, mr_blk = 0, 1
    u_blk, q_blk, k_blk, v_blk, g_blk = 4, 5, 6, 7, 8

    ab_re, ab_im, bb_re, bb_im = _s5_prep(lam_re, lam_im, log_dt, b_re, b_im)
    eye = jnp.eye(SUBLANES, dtype=F32)
    gpb = LANES // SSM_GROUP
    def bpack(bb):
        t = bb.reshape(N_UBLK, gpb, SSM_GROUP, N)
        return jnp.einsum('bgcn,gh->bgchn', t, eye).reshape(N_UBLK, LANES, ST_PER_UBLK)
    def cpack(c):
        t = c.reshape(N_UBLK, gpb, SSM_GROUP, N)
        return jnp.einsum('bgcn,gh->bgnhc', t, eye).reshape(N_UBLK, ST_PER_UBLK, LANES)
    bblk = jnp.concatenate([bpack(bb_re), bpack(bb_im)], axis=-1).astype(BF16)
    cblk = jnp.concatenate([cpack(c_re), -cpack(c_im)], axis=1).astype(BF16)
    atile = lambda a: a[::SSM_GROUP].reshape(N_SVREG, SUBLANES, LANES)
    ys = _s5(z, u_blk, bblk, cblk, atile(ab_re), atile(ab_im), d_skip.reshape(1, SSM_WIDTH),
             w_glu.astype(BF16), b_glu[None, :], _tile(L, 512))

    yr = _retention(z, (q_blk, k_blk, v_blk, g_blk), pos, _tile(L, 512))

    rw = jnp.pad(router_w, ((0, 0), (0, ROUTER_PAD - E)))
    rwh = rw.astype(BF16)
    rwl = (rw - rwh.astype(F32)).astype(BF16)
    rb = jnp.pad(router_b, (0, ROUTER_PAD - E), constant_values=NEG_BIG)[None, :]
    x1, h2, idx, gates, rank, cnt = _merge(
        ys, yr, z, ms_blk, mr_blk, x2, w_bs.astype(BF16), w_br.astype(BF16), w_out.astype(BF16),
        norm2_g[None, :], rwh, rwl, rb, _tile(L, 256))

    tm_e = 512
    n_asg = L * TOP_K
    n_rows = n_asg + E * tm_e
    n_blocks = n_rows // tm_e
    counts = cnt[0, :E]
    pcounts = (counts + tm_e - 1) // tm_e * tm_e
    pend = jnp.cumsum(pcounts)
    pstart = pend - pcounts
    idx4, rank4 = idx[:, :TOP_K], rank[:, :TOP_K]
    dest = (pstart[idx4] + rank4).astype(jnp.int32)
    tok = jnp.broadcast_to(jnp.arange(L, dtype=jnp.int32)[:, None], (L, TOP_K))
    row_tok = jnp.zeros((n_rows,), jnp.int32).at[dest.reshape(-1)].set(tok.reshape(-1))
    blk_start = jnp.arange(n_blocks, dtype=jnp.int32) * tm_e
    blk_e = jnp.minimum(jnp.searchsorted(pend, blk_start, side='right'), E - 1).astype(jnp.int32)
    n_used = (pend[-1] // tm_e).astype(jnp.int32).reshape(1)

    wg = w_gu[:, :, 0::2].astype(BF16)
    wu = w_gu[:, :, 1::2].astype(BF16)
    bg = b_gu[:, None, 0::2]
    bu = b_gu[:, None, 1::2]
    yb = _experts(blk_e, n_used, row_tok, h2, wg, wu, w_dn.astype(BF16), bg, bu, b_dn[:, None, :],
                  tm_e, 512)

    return _combine(dest.reshape(-1), yb, gates, x1, p2, w_pg.astype(BF16), w_pp.astype(BF16),
                    norm3_g[None, :], out_g[None, :], _tile(L, 256))


def kernel(x, p, positions, norm1_g, w_in, ssm_lam_re, ssm_lam_im, ssm_log_dt, ssm_b_re, ssm_b_im, ssm_c_re, ssm_c_im, ssm_d, ssm_w_glu, ssm_b_glu, w_branch_ssm, w_branch_ret, w_out, norm2_g, router_w, router_b, exp_w_gate_up, exp_b_gate_up, exp_w_down, exp_b_down, norm3_g, ple_w_gate, ple_w_proj, final_norm_g):
    bsz, L, d = x.shape
    depth = w_in.shape[0]
    assert bsz == 1 and depth == 1 and d == D_MODEL
    out = _layer(
        x[0], p[0, 0], positions.reshape(L, 1), norm1_g[0], w_in[0], ssm_lam_re[0], ssm_lam_im[0],
        ssm_log_dt[0], ssm_b_re[0], ssm_b_im[0], ssm_c_re[0], ssm_c_im[0], ssm_d[0], ssm_w_glu[0],
        ssm_b_glu[0], w_branch_ssm[0], w_branch_ret[0], w_out[0], norm2_g[0], router_w[0], router_b[0],
        exp_w_gate_up[0], exp_b_gate_up[0], exp_w_down[0], exp_b_down[0], norm3_g[0], ple_w_gate[0],
        ple_w_proj[0], final_norm_g)
    return out[None]
```

```python
import functools
import math

import jax
import jax.numpy as jnp
from jax import lax
from jax.experimental import pallas as pl
from jax.experimental.pallas import tpu as pltpu

F32 = jnp.float32
BF16 = jnp.bfloat16

D_MODEL = 2048
PLE_DIM = 256
SSM_WIDTH = D_MODEL // 2
SSM_GROUP = 16
SSM_GROUPS = SSM_WIDTH // SSM_GROUP
SSM_STATE = 64
RET_HEADS = 8
RET_HEAD_DIM = 128
RET_WIDTH = RET_HEADS * RET_HEAD_DIM
RET_CHUNK = 128
ROPE_BASE = 10000.0
N_EXPERTS = 32
TOP_K = 4
EXPERT_FF = D_MODEL
SWIGLU_LIMIT = 7.0
SWIGLU_ALPHA = 1.702
NORM_EPS = 1e-6
IN_COLS = SSM_WIDTH + 4 * RET_WIDTH + 2 * D_MODEL

LANES = 128
SUBLANES = 8
ROUTER_PAD = LANES
NEG_BIG = -1e30

MIB = 1024 * 1024


def _cparams(sem, vmem_mib):
    return pltpu.CompilerParams(dimension_semantics=sem, vmem_limit_bytes=vmem_mib * MIB)


def _rms(xf, g):
    ms = jnp.mean(xf * xf, axis=-1, keepdims=True)
    return xf * lax.rsqrt(ms + NORM_EPS) * g


def _sigmoid(x):
    return 1.0 / (1.0 + jnp.exp(-x))


def _gather_rows(src3, dst3, sem, index_of, n_groups, dst_group0, inline=False):
    def body(j, c):
        for u in range(SUBLANES):
            row = index_of(j * SUBLANES + u)
            pltpu.make_async_copy(src3.at[row >> 3, pl.ds(row & (SUBLANES - 1), 1), :],
                                  dst3.at[dst_group0 + j, pl.ds(u, 1), :], sem).start()
        return c

    if inline:
        for j in range(n_groups):
            body(j, 0)
    else:
        lax.fori_loop(0, n_groups, body, 0)


def _inproj_kernel(x_ref, g_ref, w_ref, o_ref, h_scr):
    @pl.when(pl.program_id(1) == 0)
    def _():
        h_scr[...] = _rms(x_ref[...], g_ref[...]).astype(BF16)

    o_ref[...] = jnp.dot(h_scr[...], w_ref[...], preferred_element_type=F32).astype(o_ref.dtype)


def _inproj(x2, g1, w_in_bf, tm, tn):
    L = x2.shape[0]
    n_cols = w_in_bf.shape[1]
    return pl.pallas_call(
        _inproj_kernel,
        name="inproj",
        grid=(L // tm, n_cols // tn),
        in_specs=[
            pl.BlockSpec((tm, D_MODEL), lambda i, j: (i, 0)),
            pl.BlockSpec((1, D_MODEL), lambda i, j: (0, 0)),
            pl.BlockSpec((D_MODEL, tn), lambda i, j: (0, j)),
        ],
        out_specs=pl.BlockSpec((tm, tn), lambda i, j: (i, j)),
        out_shape=jax.ShapeDtypeStruct((L, n_cols), BF16),
        scratch_shapes=[pltpu.VMEM((tm, D_MODEL), BF16)],
        compiler_params=_cparams(("arbitrary", "arbitrary"), 48),
    )(x2, g1, w_in_bf)


def _s5_prep_kernel(lr_ref, li_ref, ldt_ref, bre_ref, bim_ref, abre_ref, abim_ref, bbre_ref, bbim_ref):
    lr, li = lr_ref[...], li_ref[...]
    dt = jnp.exp(ldt_ref[...])
    mag = jnp.exp(lr * dt)
    ab_re, ab_im = mag * jnp.cos(li * dt), mag * jnp.sin(li * dt)
    den = lr * lr + li * li
    nr, ni = ab_re - 1.0, ab_im
    f_re = (nr * lr + ni * li) / den
    f_im = (ni * lr - nr * li) / den
    bre, bim = bre_ref[...], bim_ref[...]
    abre_ref[...] = ab_re
    abim_ref[...] = ab_im
    bbre_ref[...] = f_re * bre - f_im * bim
    bbim_ref[...] = f_re * bim + f_im * bre


def _s5_prep(lam_re, lam_im, log_dt, b_re, b_im):
    rep = lambda a: jnp.repeat(a, SSM_GROUP, axis=0)
    bt = lambda b: jnp.transpose(b, (0, 2, 1)).reshape(SSM_WIDTH, SSM_STATE)
    shp = jax.ShapeDtypeStruct((SSM_WIDTH, SSM_STATE), F32)
    return pl.pallas_call(_s5_prep_kernel, name="s5_prep", out_shape=(shp, shp, shp, shp))(
        rep(lam_re), rep(lam_im), rep(log_dt[:, None]), bt(b_re), bt(b_im))


def _gelu_tanh(x):
    return 0.5 * x * (1.0 + jnp.tanh(math.sqrt(2.0 / math.pi) * (x + 0.044715 * (x * x * x))))


N_UBLK = SSM_WIDTH // LANES
ST_PER_UBLK = (LANES // SSM_GROUP) * SSM_STATE
TILES_PER_UBLK = ST_PER_UBLK // LANES
N_SVREG = SSM_GROUPS * SSM_STATE // (LANES * SUBLANES)


def _s5_kernel(u_ref, bblk_ref, cblk_ref, are_ref, aim_ref, d_ref, wglu_ref, bglu_ref, o_ref,
               sre, sim, carry):
    tm = u_ref.shape[0]

    @pl.when(pl.program_id(0) == 0)
    def _():
        carry[...] = jnp.zeros_like(carry)

    u = u_ref[...]
    for b in range(N_UBLK):
        bu = jnp.dot(u[:, b * LANES:(b + 1) * LANES], bblk_ref[b], preferred_element_type=F32)
        v = (b * TILES_PER_UBLK) // SUBLANES
        for q in range(TILES_PER_UBLK):
            k = (b * TILES_PER_UBLK + q) % SUBLANES
            sre[v, pl.ds(k, tm, stride=SUBLANES), :] = bu[:, q * LANES:(q + 1) * LANES]
            sim[v, pl.ds(k, tm, stride=SUBLANES), :] = bu[:, ST_PER_UBLK + q * LANES:ST_PER_UBLK + (q + 1) * LANES]

    are, aim = are_ref[...], aim_ref[...]

    def step(t, c):
        cre, cim = c
        r0 = pl.multiple_of(t * SUBLANES, SUBLANES)
        bre = sre[:, pl.ds(r0, SUBLANES), :]
        bim = sim[:, pl.ds(r0, SUBLANES), :]
        nre = are * cre - aim * cim + bre
        nim = are * cim + aim * cre + bim
        sre[:, pl.ds(r0, SUBLANES), :] = nre
        sim[:, pl.ds(r0, SUBLANES), :] = nim
        return nre, nim

    cre, cim = lax.fori_loop(0, tm, step, (carry[0], carry[1]), unroll=4)
    carry[0] = cre
    carry[1] = cim

    ys = []
    for b in range(N_UBLK):
        v = (b * TILES_PER_UBLK) // SUBLANES
        k0 = (b * TILES_PER_UBLK) % SUBLANES
        parts = [sre[v, pl.ds(k0 + q, tm, stride=SUBLANES), :] for q in range(TILES_PER_UBLK)]
        parts += [sim[v, pl.ds(k0 + q, tm, stride=SUBLANES), :] for q in range(TILES_PER_UBLK)]
        sb = jnp.concatenate(parts, axis=-1).astype(BF16)
        ys.append(jnp.dot(sb, cblk_ref[b], preferred_element_type=F32))
    y = jnp.concatenate(ys, axis=-1) + d_ref[...] * u.astype(F32)
    y = _gelu_tanh(y)
    gl = jnp.dot(y.astype(BF16), wglu_ref[...], preferred_element_type=F32) + bglu_ref[...]
    o_ref[...] = (y * _sigmoid(gl)).astype(o_ref.dtype)


def _s5(z, u_blk, bblk, cblk, are, aim, d_skip, wglu, bglu, tm):
    L = z.shape[0]
    const = lambda *shape: pl.BlockSpec(shape, lambda i: (0,) * len(shape))
    return pl.pallas_call(
        _s5_kernel,
        name="s5_branch",
        grid=(L // tm,),
        in_specs=[
            pl.BlockSpec((tm, SSM_WIDTH), lambda i: (i, u_blk)),
            const(N_UBLK, LANES, 2 * ST_PER_UBLK),
            const(N_UBLK, 2 * ST_PER_UBLK, LANES),
            const(N_SVREG, SUBLANES, LANES),
            const(N_SVREG, SUBLANES, LANES),
            const(1, SSM_WIDTH),
            const(SSM_WIDTH, SSM_WIDTH),
            const(1, SSM_WIDTH),
        ],
        out_specs=pl.BlockSpec((tm, SSM_WIDTH), lambda i: (i, 0)),
        out_shape=jax.ShapeDtypeStruct((L, SSM_WIDTH), BF16),
        scratch_shapes=[
            pltpu.VMEM((N_SVREG, tm * SUBLANES, LANES), F32),
            pltpu.VMEM((N_SVREG, tm * SUBLANES, LANES), F32),
            pltpu.VMEM((2, N_SVREG, SUBLANES, LANES), F32),
        ],
        compiler_params=_cparams(("arbitrary",), 48),
    )(z, bblk, cblk, are, aim, d_skip, wglu, bglu)


def _ret_kernel(q_ref, k_ref, v_ref, g_ref, pos_ref, inv_ref, decay_ref, zeta_ref, xi_ref, cd_ref,
                o_ref, r_scr):
    tm = q_ref.shape[0]
    C, dk = RET_CHUNK, RET_HEAD_DIM

    @pl.when(pl.program_id(0) == 0)
    def _():
        r_scr[...] = jnp.zeros_like(r_scr)

    ang = pos_ref[...].astype(F32) * inv_ref[...]
    cs, sn = jnp.cos(ang), jnp.sin(ang)
    cf = jnp.concatenate([cs, cs], axis=-1)
    sf = jnp.concatenate([-sn, sn], axis=-1)

    def rot(xh):
        return xh * cf + pltpu.roll(xh, dk // 2, axis=1) * sf

    nt = (((1,), (1,)), ((), ()))
    for h in range(RET_HEADS):
        cols = slice(h * dk, (h + 1) * dk)
        qr = rot(q_ref[:, cols].astype(F32))
        kr = rot(k_ref[:, cols].astype(F32))
        for n in range(tm // C):
            rows = slice(n * C, (n + 1) * C)
            qc = qr[rows].astype(BF16)
            kcf = kr[rows]
            vc = v_ref[rows, cols]
            sc = lax.dot_general(qc, kcf.astype(BF16), nt, preferred_element_type=F32) * decay_ref[h]
            inner = jnp.dot(sc.astype(BF16), vc, preferred_element_type=F32)
            r_prev = r_scr[h]
            cross = jnp.dot(qc, r_prev.astype(BF16), preferred_element_type=F32) * xi_ref[h]
            o = inner + cross
            kzt = jnp.transpose(kcf * zeta_ref[h]).astype(BF16)
            r_scr[h] = r_prev * cd_ref[h] + jnp.dot(kzt, vc, preferred_element_type=F32)
            mu = jnp.mean(o, axis=-1, keepdims=True)
            oc = o - mu
            on = oc * lax.rsqrt(jnp.mean(oc * oc, axis=-1, keepdims=True) + NORM_EPS)
            gate = g_ref[rows, cols].astype(F32)
            o_ref[rows, cols] = (gate * _sigmoid(gate) * on).astype(o_ref.dtype)


def _retention(z, blks, pos, tm):
    L = z.shape[0]
    H, C, dk = RET_HEADS, RET_CHUNK, RET_HEAD_DIM
    half = dk // 2
    inv = (ROPE_BASE ** (-jnp.arange(half, dtype=F32) / half))[None, :]
    log_gamma = jnp.log(1.0 - jnp.exp2(-5.0 - jnp.arange(H, dtype=F32)))
    idx = jnp.arange(C, dtype=F32)
    rel = idx[:, None] - idx[None, :]
    scale = dk ** -0.5
    decay = jnp.where(rel >= 0, jnp.exp(jnp.maximum(rel, 0.0)[None] * log_gamma[:, None, None]), 0.0) * scale
    zeta = jnp.exp((C - 1.0 - idx)[None, :] * log_gamma[:, None]) * scale
    xi = jnp.exp((idx + 1.0)[None, :] * log_gamma[:, None])
    bc = lambda a: jnp.broadcast_to(a[:, :, None], (H, C, dk))
    cd = jnp.broadcast_to(jnp.exp(C * log_gamma)[:, None, None], (H, 1, dk))
    const = lambda *shape: pl.BlockSpec(shape, lambda i: (0,) * len(shape))
    qb, kb, vb, gb = blks
    return pl.pallas_call(
        _ret_kernel,
        name="retention",
        grid=(L // tm,),
        in_specs=[
            pl.BlockSpec((tm, RET_WIDTH), lambda i: (i, qb)),
            pl.BlockSpec((tm, RET_WIDTH), lambda i: (i, kb)),
            pl.BlockSpec((tm, RET_WIDTH), lambda i: (i, vb)),
            pl.BlockSpec((tm, RET_WIDTH), lambda i: (i, gb)),
            pl.BlockSpec((tm, 1), lambda i: (i, 0)),
            const(1, half),
            const(H, C, C),
            const(H, C, dk),
            const(H, C, dk),
            const(H, 1, dk),
        ],
        out_specs=pl.BlockSpec((tm, RET_WIDTH), lambda i: (i, 0)),
        out_shape=jax.ShapeDtypeStruct((L, RET_WIDTH), BF16),
        scratch_shapes=[pltpu.VMEM((H, dk, dk), F32)],
        compiler_params=_cparams(("arbitrary",), 48),
    )(z, z, z, z, pos, inv, decay, bc(zeta), bc(xi), cd)


def _merge_kernel(ys_ref, yr_ref, ms_ref, mr_ref, x_ref, ps_ref, pr_ref, wo_ref, g2_ref,
                  rwh_ref, rwl_ref, rb_ref, tril_ref,
                  x1_ref, h2_ref, idx_ref, gate_ref, rank_ref, cnt_ref, carry):
    tm = x_ref.shape[0]

    @pl.when(pl.program_id(0) == 0)
    def _():
        carry[...] = jnp.zeros_like(carry)

    a = jnp.dot(ys_ref[...], ps_ref[...], preferred_element_type=F32)
    b = jnp.dot(yr_ref[...], pr_ref[...], preferred_element_type=F32)
    merged = _sigmoid(ms_ref[...].astype(F32)) * a + _sigmoid(mr_ref[...].astype(F32)) * b
    x1 = x_ref[...] + jnp.dot(merged.astype(BF16), wo_ref[...], preferred_element_type=F32)
    x1_ref[...] = x1
    h2 = _rms(x1, g2_ref[...])
    h2_ref[...] = h2

    hh = h2.astype(BF16)
    hl = (h2 - hh.astype(F32)).astype(BF16)
    logits = (jnp.dot(hh, rwh_ref[...], preferred_element_type=F32)
              + jnp.dot(hh, rwl_ref[...], preferred_element_type=F32)
              + jnp.dot(hl, rwh_ref[...], preferred_element_type=F32)) + rb_ref[...]

    lane = lax.broadcasted_iota(jnp.int32, (tm, ROUTER_PAD), 1)
    work = logits
    vals, idxs = [], []
    for _ in range(TOP_K):
        m = jnp.max(work, axis=-1, keepdims=True)
        ix = jnp.min(jnp.where(work == m, lane, ROUTER_PAD), axis=-1, keepdims=True)
        vals.append(m)
        idxs.append(ix)
        work = jnp.where(lane == ix, -jnp.inf, work)
    es = [jnp.exp(v - vals[0]) for v in vals]
    den = es[0] + es[1] + es[2] + es[3]

    onehot = jnp.zeros((tm, ROUTER_PAD), F32)
    for ix in idxs:
        onehot = onehot + (lane == ix).astype(F32)
    cum = jnp.dot(tril_ref[...], onehot.astype(BF16), preferred_element_type=F32) + carry[...]
    carry[...] = carry[...] + jnp.sum(onehot, axis=0, keepdims=True)
    cnt_ref[...] = jnp.broadcast_to(carry[...], cnt_ref.shape).astype(jnp.int32)

    idx_o = jnp.zeros((tm, ROUTER_PAD), jnp.int32)
    gate_o = jnp.zeros((tm, ROUTER_PAD), F32)
    rank_o = jnp.zeros((tm, ROUTER_PAD), jnp.int32)
    for k in range(TOP_K):
        rk = jnp.sum(jnp.where(lane == idxs[k], cum, 0.0), axis=-1, keepdims=True).astype(jnp.int32)
        idx_o = jnp.where(lane == k, idxs[k], idx_o)
        gate_o = jnp.where(lane == k, es[k] / den, gate_o)
        rank_o = jnp.where(lane == k, rk, rank_o)
    idx_ref[...] = idx_o
    gate_ref[...] = gate_o
    rank_ref[...] = rank_o


def _merge(ys, yr, z, ms_blk, mr_blk, x2, ps, pr, wo, g2, rwh, rwl, rb, tm):
    L = x2.shape[0]
    tril = (jnp.arange(tm)[:, None] > jnp.arange(tm)[None, :]).astype(BF16)
    const = lambda *shape: pl.BlockSpec(shape, lambda i: (0,) * len(shape), pipeline_mode=pl.Buffered(1))
    row = lambda w: pl.BlockSpec((tm, w), lambda i: (i, 0))
    return pl.pallas_call(
        _merge_kernel,
        name="merge_router",
        grid=(L // tm,),
        in_specs=[
            row(SSM_WIDTH), row(RET_WIDTH),
            pl.BlockSpec((tm, D_MODEL), lambda i: (i, ms_blk)),
            pl.BlockSpec((tm, D_MODEL), lambda i: (i, mr_blk)),
            row(D_MODEL),
            const(SSM_WIDTH, D_MODEL), const(RET_WIDTH, D_MODEL), const(D_MODEL, D_MODEL),
            const(1, D_MODEL),
            const(D_MODEL, ROUTER_PAD), const(D_MODEL, ROUTER_PAD), const(1, ROUTER_PAD),
            const(tm, tm),
        ],
        out_specs=[
            row(D_MODEL), row(D_MODEL), row(ROUTER_PAD), row(ROUTER_PAD), row(ROUTER_PAD),
            pl.BlockSpec((SUBLANES, ROUTER_PAD), lambda i: (0, 0)),
        ],
        out_shape=[
            jax.ShapeDtypeStruct((L, D_MODEL), F32),
            jax.ShapeDtypeStruct((L, D_MODEL), F32),
            jax.ShapeDtypeStruct((L, ROUTER_PAD), jnp.int32),
            jax.ShapeDtypeStruct((L, ROUTER_PAD), F32),
            jax.ShapeDtypeStruct((L, ROUTER_PAD), jnp.int32),
            jax.ShapeDtypeStruct((SUBLANES, ROUTER_PAD), jnp.int32),
        ],
        scratch_shapes=[pltpu.VMEM((1, ROUTER_PAD), F32)],
        compiler_params=_cparams(("arbitrary",), 56),
    )(ys, yr, z, z, x2, ps, pr, wo, g2, rwh, rwl, rb, tril)


def _expert_kernel(blk_e_ref, nused_ref, rowtok_ref, h2_hbm, wgu_ref, wd_ref, bgu_ref, bd_ref,
                   o_ref, xbuf, xb, acc, sem):
    b = pl.program_id(0)
    f = pl.program_id(1)
    n_f = pl.num_programs(1)
    tm = xb.shape[0]
    n_used = nused_ref[0]
    share = tm // n_f

    def issue(blk, lo, n, inline):
        base = blk * tm + lo
        _gather_rows(h2_hbm, xbuf, sem.at[0], lambda r: rowtok_ref[base + r], n // SUBLANES,
                     lo // SUBLANES, inline)

    def wait_block():
        pltpu.make_async_copy(h2_hbm.at[pl.ds(0, tm // SUBLANES)], xbuf, sem.at[0]).wait()

    @pl.when(b < n_used)
    def _():
        @pl.when((b == 0) & (f == 0))
        def _():
            issue(0, 0, tm, False)

        @pl.when(f == 0)
        def _():
            wait_block()
            xb[...] = xbuf[...].reshape(tm, D_MODEL).astype(BF16)
            acc[...] = jnp.zeros_like(acc)

        gu = jnp.dot(xb[...], wgu_ref[0], preferred_element_type=F32) + bgu_ref[0]

        issue(jnp.minimum(b + 1, n_used - 1), f * share, share, True)
        n_ch = gu.shape[1] // LANES
        even = (lax.broadcasted_iota(jnp.int32, (tm, LANES), 1) & 1) == 0

        def act(c):
            guc = gu[:, c * LANES:(c + 1) * LANES]
            g = jnp.minimum(guc, SWIGLU_LIMIT)
            u = jnp.clip(pltpu.roll(guc, LANES - 1, axis=1), -SWIGLU_LIMIT, SWIGLU_LIMIT)
            return g * _sigmoid(SWIGLU_ALPHA * g) * (u + 1.0)

        parts = [jnp.where(even, act(2 * c), pltpu.roll(act(2 * c + 1), 1, axis=1))
                 for c in range(n_ch // 2)]
        a = jnp.concatenate(parts, axis=-1).astype(BF16)
        acc[...] += jnp.dot(a, wd_ref[0], preferred_element_type=F32)

        @pl.when(f == n_f - 1)
        def _():
            o_ref[...] = acc[...] + bd_ref[0]

        @pl.when((f == n_f - 1) & (b == n_used - 1))
        def _():
            wait_block()

    @pl.when((b >= n_used) & (f == n_f - 1))
    def _():
        o_ref[...] = jnp.zeros_like(o_ref)


def _permute_down_rows(w_dn):
    E, F, D = w_dn.shape
    pairs = LANES // 2
    t = w_dn.reshape(E, F // LANES, 2, pairs, D)
    return jnp.transpose(t, (0, 1, 3, 2, 4)).reshape(E, F, D)


def _experts(blk_e, n_used, row_tok, h2, wgu, wd, bgu, bd, tm, tf):
    n_rows = row_tok.shape[0]
    n_blocks = n_rows // tm
    F = EXPERT_FF
    n_f = F // tf

    def f_eff(b, f, nu):
        return jnp.where(b < nu[0], f, n_f - 1)

    grid_spec = pltpu.PrefetchScalarGridSpec(
        num_scalar_prefetch=3,
        grid=(n_blocks, n_f),
        in_specs=[
            pl.BlockSpec(memory_space=pl.ANY),
            pl.BlockSpec((1, D_MODEL, 2 * tf), lambda b, f, be, nu, rt: (be[b], 0, f_eff(b, f, nu))),
            pl.BlockSpec((1, tf, D_MODEL), lambda b, f, be, nu, rt: (be[b], f_eff(b, f, nu), 0)),
            pl.BlockSpec((1, 1, 2 * tf), lambda b, f, be, nu, rt: (be[b], 0, f_eff(b, f, nu))),
            pl.BlockSpec((1, 1, D_MODEL), lambda b, f, be, nu, rt: (be[b], 0, 0)),
        ],
        out_specs=pl.BlockSpec((tm, D_MODEL), lambda b, f, be, nu, rt: (b, 0)),
        scratch_shapes=[
            pltpu.VMEM((tm // SUBLANES, SUBLANES, D_MODEL), F32),
            pltpu.VMEM((tm, D_MODEL), BF16),
            pltpu.VMEM((tm, D_MODEL), F32),
            pltpu.SemaphoreType.DMA((1,)),
        ],
    )
    return pl.pallas_call(
        _expert_kernel,
        name="experts",
        grid_spec=grid_spec,
        out_shape=jax.ShapeDtypeStruct((n_rows, D_MODEL), F32),
        compiler_params=_cparams(("arbitrary", "arbitrary"), 56),
    )(blk_e, n_used, row_tok, h2, wgu, wd, bgu, bd)


def _combine_kernel(dest_ref, yb_hbm, gate_ref, x1_ref, p_ref, wpg_ref, wpp_ref, g3_ref, gf_ref,
                    o_ref, gbuf, sem):
    i = pl.program_id(0)
    n = pl.num_programs(0)
    tm = x1_ref.shape[0]

    def issue(tile):
        base = tile * tm * TOP_K

        for k in range(TOP_K):
            _gather_rows(yb_hbm, gbuf.at[k], sem.at[0], lambda r, k=k: dest_ref[base + r * TOP_K + k],
                         tm // SUBLANES, 0)

    @pl.when(i == 0)
    def _():
        issue(0)

    for k in range(TOP_K):
        pltpu.make_async_copy(yb_hbm.at[pl.ds(0, tm // SUBLANES)], gbuf.at[k], sem.at[0]).wait()
    gates = gate_ref[...]
    moe = gates[:, 0:1] * gbuf[0].reshape(tm, D_MODEL)
    for k in range(1, TOP_K):
        moe = moe + gates[:, k:k + 1] * gbuf[k].reshape(tm, D_MODEL)
    x2 = x1_ref[...] + moe

    @pl.when(i + 1 < n)
    def _():
        issue(i + 1)

    h3 = _rms(x2, g3_ref[...]).astype(BF16)
    pg = _sigmoid(jnp.dot(h3, wpg_ref[...], preferred_element_type=F32))
    pp = jnp.dot(p_ref[...].astype(BF16), wpp_ref[...], preferred_element_type=F32)
    x3 = x2 + pg * pp
    o_ref[...] = _rms(x3, gf_ref[...])


def _combine(dest, yb, gates, x1, p2, wpg, wpp, g3, gf, tm):
    L = x1.shape[0]
    const = lambda *shape: pl.BlockSpec(shape, lambda i, d: (0,) * len(shape), pipeline_mode=pl.Buffered(1))
    row = lambda w: pl.BlockSpec((tm, w), lambda i, d: (i, 0))
    grid_spec = pltpu.PrefetchScalarGridSpec(
        num_scalar_prefetch=1,
        grid=(L // tm,),
        in_specs=[
            pl.BlockSpec(memory_space=pl.ANY),
            row(ROUTER_PAD), row(D_MODEL), row(PLE_DIM),
            const(D_MODEL, D_MODEL), const(PLE_DIM, D_MODEL), const(1, D_MODEL), const(1, D_MODEL),
        ],
        out_specs=row(D_MODEL),
        scratch_shapes=[
            pltpu.VMEM((TOP_K, tm // SUBLANES, SUBLANES, D_MODEL), F32),
            pltpu.SemaphoreType.DMA((1,)),
        ],
    )
    return pl.pallas_call(
        _combine_kernel,
        name="combine_ple",
        grid_spec=grid_spec,
        out_shape=jax.ShapeDtypeStruct((L, D_MODEL), F32),
        compiler_params=_cparams(("arbitrary",), 48),
    )(dest, yb, gates, x1, p2, wpg, wpp, g3, gf)


def _tile(L, pref):
    return min(pref, L)


def _layer(x2, p2, pos, norm1_g, w_in, lam_re, lam_im, log_dt, b_re, b_im, c_re, c_im, d_skip, w_glu,
           b_glu, w_bs, w_br, w_out, norm2_g, router_w, router_b, w_gu, b_gu, w_dn, b_dn, norm3_g,
           w_pg, w_pp, out_g):
    L = x2.shape[0]
    G, N, E = SSM_GROUPS, SSM_STATE, N_EXPERTS

    cuts = SSM_WIDTH + 4 * RET_WIDTH
    w_in_bf = jnp.concatenate([w_in[:, cuts:], w_in[:, :cuts]], axis=1).astype(BF16)
    z = _inproj(x2, norm1_g[None, :], w_in_bf, _tile(L, 1024), 1024)
    ms_blk, mr_blk = 0, 1
    u_blk, q_blk, k_blk, v_blk, g_blk = 4, 5, 6, 7, 8

    ab_re, ab_im, bb_re, bb_im = _s5_prep(lam_re, lam_im, log_dt, b_re, b_im)
    eye = jnp.eye(SUBLANES, dtype=F32)
    gpb = LANES // SSM_GROUP
    def bpack(bb):
        t = bb.reshape(N_UBLK, gpb, SSM_GROUP, N)
        return jnp.einsum('bgcn,gh->bgchn', t, eye).reshape(N_UBLK, LANES, ST_PER_UBLK)
    def cpack(c):
        t = c.reshape(N_UBLK, gpb, SSM_GROUP, N)
        return jnp.einsum('bgcn,gh->bgnhc', t, eye).reshape(N_UBLK, ST_PER_UBLK, LANES)
    bblk = jnp.concatenate([bpack(bb_re), bpack(bb_im)], axis=-1).astype(BF16)
    cblk = jnp.concatenate([cpack(c_re), -cpack(c_im)], axis=1).astype(BF16)
    atile = lambda a: a[::SSM_GROUP].reshape(N_SVREG, SUBLANES, LANES)
    ys = _s5(z, u_blk, bblk, cblk, atile(ab_re), atile(ab_im), d_skip.reshape(1, SSM_WIDTH),
             w_glu.astype(BF16), b_glu[None, :], _tile(L, 512))

    yr = _retention(z, (q_blk, k_blk, v_blk, g_blk), pos, _tile(L, 512))

    rw = jnp.pad(router_w, ((0, 0), (0, ROUTER_PAD - E)))
    rwh = rw.astype(BF16)
    rwl = (rw - rwh.astype(F32)).astype(BF16)
    rb = jnp.pad(router_b, (0, ROUTER_PAD - E), constant_values=NEG_BIG)[None, :]
    x1, h2, idx, gates, rank, cnt = _merge(
        ys, yr, z, ms_blk, mr_blk, x2, w_bs.astype(BF16), w_br.astype(BF16), w_out.astype(BF16),
        norm2_g[None, :], rwh, rwl, rb, _tile(L, 256))

    tm_e = 512
    n_asg = L * TOP_K
    n_rows = n_asg + E * tm_e
    n_blocks = n_rows // tm_e
    counts = cnt[0, :E]
    pcounts = (counts + tm_e - 1) // tm_e * tm_e
    pend = jnp.cumsum(pcounts)
    pstart = pend - pcounts
    idx4, rank4 = idx[:, :TOP_K], rank[:, :TOP_K]
    dest = (pstart[idx4] + rank4).astype(jnp.int32)
    tok = jnp.broadcast_to(jnp.arange(L, dtype=jnp.int32)[:, None], (L, TOP_K))
    row_tok = jnp.zeros((n_rows,), jnp.int32).at[dest.reshape(-1)].set(tok.reshape(-1))
    blk_start = jnp.arange(n_blocks, dtype=jnp.int32) * tm_e
    blk_e = jnp.minimum(jnp.sum(blk_start[:, None] >= pend[None, :], axis=1), E - 1).astype(jnp.int32)
    n_used = (pend[-1] // tm_e).astype(jnp.int32).reshape(1)

    tf = 512
    grouped = lambda a: a.reshape(a.shape[0] // SUBLANES, SUBLANES, D_MODEL)
    yb = _experts(blk_e, n_used, row_tok, grouped(h2), w_gu.astype(BF16),
                  _permute_down_rows(w_dn).astype(BF16), b_gu[:, None, :], b_dn[:, None, :], tm_e, tf)

    return _combine(dest.reshape(-1), grouped(yb), gates, x1, p2, w_pg.astype(BF16), w_pp.astype(BF16),
                    norm3_g[None, :], out_g[None, :], _tile(L, 256))


def kernel(x, p, positions, norm1_g, w_in, ssm_lam_re, ssm_lam_im, ssm_log_dt, ssm_b_re, ssm_b_im, ssm_c_re, ssm_c_im, ssm_d, ssm_w_glu, ssm_b_glu, w_branch_ssm, w_branch_ret, w_out, norm2_g, router_w, router_b, exp_w_gate_up, exp_b_gate_up, exp_w_down, exp_b_down, norm3_g, ple_w_gate, ple_w_proj, final_norm_g):
    bsz, L, d = x.shape
    depth = w_in.shape[0]
    assert bsz == 1 and depth == 1 and d == D_MODEL
    out = _layer(
        x[0], p[0, 0], positions.reshape(L, 1), norm1_g[0], w_in[0], ssm_lam_re[0], ssm_lam_im[0],
        ssm_log_dt[0], ssm_b_re[0], ssm_b_im[0], ssm_c_re[0], ssm_c_im[0], ssm_d[0], ssm_w_glu[0],
        ssm_b_glu[0], w_branch_ssm[0], w_branch_ret[0], w_out[0], norm2_g[0], router_w[0], router_b[0],
        exp_w_gate_up[0], exp_b_gate_up[0], exp_w_down[0], exp_b_down[0], norm3_g[0], ple_w_gate[0],
        ple_w_proj[0], final_norm_g)
    return out[None]
```

```python
import functools
import math

import jax
import jax.numpy as jnp
from jax import lax
from jax.experimental import pallas as pl
from jax.experimental.pallas import tpu as pltpu

F32 = jnp.float32
BF16 = jnp.bfloat16

D_MODEL = 2048
PLE_DIM = 256
SSM_WIDTH = D_MODEL // 2
SSM_GROUP = 16
SSM_GROUPS = SSM_WIDTH // SSM_GROUP
SSM_STATE = 64
RET_HEADS = 8
RET_HEAD_DIM = 128
RET_WIDTH = RET_HEADS * RET_HEAD_DIM
RET_CHUNK = 128
ROPE_BASE = 10000.0
N_EXPERTS = 32
TOP_K = 4
EXPERT_FF = D_MODEL
SWIGLU_LIMIT = 7.0
SWIGLU_ALPHA = 1.702
NORM_EPS = 1e-6
IN_COLS = SSM_WIDTH + 4 * RET_WIDTH + 2 * D_MODEL

LANES = 128
SUBLANES = 8
MXU_DIM = 256
ROUTER_PAD = LANES
NEG_BIG = -1e30

MIB = 1024 * 1024


def _cparams(sem, vmem_mib):
    return pltpu.CompilerParams(dimension_semantics=sem, vmem_limit_bytes=vmem_mib * MIB)


def _rms(xf, g):
    ms = jnp.mean(xf * xf, axis=-1, keepdims=True)
    return xf * lax.rsqrt(ms + NORM_EPS) * g


def _sigmoid(x):
    return 1.0 / (1.0 + jnp.exp(-x))


def _gather_rows(src3, dst3, sem, index_of, n_groups, dst_group0, inline=False):
    def body(j, c):
        for u in range(SUBLANES):
            row = index_of(j * SUBLANES + u)
            pltpu.make_async_copy(src3.at[row >> 3, pl.ds(row & (SUBLANES - 1), 1), :],
                                  dst3.at[dst_group0 + j, pl.ds(u, 1), :], sem).start()
        return c

    if inline:
        for j in range(n_groups):
            body(j, 0)
    else:
        lax.fori_loop(0, n_groups, body, 0)


def _inproj_kernel(x_ref, g_ref, w_ref, o_ref, h_scr):
    @pl.when(pl.program_id(1) == 0)
    def _():
        h_scr[...] = _rms(x_ref[...], g_ref[...]).astype(BF16)

    o_ref[...] = jnp.dot(h_scr[...], w_ref[...], preferred_element_type=F32).astype(o_ref.dtype)


def _inproj(x2, g1, w_in_bf, tm, tn):
    L = x2.shape[0]
    n_cols = w_in_bf.shape[1]
    return pl.pallas_call(
        _inproj_kernel,
        name="inproj",
        grid=(L // tm, n_cols // tn),
        in_specs=[
            pl.BlockSpec((tm, D_MODEL), lambda i, j: (i, 0)),
            pl.BlockSpec((1, D_MODEL), lambda i, j: (0, 0)),
            pl.BlockSpec((D_MODEL, tn), lambda i, j: (0, j)),
        ],
        out_specs=pl.BlockSpec((tm, tn), lambda i, j: (i, j)),
        out_shape=jax.ShapeDtypeStruct((L, n_cols), BF16),
        scratch_shapes=[pltpu.VMEM((tm, D_MODEL), BF16)],
        compiler_params=_cparams(("arbitrary", "arbitrary"), 48),
    )(x2, g1, w_in_bf)


def _s5_prep_kernel(lr_ref, li_ref, ldt_ref, bre_ref, bim_ref, abre_ref, abim_ref, bbre_ref, bbim_ref):
    lr, li = lr_ref[...], li_ref[...]
    dt = jnp.exp(ldt_ref[...])
    mag = jnp.exp(lr * dt)
    ab_re, ab_im = mag * jnp.cos(li * dt), mag * jnp.sin(li * dt)
    den = lr * lr + li * li
    nr, ni = ab_re - 1.0, ab_im
    f_re = (nr * lr + ni * li) / den
    f_im = (ni * lr - nr * li) / den
    bre, bim = bre_ref[...], bim_ref[...]
    abre_ref[...] = ab_re
    abim_ref[...] = ab_im
    bbre_ref[...] = f_re * bre - f_im * bim
    bbim_ref[...] = f_re * bim + f_im * bre


def _s5_prep(lam_re, lam_im, log_dt, b_re, b_im):
    rep = lambda a: jnp.repeat(a, SSM_GROUP, axis=0)
    bt = lambda b: jnp.transpose(b, (0, 2, 1)).reshape(SSM_WIDTH, SSM_STATE)
    shp = jax.ShapeDtypeStruct((SSM_WIDTH, SSM_STATE), F32)
    return pl.pallas_call(_s5_prep_kernel, name="s5_prep", out_shape=(shp, shp, shp, shp))(
        rep(lam_re), rep(lam_im), rep(log_dt[:, None]), bt(b_re), bt(b_im))


def _gelu_tanh(x):
    return 0.5 * x * (1.0 + jnp.tanh(math.sqrt(2.0 / math.pi) * (x + 0.044715 * (x * x * x))))


N_UBLK = SSM_WIDTH // LANES
ST_PER_UBLK = (LANES // SSM_GROUP) * SSM_STATE
TILES_PER_UBLK = ST_PER_UBLK // LANES
N_SVREG = SSM_GROUPS * SSM_STATE // (LANES * SUBLANES)


def _s5_kernel(u_ref, bblk_ref, cblk_ref, are_ref, aim_ref, d_ref, wglu_ref, bglu_ref, o_ref,
               sre, sim, carry):
    tm = u_ref.shape[0]

    @pl.when(pl.program_id(0) == 0)
    def _():
        carry[...] = jnp.zeros_like(carry)

    u = u_ref[...]
    for b in range(N_UBLK):
        bu = jnp.dot(u[:, b * LANES:(b + 1) * LANES], bblk_ref[b], preferred_element_type=F32)
        v = (b * TILES_PER_UBLK) // SUBLANES
        for q in range(TILES_PER_UBLK):
            k = (b * TILES_PER_UBLK + q) % SUBLANES
            sre[v, pl.ds(k, tm, stride=SUBLANES), :] = bu[:, q * LANES:(q + 1) * LANES]
            sim[v, pl.ds(k, tm, stride=SUBLANES), :] = bu[:, ST_PER_UBLK + q * LANES:ST_PER_UBLK + (q + 1) * LANES]

    are, aim = are_ref[...], aim_ref[...]

    def step(t, c):
        cre, cim = c
        r0 = pl.multiple_of(t * SUBLANES, SUBLANES)
        bre = sre[:, pl.ds(r0, SUBLANES), :]
        bim = sim[:, pl.ds(r0, SUBLANES), :]
        nre = are * cre - aim * cim + bre
        nim = are * cim + aim * cre + bim
        sre[:, pl.ds(r0, SUBLANES), :] = nre
        sim[:, pl.ds(r0, SUBLANES), :] = nim
        return nre, nim

    cre, cim = lax.fori_loop(0, tm, step, (carry[0], carry[1]), unroll=4)
    carry[0] = cre
    carry[1] = cim

    ys = []
    for b in range(N_UBLK):
        v = (b * TILES_PER_UBLK) // SUBLANES
        k0 = (b * TILES_PER_UBLK) % SUBLANES
        parts = [sre[v, pl.ds(k0 + q, tm, stride=SUBLANES), :] for q in range(TILES_PER_UBLK)]
        parts += [sim[v, pl.ds(k0 + q, tm, stride=SUBLANES), :] for q in range(TILES_PER_UBLK)]
        sb = jnp.concatenate(parts, axis=-1).astype(BF16)
        ys.append(jnp.dot(sb, cblk_ref[b], preferred_element_type=F32))
    y = jnp.concatenate(ys, axis=-1) + d_ref[...] * u.astype(F32)
    y = _gelu_tanh(y)
    gl = jnp.dot(y.astype(BF16), wglu_ref[...], preferred_element_type=F32) + bglu_ref[...]
    o_ref[...] = (y * _sigmoid(gl)).astype(o_ref.dtype)


def _s5(z, u_blk, bblk, cblk, are, aim, d_skip, wglu, bglu, tm):
    L = z.shape[0]
    const = lambda *shape: pl.BlockSpec(shape, lambda i: (0,) * len(shape))
    return pl.pallas_call(
        _s5_kernel,
        name="s5_branch",
        grid=(L // tm,),
        in_specs=[
            pl.BlockSpec((tm, SSM_WIDTH), lambda i: (i, u_blk)),
            const(N_UBLK, LANES, 2 * ST_PER_UBLK),
            const(N_UBLK, 2 * ST_PER_UBLK, LANES),
            const(N_SVREG, SUBLANES, LANES),
            const(N_SVREG, SUBLANES, LANES),
            const(1, SSM_WIDTH),
            const(SSM_WIDTH, SSM_WIDTH),
            const(1, SSM_WIDTH),
        ],
        out_specs=pl.BlockSpec((tm, SSM_WIDTH), lambda i: (i, 0)),
        out_shape=jax.ShapeDtypeStruct((L, SSM_WIDTH), BF16),
        scratch_shapes=[
            pltpu.VMEM((N_SVREG, tm * SUBLANES, LANES), F32),
            pltpu.VMEM((N_SVREG, tm * SUBLANES, LANES), F32),
            pltpu.VMEM((2, N_SVREG, SUBLANES, LANES), F32),
        ],
        compiler_params=_cparams(("arbitrary",), 48),
    )(z, bblk, cblk, are, aim, d_skip, wglu, bglu)


def _ret_kernel(q_ref, k_ref, v_ref, g_ref, pos_ref, inv_ref, decay_ref, zeta_ref, xi_ref, cd_ref,
                o_ref, r_scr):
    tm = q_ref.shape[0]
    C, dk = RET_CHUNK, RET_HEAD_DIM

    @pl.when(pl.program_id(0) == 0)
    def _():
        r_scr[...] = jnp.zeros_like(r_scr)

    ang = pos_ref[...].astype(F32) * inv_ref[...]
    cs, sn = jnp.cos(ang), jnp.sin(ang)
    cf = jnp.concatenate([cs, cs], axis=-1)
    sf = jnp.concatenate([-sn, sn], axis=-1)

    def rot(xh):
        return xh * cf + pltpu.roll(xh, dk // 2, axis=1) * sf

    nt = (((1,), (1,)), ((), ()))
    for h in range(RET_HEADS):
        cols = slice(h * dk, (h + 1) * dk)
        qr = rot(q_ref[:, cols].astype(F32))
        kr = rot(k_ref[:, cols].astype(F32))
        for n in range(tm // C):
            rows = slice(n * C, (n + 1) * C)
            qc = qr[rows].astype(BF16)
            kcf = kr[rows]
            vc = v_ref[rows, cols]
            sc = lax.dot_general(qc, kcf.astype(BF16), nt, preferred_element_type=F32) * decay_ref[h]
            inner = jnp.dot(sc.astype(BF16), vc, preferred_element_type=F32)
            r_prev = r_scr[h]
            cross = jnp.dot(qc, r_prev.astype(BF16), preferred_element_type=F32) * xi_ref[h]
            o = inner + cross
            kzt = jnp.transpose(kcf * zeta_ref[h]).astype(BF16)
            r_scr[h] = r_prev * cd_ref[h] + jnp.dot(kzt, vc, preferred_element_type=F32)
            mu = jnp.mean(o, axis=-1, keepdims=True)
            oc = o - mu
            on = oc * lax.rsqrt(jnp.mean(oc * oc, axis=-1, keepdims=True) + NORM_EPS)
            gate = g_ref[rows, cols].astype(F32)
            o_ref[rows, cols] = (gate * _sigmoid(gate) * on).astype(o_ref.dtype)


def _retention(z, blks, pos, tm):
    L = z.shape[0]
    H, C, dk = RET_HEADS, RET_CHUNK, RET_HEAD_DIM
    half = dk // 2
    inv = (ROPE_BASE ** (-jnp.arange(half, dtype=F32) / half))[None, :]
    log_gamma = jnp.log(1.0 - jnp.exp2(-5.0 - jnp.arange(H, dtype=F32)))
    idx = jnp.arange(C, dtype=F32)
    rel = idx[:, None] - idx[None, :]
    scale = dk ** -0.5
    decay = jnp.where(rel >= 0, jnp.exp(jnp.maximum(rel, 0.0)[None] * log_gamma[:, None, None]), 0.0) * scale
    zeta = jnp.exp((C - 1.0 - idx)[None, :] * log_gamma[:, None]) * scale
    xi = jnp.exp((idx + 1.0)[None, :] * log_gamma[:, None])
    bc = lambda a: jnp.broadcast_to(a[:, :, None], (H, C, dk))
    cd = jnp.broadcast_to(jnp.exp(C * log_gamma)[:, None, None], (H, 1, dk))
    const = lambda *shape: pl.BlockSpec(shape, lambda i: (0,) * len(shape))
    qb, kb, vb, gb = blks
    return pl.pallas_call(
        _ret_kernel,
        name="retention",
        grid=(L // tm,),
        in_specs=[
            pl.BlockSpec((tm, RET_WIDTH), lambda i: (i, qb)),
            pl.BlockSpec((tm, RET_WIDTH), lambda i: (i, kb)),
            pl.BlockSpec((tm, RET_WIDTH), lambda i: (i, vb)),
            pl.BlockSpec((tm, RET_WIDTH), lambda i: (i, gb)),
            pl.BlockSpec((tm, 1), lambda i: (i, 0)),
            const(1, half),
            const(H, C, C),
            const(H, C, dk),
            const(H, C, dk),
            const(H, 1, dk),
        ],
        out_specs=pl.BlockSpec((tm, RET_WIDTH), lambda i: (i, 0)),
        out_shape=jax.ShapeDtypeStruct((L, RET_WIDTH), BF16),
        scratch_shapes=[pltpu.VMEM((H, dk, dk), F32)],
        compiler_params=_cparams(("arbitrary",), 48),
    )(z, z, z, z, pos, inv, decay, bc(zeta), bc(xi), cd)


def _merge_kernel(ys_ref, yr_ref, ms_ref, mr_ref, x_ref, ps_ref, pr_ref, wo_ref, g2_ref,
                  rwh_ref, rwl_ref, rb_ref, tril_ref,
                  x1_ref, h2_ref, idx_ref, gate_ref, rank_ref, cnt_ref, carry):
    tm = x_ref.shape[0]

    @pl.when(pl.program_id(0) == 0)
    def _():
        carry[...] = jnp.zeros_like(carry)

    a = jnp.dot(ys_ref[...], ps_ref[...], preferred_element_type=F32)
    b = jnp.dot(yr_ref[...], pr_ref[...], preferred_element_type=F32)
    merged = _sigmoid(ms_ref[...].astype(F32)) * a + _sigmoid(mr_ref[...].astype(F32)) * b
    x1 = x_ref[...] + jnp.dot(merged.astype(BF16), wo_ref[...], preferred_element_type=F32)
    x1_ref[...] = x1
    h2 = _rms(x1, g2_ref[...])
    h2_ref[...] = h2.reshape(h2_ref.shape)

    hh = h2.astype(BF16)
    hl = (h2 - hh.astype(F32)).astype(BF16)
    logits = (jnp.dot(hh, rwh_ref[...], preferred_element_type=F32)
              + jnp.dot(hh, rwl_ref[...], preferred_element_type=F32)
              + jnp.dot(hl, rwh_ref[...], preferred_element_type=F32)) + rb_ref[...]

    lane = lax.broadcasted_iota(jnp.int32, (tm, ROUTER_PAD), 1)
    work = logits
    vals, idxs = [], []
    for _ in range(TOP_K):
        m = jnp.max(work, axis=-1, keepdims=True)
        ix = jnp.min(jnp.where(work == m, lane, ROUTER_PAD), axis=-1, keepdims=True)
        vals.append(m)
        idxs.append(ix)
        work = jnp.where(lane == ix, -jnp.inf, work)
    es = [jnp.exp(v - vals[0]) for v in vals]
    den = es[0] + es[1] + es[2] + es[3]

    onehot = jnp.zeros((tm, ROUTER_PAD), F32)
    for ix in idxs:
        onehot = onehot + (lane == ix).astype(F32)
    cum = jnp.dot(tril_ref[...], onehot.astype(BF16), preferred_element_type=F32) + carry[...]
    carry[...] = carry[...] + jnp.sum(onehot, axis=0, keepdims=True)
    cnt_ref[...] = jnp.broadcast_to(carry[...], cnt_ref.shape).astype(jnp.int32)

    idx_o = jnp.zeros((tm, ROUTER_PAD), jnp.int32)
    gate_o = jnp.zeros((tm, ROUTER_PAD), F32)
    rank_o = jnp.zeros((tm, ROUTER_PAD), jnp.int32)
    for k in range(TOP_K):
        rk = jnp.sum(jnp.where(lane == idxs[k], cum, 0.0), axis=-1, keepdims=True).astype(jnp.int32)
        idx_o = jnp.where(lane == k, idxs[k], idx_o)
        gate_o = jnp.where(lane == k, es[k] / den, gate_o)
        rank_o = jnp.where(lane == k, rk, rank_o)
    idx_ref[...] = idx_o
    gate_ref[...] = gate_o
    rank_ref[...] = rank_o


def _merge(ys, yr, z, ms_blk, mr_blk, x2, ps, pr, wo, g2, rwh, rwl, rb, tm):
    L = x2.shape[0]
    tril = (jnp.arange(tm)[:, None] > jnp.arange(tm)[None, :]).astype(BF16)
    const = lambda *shape: pl.BlockSpec(shape, lambda i: (0,) * len(shape), pipeline_mode=pl.Buffered(1))
    row = lambda w: pl.BlockSpec((tm, w), lambda i: (i, 0))
    return pl.pallas_call(
        _merge_kernel,
        name="merge_router",
        grid=(L // tm,),
        in_specs=[
            row(SSM_WIDTH), row(RET_WIDTH),
            pl.BlockSpec((tm, D_MODEL), lambda i: (i, ms_blk)),
            pl.BlockSpec((tm, D_MODEL), lambda i: (i, mr_blk)),
            row(D_MODEL),
            const(SSM_WIDTH, D_MODEL), const(RET_WIDTH, D_MODEL), const(D_MODEL, D_MODEL),
            const(1, D_MODEL),
            const(D_MODEL, ROUTER_PAD), const(D_MODEL, ROUTER_PAD), const(1, ROUTER_PAD),
            const(tm, tm),
        ],
        out_specs=[
            row(D_MODEL), pl.BlockSpec((tm // SUBLANES, SUBLANES, D_MODEL), lambda i: (i, 0, 0)),
            row(ROUTER_PAD), row(ROUTER_PAD), row(ROUTER_PAD),
            pl.BlockSpec((SUBLANES, ROUTER_PAD), lambda i: (0, 0)),
        ],
        out_shape=[
            jax.ShapeDtypeStruct((L, D_MODEL), F32),
            jax.ShapeDtypeStruct((L // SUBLANES, SUBLANES, D_MODEL), F32),
            jax.ShapeDtypeStruct((L, ROUTER_PAD), jnp.int32),
            jax.ShapeDtypeStruct((L, ROUTER_PAD), F32),
            jax.ShapeDtypeStruct((L, ROUTER_PAD), jnp.int32),
            jax.ShapeDtypeStruct((SUBLANES, ROUTER_PAD), jnp.int32),
        ],
        scratch_shapes=[pltpu.VMEM((1, ROUTER_PAD), F32)],
        compiler_params=_cparams(("arbitrary",), 56),
    )(ys, yr, z, z, x2, ps, pr, wo, g2, rwh, rwl, rb, tril)


def _expert_kernel(blk_e_ref, nused_ref, rowtok_ref, h2_hbm, wgu_ref, wd_ref, bgu_ref, bd_ref,
                   unpack_ref, o_ref, xbuf, xb, acc, sem):
    b = pl.program_id(0)
    f = pl.program_id(1)
    n_f = pl.num_programs(1)
    tm = xb.shape[0]
    n_used = nused_ref[0]
    share = tm // n_f

    def issue(blk, lo, n, inline):
        base = blk * tm + lo
        _gather_rows(h2_hbm, xbuf, sem.at[0], lambda r: rowtok_ref[base + r], n // SUBLANES,
                     lo // SUBLANES, inline)

    def wait_block():
        pltpu.make_async_copy(h2_hbm.at[pl.ds(0, tm // SUBLANES)], xbuf, sem.at[0]).wait()

    @pl.when(b < n_used)
    def _():
        @pl.when((b == 0) & (f == 0))
        def _():
            issue(0, 0, tm, False)

        @pl.when(f == 0)
        def _():
            wait_block()
            xb[...] = xbuf[...].reshape(tm, D_MODEL).astype(BF16)
            acc[...] = jnp.zeros_like(acc)

        issue(jnp.minimum(b + 1, n_used - 1), f * share, share, True)

        even = (lax.broadcasted_iota(jnp.int32, (tm, LANES), 1) & 1) == 0

        def act(guc):
            g = jnp.minimum(guc, SWIGLU_LIMIT)
            u = jnp.clip(pltpu.roll(guc, LANES - 1, axis=1), -SWIGLU_LIMIT, SWIGLU_LIMIT)
            return g * _sigmoid(SWIGLU_ALPHA * g) * (u + 1.0)

        x = xb[...]
        tf = wd_ref.shape[1]
        for k in range(tf // MXU_DIM):
            parts = []
            for n in range(2):
                cols = slice((2 * k + n) * MXU_DIM, (2 * k + n + 1) * MXU_DIM)
                gu = jnp.dot(x, wgu_ref[0, :, cols].astype(BF16), preferred_element_type=F32)
                gu = gu + bgu_ref[0, :, cols]
                parts.append(jnp.where(even, act(gu[:, :LANES]), pltpu.roll(act(gu[:, LANES:]), 1, axis=1)))
            packed = jnp.concatenate(parts, axis=-1).astype(BF16)
            a = jnp.dot(packed, unpack_ref[...], preferred_element_type=F32).astype(BF16)
            rows = slice(k * MXU_DIM, (k + 1) * MXU_DIM)
            acc[...] += jnp.dot(a, wd_ref[0, rows, :].astype(BF16), preferred_element_type=F32)

        @pl.when(f == n_f - 1)
        def _():
            o_ref[...] = (acc[...] + bd_ref[0]).reshape(o_ref.shape)

        @pl.when((f == n_f - 1) & (b == n_used - 1))
        def _():
            wait_block()

    @pl.when((b >= n_used) & (f == n_f - 1))
    def _():
        o_ref[...] = jnp.zeros_like(o_ref)


def _unpack_matrix():
    lane = jnp.arange(MXU_DIM)
    n, r = lane // LANES, lane % LANES
    natural = n * LANES + (r % 2) * (LANES // 2) + r // 2
    return (natural[:, None] == jnp.arange(MXU_DIM)[None, :]).astype(BF16)


def _experts(blk_e, n_used, row_tok, h2, wgu, wd, bgu, bd, tm, tf):
    n_rows = row_tok.shape[0]
    n_blocks = n_rows // tm
    F = EXPERT_FF
    n_f = F // tf
    assert tf % MXU_DIM == 0

    def f_eff(b, f, nu):
        return jnp.where(b < nu[0], f, n_f - 1)

    grid_spec = pltpu.PrefetchScalarGridSpec(
        num_scalar_prefetch=3,
        grid=(n_blocks, n_f),
        in_specs=[
            pl.BlockSpec(memory_space=pl.ANY),
            pl.BlockSpec((1, D_MODEL, 2 * tf), lambda b, f, be, nu, rt: (be[b], 0, f_eff(b, f, nu))),
            pl.BlockSpec((1, tf, D_MODEL), lambda b, f, be, nu, rt: (be[b], f_eff(b, f, nu), 0)),
            pl.BlockSpec((1, 1, 2 * tf), lambda b, f, be, nu, rt: (be[b], 0, f_eff(b, f, nu))),
            pl.BlockSpec((1, 1, D_MODEL), lambda b, f, be, nu, rt: (be[b], 0, 0)),
            pl.BlockSpec((MXU_DIM, MXU_DIM), lambda b, f, be, nu, rt: (0, 0)),
        ],
        out_specs=pl.BlockSpec((tm // SUBLANES, SUBLANES, D_MODEL), lambda b, f, be, nu, rt: (b, 0, 0)),
        scratch_shapes=[
            pltpu.VMEM((tm // SUBLANES, SUBLANES, D_MODEL), F32),
            pltpu.VMEM((tm, D_MODEL), BF16),
            pltpu.VMEM((tm, D_MODEL), F32),
            pltpu.SemaphoreType.DMA((1,)),
        ],
    )
    return pl.pallas_call(
        _expert_kernel,
        name="experts",
        grid_spec=grid_spec,
        out_shape=jax.ShapeDtypeStruct((n_rows // SUBLANES, SUBLANES, D_MODEL), F32),
        compiler_params=_cparams(("arbitrary", "arbitrary"), 60),
    )(blk_e, n_used, row_tok, h2, wgu, wd, bgu, bd, _unpack_matrix())


def _combine_kernel(dest_ref, yb_hbm, gate_ref, x1_ref, p_ref, wpg_ref, wpp_ref, g3_ref, gf_ref,
                    o_ref, gbuf, sem):
    i = pl.program_id(0)
    n = pl.num_programs(0)
    tm = x1_ref.shape[0]

    def issue(tile):
        base = tile * tm * TOP_K

        for k in range(TOP_K):
            _gather_rows(yb_hbm, gbuf.at[k], sem.at[0], lambda r, k=k: dest_ref[base + r * TOP_K + k],
                         tm // SUBLANES, 0)

    @pl.when(i == 0)
    def _():
        issue(0)

    for k in range(TOP_K):
        pltpu.make_async_copy(yb_hbm.at[pl.ds(0, tm // SUBLANES)], gbuf.at[k], sem.at[0]).wait()
    gates = gate_ref[...]
    moe = gates[:, 0:1] * gbuf[0].reshape(tm, D_MODEL)
    for k in range(1, TOP_K):
        moe = moe + gates[:, k:k + 1] * gbuf[k].reshape(tm, D_MODEL)
    x2 = x1_ref[...] + moe

    @pl.when(i + 1 < n)
    def _():
        issue(i + 1)

    h3 = _rms(x2, g3_ref[...]).astype(BF16)
    pg = _sigmoid(jnp.dot(h3, wpg_ref[...], preferred_element_type=F32))
    pp = jnp.dot(p_ref[...].astype(BF16), wpp_ref[...], preferred_element_type=F32)
    x3 = x2 + pg * pp
    o_ref[...] = _rms(x3, gf_ref[...])


def _combine(dest, yb, gates, x1, p2, wpg, wpp, g3, gf, tm):
    L = x1.shape[0]
    const = lambda *shape: pl.BlockSpec(shape, lambda i, d: (0,) * len(shape), pipeline_mode=pl.Buffered(1))
    row = lambda w: pl.BlockSpec((tm, w), lambda i, d: (i, 0))
    grid_spec = pltpu.PrefetchScalarGridSpec(
        num_scalar_prefetch=1,
        grid=(L // tm,),
        in_specs=[
            pl.BlockSpec(memory_space=pl.ANY),
            row(ROUTER_PAD), row(D_MODEL), row(PLE_DIM),
            const(D_MODEL, D_MODEL), const(PLE_DIM, D_MODEL), const(1, D_MODEL), const(1, D_MODEL),
        ],
        out_specs=row(D_MODEL),
        scratch_shapes=[
            pltpu.VMEM((TOP_K, tm // SUBLANES, SUBLANES, D_MODEL), F32),
            pltpu.SemaphoreType.DMA((1,)),
        ],
    )
    return pl.pallas_call(
        _combine_kernel,
        name="combine_ple",
        grid_spec=grid_spec,
        out_shape=jax.ShapeDtypeStruct((L, D_MODEL), F32),
        compiler_params=_cparams(("arbitrary",), 48),
    )(dest, yb, gates, x1, p2, wpg, wpp, g3, gf)


def _tile(L, pref):
    return min(pref, L)


def _layer(x2, p2, pos, norm1_g, w_in, lam_re, lam_im, log_dt, b_re, b_im, c_re, c_im, d_skip, w_glu,
           b_glu, w_bs, w_br, w_out, norm2_g, router_w, router_b, w_gu, b_gu, w_dn, b_dn, norm3_g,
           w_pg, w_pp, out_g):
    L = x2.shape[0]
    G, N, E = SSM_GROUPS, SSM_STATE, N_EXPERTS

    cuts = SSM_WIDTH + 4 * RET_WIDTH
    w_in_bf = jnp.concatenate([w_in[:, cuts:], w_in[:, :cuts]], axis=1).astype(BF16)
    z = _inproj(x2, norm1_g[None, :], w_in_bf, _tile(L, 1024), 1024)
    ms_blk, mr_blk = 0, 1
    u_blk, q_blk, k_blk, v_blk, g_blk = 4, 5, 6, 7, 8

    ab_re, ab_im, bb_re, bb_im = _s5_prep(lam_re, lam_im, log_dt, b_re, b_im)
    eye = jnp.eye(SUBLANES, dtype=F32)
    gpb = LANES // SSM_GROUP
    def bpack(bb):
        t = bb.reshape(N_UBLK, gpb, SSM_GROUP, N)
        return jnp.einsum('bgcn,gh->bgchn', t, eye).reshape(N_UBLK, LANES, ST_PER_UBLK)
    def cpack(c):
        t = c.reshape(N_UBLK, gpb, SSM_GROUP, N)
        return jnp.einsum('bgcn,gh->bgnhc', t, eye).reshape(N_UBLK, ST_PER_UBLK, LANES)
    bblk = jnp.concatenate([bpack(bb_re), bpack(bb_im)], axis=-1).astype(BF16)
    cblk = jnp.concatenate([cpack(c_re), -cpack(c_im)], axis=1).astype(BF16)
    atile = lambda a: a[::SSM_GROUP].reshape(N_SVREG, SUBLANES, LANES)
    ys = _s5(z, u_blk, bblk, cblk, atile(ab_re), atile(ab_im), d_skip.reshape(1, SSM_WIDTH),
             w_glu.astype(BF16), b_glu[None, :], _tile(L, 512))

    yr = _retention(z, (q_blk, k_blk, v_blk, g_blk), pos, _tile(L, 512))

    rw = jnp.pad(router_w, ((0, 0), (0, ROUTER_PAD - E)))
    rwh = rw.astype(BF16)
    rwl = (rw - rwh.astype(F32)).astype(BF16)
    rb = jnp.pad(router_b, (0, ROUTER_PAD - E), constant_values=NEG_BIG)[None, :]
    x1, h2, idx, gates, rank, cnt = _merge(
        ys, yr, z, ms_blk, mr_blk, x2, w_bs.astype(BF16), w_br.astype(BF16), w_out.astype(BF16),
        norm2_g[None, :], rwh, rwl, rb, _tile(L, 256))

    tm_e = 512
    n_asg = L * TOP_K
    n_rows = n_asg + E * tm_e
    n_blocks = n_rows // tm_e
    counts = cnt[0, :E]
    pcounts = (counts + tm_e - 1) // tm_e * tm_e
    pend = jnp.cumsum(pcounts)
    pstart = pend - pcounts
    idx4, rank4 = idx[:, :TOP_K], rank[:, :TOP_K]
    dest = (pstart[idx4] + rank4).astype(jnp.int32)
    tok = jnp.broadcast_to(jnp.arange(L, dtype=jnp.int32)[:, None], (L, TOP_K))
    row_tok = jnp.zeros((n_rows,), jnp.int32).at[dest.reshape(-1)].set(tok.reshape(-1))
    blk_start = jnp.arange(n_blocks, dtype=jnp.int32) * tm_e
    blk_e = jnp.minimum(jnp.sum(blk_start[:, None] >= pend[None, :], axis=1), E - 1).astype(jnp.int32)
    n_used = (pend[-1] // tm_e).astype(jnp.int32).reshape(1)

    tf = 512
    yb = _experts(blk_e, n_used, row_tok, h2, w_gu, w_dn, b_gu[:, None, :], b_dn[:, None, :], tm_e, tf)

    return _combine(dest.reshape(-1), yb, gates, x1, p2, w_pg.astype(BF16), w_pp.astype(BF16),
                    norm3_g[None, :], out_g[None, :], _tile(L, 256))


def kernel(x, p, positions, norm1_g, w_in, ssm_lam_re, ssm_lam_im, ssm_log_dt, ssm_b_re, ssm_b_im, ssm_c_re, ssm_c_im, ssm_d, ssm_w_glu, ssm_b_glu, w_branch_ssm, w_branch_ret, w_out, norm2_g, router_w, router_b, exp_w_gate_up, exp_b_gate_up, exp_w_down, exp_b_down, norm3_g, ple_w_gate, ple_w_proj, final_norm_g):
    bsz, L, d = x.shape
    depth = w_in.shape[0]
    assert bsz == 1 and depth == 1 and d == D_MODEL
    out = _layer(
        x[0], p[0, 0], positions.reshape(L, 1), norm1_g[0], w_in[0], ssm_lam_re[0], ssm_lam_im[0],
        ssm_log_dt[0], ssm_b_re[0], ssm_b_im[0], ssm_c_re[0], ssm_c_im[0], ssm_d[0], ssm_w_glu[0],
        ssm_b_glu[0], w_branch_ssm[0], w_branch_ret[0], w_out[0], norm2_g[0], router_w[0], router_b[0],
        exp_w_gate_up[0], exp_b_gate_up[0], exp_w_down[0], exp_b_down[0], norm3_g[0], ple_w_gate[0],
        ple_w_proj[0], final_norm_g)
    return out[None]
```

```python
import functools
import math

import jax
import jax.numpy as jnp
from jax import lax
from jax.experimental import pallas as pl
from jax.experimental.pallas import tpu as pltpu

F32 = jnp.float32
BF16 = jnp.bfloat16

D_MODEL = 2048
PLE_DIM = 256
SSM_WIDTH = D_MODEL // 2
SSM_GROUP = 16
SSM_GROUPS = SSM_WIDTH // SSM_GROUP
SSM_STATE = 64
RET_HEADS = 8
RET_HEAD_DIM = 128
RET_WIDTH = RET_HEADS * RET_HEAD_DIM
RET_CHUNK = 128
ROPE_BASE = 10000.0
N_EXPERTS = 32
TOP_K = 4
EXPERT_FF = D_MODEL
SWIGLU_LIMIT = 7.0
SWIGLU_ALPHA = 1.702
NORM_EPS = 1e-6
IN_COLS = SSM_WIDTH + 4 * RET_WIDTH + 2 * D_MODEL

LANES = 128
SUBLANES = 8
MXU_DIM = 256
ROUTER_PAD = LANES
NEG_BIG = -1e30

MIB = 1024 * 1024


def _cparams(sem, vmem_mib):
    return pltpu.CompilerParams(dimension_semantics=sem, vmem_limit_bytes=vmem_mib * MIB)


def _rms(xf, g):
    ms = jnp.mean(xf * xf, axis=-1, keepdims=True)
    return xf * lax.rsqrt(ms + NORM_EPS) * g


def _sigmoid(x):
    return 1.0 / (1.0 + jnp.exp(-x))


def _gather_rows(src3, dst3, sem, index_of, n_groups, dst_group0, inline=False, first_group=0):
    def body(j, c):
        for u in range(SUBLANES):
            row = index_of(j * SUBLANES + u)
            pltpu.make_async_copy(src3.at[row >> 3, pl.ds(row & (SUBLANES - 1), 1), :],
                                  dst3.at[dst_group0 + j, pl.ds(u, 1), :], sem).start()
        return c

    if inline:
        for j in range(first_group, first_group + n_groups):
            body(j, 0)
    else:
        lax.fori_loop(first_group, first_group + n_groups, body, 0)


def _inproj_kernel(x_ref, g_ref, w_ref, o_ref, h_scr):
    @pl.when(pl.program_id(1) == 0)
    def _():
        h_scr[...] = _rms(x_ref[...], g_ref[...]).astype(BF16)

    o_ref[...] = jnp.dot(h_scr[...], w_ref[...], preferred_element_type=F32).astype(o_ref.dtype)


def _inproj(x2, g1, w_in_bf, tm, tn):
    L = x2.shape[0]
    n_cols = w_in_bf.shape[1]
    return pl.pallas_call(
        _inproj_kernel,
        name="inproj",
        grid=(L // tm, n_cols // tn),
        in_specs=[
            pl.BlockSpec((tm, D_MODEL), lambda i, j: (i, 0)),
            pl.BlockSpec((1, D_MODEL), lambda i, j: (0, 0)),
            pl.BlockSpec((D_MODEL, tn), lambda i, j: (0, j)),
        ],
        out_specs=pl.BlockSpec((tm, tn), lambda i, j: (i, j)),
        out_shape=jax.ShapeDtypeStruct((L, n_cols), BF16),
        scratch_shapes=[pltpu.VMEM((tm, D_MODEL), BF16)],
        compiler_params=_cparams(("arbitrary", "arbitrary"), 48),
    )(x2, g1, w_in_bf)


def _s5_prep_kernel(lr_ref, li_ref, ldt_ref, bre_ref, bim_ref, abre_ref, abim_ref, bbre_ref, bbim_ref):
    lr, li = lr_ref[...], li_ref[...]
    dt = jnp.exp(ldt_ref[...])
    mag = jnp.exp(lr * dt)
    ab_re, ab_im = mag * jnp.cos(li * dt), mag * jnp.sin(li * dt)
    den = lr * lr + li * li
    nr, ni = ab_re - 1.0, ab_im
    f_re = (nr * lr + ni * li) / den
    f_im = (ni * lr - nr * li) / den
    bre, bim = bre_ref[...], bim_ref[...]
    abre_ref[...] = ab_re
    abim_ref[...] = ab_im
    bbre_ref[...] = f_re * bre - f_im * bim
    bbim_ref[...] = f_re * bim + f_im * bre


def _s5_prep(lam_re, lam_im, log_dt, b_re, b_im):
    rep = lambda a: jnp.repeat(a, SSM_GROUP, axis=0)
    bt = lambda b: jnp.transpose(b, (0, 2, 1)).reshape(SSM_WIDTH, SSM_STATE)
    shp = jax.ShapeDtypeStruct((SSM_WIDTH, SSM_STATE), F32)
    return pl.pallas_call(_s5_prep_kernel, name="s5_prep", out_shape=(shp, shp, shp, shp))(
        rep(lam_re), rep(lam_im), rep(log_dt[:, None]), bt(b_re), bt(b_im))


def _gelu_tanh(x):
    return 0.5 * x * (1.0 + jnp.tanh(math.sqrt(2.0 / math.pi) * (x + 0.044715 * (x * x * x))))


N_UBLK = SSM_WIDTH // LANES
ST_PER_UBLK = (LANES // SSM_GROUP) * SSM_STATE
TILES_PER_UBLK = ST_PER_UBLK // LANES
N_SVREG = SSM_GROUPS * SSM_STATE // (LANES * SUBLANES)


def _s5_kernel(u_ref, bblk_ref, cblk_ref, are_ref, aim_ref, d_ref, wglu_ref, bglu_ref, o_ref,
               sre, sim, carry):
    tm = u_ref.shape[0]

    @pl.when(pl.program_id(0) == 0)
    def _():
        carry[...] = jnp.zeros_like(carry)

    u = u_ref[...]
    for b in range(N_UBLK):
        bu = jnp.dot(u[:, b * LANES:(b + 1) * LANES], bblk_ref[b], preferred_element_type=F32)
        v = (b * TILES_PER_UBLK) // SUBLANES
        for q in range(TILES_PER_UBLK):
            k = (b * TILES_PER_UBLK + q) % SUBLANES
            sre[v, pl.ds(k, tm, stride=SUBLANES), :] = bu[:, q * LANES:(q + 1) * LANES]
            sim[v, pl.ds(k, tm, stride=SUBLANES), :] = bu[:, ST_PER_UBLK + q * LANES:ST_PER_UBLK + (q + 1) * LANES]

    are, aim = are_ref[...], aim_ref[...]

    def step(t, c):
        cre, cim = c
        r0 = pl.multiple_of(t * SUBLANES, SUBLANES)
        bre = sre[:, pl.ds(r0, SUBLANES), :]
        bim = sim[:, pl.ds(r0, SUBLANES), :]
        nre = are * cre - aim * cim + bre
        nim = are * cim + aim * cre + bim
        sre[:, pl.ds(r0, SUBLANES), :] = nre
        sim[:, pl.ds(r0, SUBLANES), :] = nim
        return nre, nim

    cre, cim = lax.fori_loop(0, tm, step, (carry[0], carry[1]), unroll=4)
    carry[0] = cre
    carry[1] = cim

    ys = []
    for b in range(N_UBLK):
        v = (b * TILES_PER_UBLK) // SUBLANES
        k0 = (b * TILES_PER_UBLK) % SUBLANES
        parts = [sre[v, pl.ds(k0 + q, tm, stride=SUBLANES), :] for q in range(TILES_PER_UBLK)]
        parts += [sim[v, pl.ds(k0 + q, tm, stride=SUBLANES), :] for q in range(TILES_PER_UBLK)]
        sb = jnp.concatenate(parts, axis=-1).astype(BF16)
        ys.append(jnp.dot(sb, cblk_ref[b], preferred_element_type=F32))
    y = jnp.concatenate(ys, axis=-1) + d_ref[...] * u.astype(F32)
    y = _gelu_tanh(y)
    gl = jnp.dot(y.astype(BF16), wglu_ref[...], preferred_element_type=F32) + bglu_ref[...]
    o_ref[...] = (y * _sigmoid(gl)).astype(o_ref.dtype)


def _s5(z, u_blk, bblk, cblk, are, aim, d_skip, wglu, bglu, tm):
    L = z.shape[0]
    const = lambda *shape: pl.BlockSpec(shape, lambda i: (0,) * len(shape))
    return pl.pallas_call(
        _s5_kernel,
        name="s5_branch",
        grid=(L // tm,),
        in_specs=[
            pl.BlockSpec((tm, SSM_WIDTH), lambda i: (i, u_blk)),
            const(N_UBLK, LANES, 2 * ST_PER_UBLK),
            const(N_UBLK, 2 * ST_PER_UBLK, LANES),
            const(N_SVREG, SUBLANES, LANES),
            const(N_SVREG, SUBLANES, LANES),
            const(1, SSM_WIDTH),
            const(SSM_WIDTH, SSM_WIDTH),
            const(1, SSM_WIDTH),
        ],
        out_specs=pl.BlockSpec((tm, SSM_WIDTH), lambda i: (i, 0)),
        out_shape=jax.ShapeDtypeStruct((L, SSM_WIDTH), BF16),
        scratch_shapes=[
            pltpu.VMEM((N_SVREG, tm * SUBLANES, LANES), F32),
            pltpu.VMEM((N_SVREG, tm * SUBLANES, LANES), F32),
            pltpu.VMEM((2, N_SVREG, SUBLANES, LANES), F32),
        ],
        compiler_params=_cparams(("arbitrary",), 48),
    )(z, bblk, cblk, are, aim, d_skip, wglu, bglu)


def _ret_kernel(q_ref, k_ref, v_ref, g_ref, pos_ref, inv_ref, decay_ref, zeta_ref, xi_ref, cd_ref,
                o_ref, r_scr):
    tm = q_ref.shape[0]
    C, dk = RET_CHUNK, RET_HEAD_DIM

    @pl.when(pl.program_id(0) == 0)
    def _():
        r_scr[...] = jnp.zeros_like(r_scr)

    ang = pos_ref[...].astype(F32) * inv_ref[...]
    cs, sn = jnp.cos(ang), jnp.sin(ang)
    cf = jnp.concatenate([cs, cs], axis=-1)
    sf = jnp.concatenate([-sn, sn], axis=-1)

    def rot(xh):
        return xh * cf + pltpu.roll(xh, dk // 2, axis=1) * sf

    nt = (((1,), (1,)), ((), ()))
    for h in range(RET_HEADS):
        cols = slice(h * dk, (h + 1) * dk)
        qr = rot(q_ref[:, cols].astype(F32))
        kr = rot(k_ref[:, cols].astype(F32))
        for n in range(tm // C):
            rows = slice(n * C, (n + 1) * C)
            qc = qr[rows].astype(BF16)
            kcf = kr[rows]
            vc = v_ref[rows, cols]
            sc = lax.dot_general(qc, kcf.astype(BF16), nt, preferred_element_type=F32) * decay_ref[h]
            inner = jnp.dot(sc.astype(BF16), vc, preferred_element_type=F32)
            r_prev = r_scr[h]
            cross = jnp.dot(qc, r_prev.astype(BF16), preferred_element_type=F32) * xi_ref[h]
            o = inner + cross
            kzt = jnp.transpose(kcf * zeta_ref[h]).astype(BF16)
            r_scr[h] = r_prev * cd_ref[h] + jnp.dot(kzt, vc, preferred_element_type=F32)
            mu = jnp.mean(o, axis=-1, keepdims=True)
            oc = o - mu
            on = oc * lax.rsqrt(jnp.mean(oc * oc, axis=-1, keepdims=True) + NORM_EPS)
            gate = g_ref[rows, cols].astype(F32)
            o_ref[rows, cols] = (gate * _sigmoid(gate) * on).astype(o_ref.dtype)


def _retention(z, blks, pos, tm):
    L = z.shape[0]
    H, C, dk = RET_HEADS, RET_CHUNK, RET_HEAD_DIM
    half = dk // 2
    inv = (ROPE_BASE ** (-jnp.arange(half, dtype=F32) / half))[None, :]
    log_gamma = jnp.log(1.0 - jnp.exp2(-5.0 - jnp.arange(H, dtype=F32)))
    idx = jnp.arange(C, dtype=F32)
    rel = idx[:, None] - idx[None, :]
    scale = dk ** -0.5
    decay = jnp.where(rel >= 0, jnp.exp(jnp.maximum(rel, 0.0)[None] * log_gamma[:, None, None]), 0.0) * scale
    zeta = jnp.exp((C - 1.0 - idx)[None, :] * log_gamma[:, None]) * scale
    xi = jnp.exp((idx + 1.0)[None, :] * log_gamma[:, None])
    bc = lambda a: jnp.broadcast_to(a[:, :, None], (H, C, dk))
    cd = jnp.broadcast_to(jnp.exp(C * log_gamma)[:, None, None], (H, 1, dk))
    const = lambda *shape: pl.BlockSpec(shape, lambda i: (0,) * len(shape))
    qb, kb, vb, gb = blks
    return pl.pallas_call(
        _ret_kernel,
        name="retention",
        grid=(L // tm,),
        in_specs=[
            pl.BlockSpec((tm, RET_WIDTH), lambda i: (i, qb)),
            pl.BlockSpec((tm, RET_WIDTH), lambda i: (i, kb)),
            pl.BlockSpec((tm, RET_WIDTH), lambda i: (i, vb)),
            pl.BlockSpec((tm, RET_WIDTH), lambda i: (i, gb)),
            pl.BlockSpec((tm, 1), lambda i: (i, 0)),
            const(1, half),
            const(H, C, C),
            const(H, C, dk),
            const(H, C, dk),
            const(H, 1, dk),
        ],
        out_specs=pl.BlockSpec((tm, RET_WIDTH), lambda i: (i, 0)),
        out_shape=jax.ShapeDtypeStruct((L, RET_WIDTH), BF16),
        scratch_shapes=[pltpu.VMEM((H, dk, dk), F32)],
        compiler_params=_cparams(("arbitrary",), 48),
    )(z, z, z, z, pos, inv, decay, bc(zeta), bc(xi), cd)


def _merge_kernel(ys_ref, yr_ref, ms_ref, mr_ref, x_ref, ps_ref, pr_ref, wo_ref, g2_ref,
                  rwh_ref, rwl_ref, rb_ref, tril_ref,
                  x1_ref, h2_ref, idx_ref, gate_ref, rank_ref, cnt_ref, carry):
    tm = x_ref.shape[0]

    @pl.when(pl.program_id(0) == 0)
    def _():
        carry[...] = jnp.zeros_like(carry)

    a = jnp.dot(ys_ref[...], ps_ref[...], preferred_element_type=F32)
    b = jnp.dot(yr_ref[...], pr_ref[...], preferred_element_type=F32)
    merged = _sigmoid(ms_ref[...].astype(F32)) * a + _sigmoid(mr_ref[...].astype(F32)) * b
    x1 = x_ref[...] + jnp.dot(merged.astype(BF16), wo_ref[...], preferred_element_type=F32)
    x1_ref[...] = x1
    h2 = _rms(x1, g2_ref[...])
    h2_ref[...] = h2.reshape(h2_ref.shape)

    hh = h2.astype(BF16)
    hl = (h2 - hh.astype(F32)).astype(BF16)
    logits = (jnp.dot(hh, rwh_ref[...], preferred_element_type=F32)
              + jnp.dot(hh, rwl_ref[...], preferred_element_type=F32)
              + jnp.dot(hl, rwh_ref[...], preferred_element_type=F32)) + rb_ref[...]

    lane = lax.broadcasted_iota(jnp.int32, (tm, ROUTER_PAD), 1)
    work = logits
    vals, idxs = [], []
    for _ in range(TOP_K):
        m = jnp.max(work, axis=-1, keepdims=True)
        ix = jnp.min(jnp.where(work == m, lane, ROUTER_PAD), axis=-1, keepdims=True)
        vals.append(m)
        idxs.append(ix)
        work = jnp.where(lane == ix, -jnp.inf, work)
    es = [jnp.exp(v - vals[0]) for v in vals]
    den = es[0] + es[1] + es[2] + es[3]

    onehot = jnp.zeros((tm, ROUTER_PAD), F32)
    for ix in idxs:
        onehot = onehot + (lane == ix).astype(F32)
    cum = jnp.dot(tril_ref[...], onehot.astype(BF16), preferred_element_type=F32) + carry[...]
    carry[...] = carry[...] + jnp.sum(onehot, axis=0, keepdims=True)
    cnt_ref[...] = jnp.broadcast_to(carry[...], cnt_ref.shape).astype(jnp.int32)

    idx_o = jnp.zeros((tm, ROUTER_PAD), jnp.int32)
    gate_o = jnp.zeros((tm, ROUTER_PAD), F32)
    rank_o = jnp.zeros((tm, ROUTER_PAD), jnp.int32)
    for k in range(TOP_K):
        rk = jnp.sum(jnp.where(lane == idxs[k], cum, 0.0), axis=-1, keepdims=True).astype(jnp.int32)
        idx_o = jnp.where(lane == k, idxs[k], idx_o)
        gate_o = jnp.where(lane == k, es[k] / den, gate_o)
        rank_o = jnp.where(lane == k, rk, rank_o)
    idx_ref[...] = idx_o
    gate_ref[...] = gate_o
    rank_ref[...] = rank_o


def _merge(ys, yr, z, ms_blk, mr_blk, x2, ps, pr, wo, g2, rwh, rwl, rb, tm):
    L = x2.shape[0]
    tril = (jnp.arange(tm)[:, None] > jnp.arange(tm)[None, :]).astype(BF16)
    const = lambda *shape: pl.BlockSpec(shape, lambda i: (0,) * len(shape), pipeline_mode=pl.Buffered(1))
    row = lambda w: pl.BlockSpec((tm, w), lambda i: (i, 0))
    return pl.pallas_call(
        _merge_kernel,
        name="merge_router",
        grid=(L // tm,),
        in_specs=[
            row(SSM_WIDTH), row(RET_WIDTH),
            pl.BlockSpec((tm, D_MODEL), lambda i: (i, ms_blk)),
            pl.BlockSpec((tm, D_MODEL), lambda i: (i, mr_blk)),
            row(D_MODEL),
            const(SSM_WIDTH, D_MODEL), const(RET_WIDTH, D_MODEL), const(D_MODEL, D_MODEL),
            const(1, D_MODEL),
            const(D_MODEL, ROUTER_PAD), const(D_MODEL, ROUTER_PAD), const(1, ROUTER_PAD),
            const(tm, tm),
        ],
        out_specs=[
            row(D_MODEL), pl.BlockSpec((tm // SUBLANES, SUBLANES, D_MODEL), lambda i: (i, 0, 0)),
            row(ROUTER_PAD), row(ROUTER_PAD), row(ROUTER_PAD),
            pl.BlockSpec((SUBLANES, ROUTER_PAD), lambda i: (0, 0)),
        ],
        out_shape=[
            jax.ShapeDtypeStruct((L, D_MODEL), F32),
            jax.ShapeDtypeStruct((L // SUBLANES, SUBLANES, D_MODEL), F32),
            jax.ShapeDtypeStruct((L, ROUTER_PAD), jnp.int32),
            jax.ShapeDtypeStruct((L, ROUTER_PAD), F32),
            jax.ShapeDtypeStruct((L, ROUTER_PAD), jnp.int32),
            jax.ShapeDtypeStruct((SUBLANES, ROUTER_PAD), jnp.int32),
        ],
        scratch_shapes=[pltpu.VMEM((1, ROUTER_PAD), F32)],
        compiler_params=_cparams(("arbitrary",), 56),
    )(ys, yr, z, z, x2, ps, pr, wo, g2, rwh, rwl, rb, tril)


SUB_ROWS = 256
ITEM_SUBS = 4
SUB_GROUPS = SUB_ROWS // SUBLANES


def _expert_kernel(item_e_ref, item_g0_ref, item_nc_ref, item_no_ref, rowtok_ref,
                   h2_hbm, wgu_ref, wd_ref, bgu_ref, bd_ref, unpack_ref, yb_hbm,
                   xbuf, xb, acc, wgu_bf, wd_bf, gsem, osem):
    w = pl.program_id(0)
    f = pl.program_id(1)
    n_w = pl.num_programs(0)
    n_f = pl.num_programs(1)
    tf = wd_ref.shape[1]
    nc = item_nc_ref[w]
    no = item_no_ref[w]

    def groups(s):
        return pl.ds(s * SUB_GROUPS, SUB_GROUPS)

    def gather_sub(item, s):
        base = item_g0_ref[item] * SUBLANES + s * SUB_ROWS
        _gather_rows(h2_hbm, xbuf, gsem.at[0], lambda r: rowtok_ref[base + r], SUB_GROUPS, s * SUB_GROUPS)

    def gather_wait(s):
        pltpu.make_async_copy(h2_hbm.at[pl.ds(0, SUB_GROUPS)], xbuf.at[groups(s)], gsem.at[0]).wait()

    def out_copy(s, g0):
        return pltpu.make_async_copy(acc.at[groups(s)], yb_hbm.at[pl.ds(g0 + s * SUB_GROUPS, SUB_GROUPS)],
                                     osem.at[0])

    def for_subs(count, fn):
        for s in range(ITEM_SUBS):
            pl.when(s < count)(functools.partial(fn, s))

    @pl.when(f == 0)
    def _():
        @pl.when(w > 0)
        def _():
            for_subs(item_no_ref[jnp.maximum(w - 1, 0)], lambda s: out_copy(s, 0).wait())

        @pl.when(w == 0)
        def _():
            for_subs(nc, lambda s: gather_sub(0, s))

        init = bd_ref[0] * (nc > 0).astype(F32)

        for_subs(nc, gather_wait)

        def stage(s):
            xb[s * SUB_ROWS:(s + 1) * SUB_ROWS, :] = xbuf[groups(s)].reshape(SUB_ROWS, D_MODEL).astype(BF16)

        def init_acc(s):
            acc[groups(s)] = jnp.broadcast_to(init, (SUB_ROWS, D_MODEL)).reshape(SUB_GROUPS, SUBLANES, D_MODEL)

        for_subs(nc, stage)
        for_subs(no, init_acc)

    even = (lax.broadcasted_iota(jnp.int32, (SUB_ROWS, LANES), 1) & 1) == 0

    def act(guc):
        g = jnp.minimum(guc, SWIGLU_LIMIT)
        u = jnp.clip(pltpu.roll(guc, LANES - 1, axis=1), -SWIGLU_LIMIT, SWIGLU_LIMIT)
        return g * _sigmoid(SWIGLU_ALPHA * g) * (u + 1.0)

    def sub_block(s):
        x = xb[s * SUB_ROWS:(s + 1) * SUB_ROWS, :]
        n_k = tf // MXU_DIM

        def gate_up(k):
            outs = []
            for n in range(2):
                cols = slice((2 * k + n) * MXU_DIM, (2 * k + n + 1) * MXU_DIM)
                if s == 0:
                    wt = wgu_ref[0, :, cols].astype(BF16)
                    wgu_bf[:, cols] = wt
                else:
                    wt = wgu_bf[:, cols]
                outs.append(jnp.dot(x, wt, preferred_element_type=F32) + bgu_ref[0, :, cols])
            return outs

        def activate(gus):
            parts = [jnp.where(even, act(gu[:, :LANES]), pltpu.roll(act(gu[:, LANES:]), 1, axis=1))
                     for gu in gus]
            packed = jnp.concatenate(parts, axis=-1).astype(BF16)
            return jnp.dot(packed, unpack_ref[...], preferred_element_type=F32).astype(BF16)

        gus = gate_up(0)
        for k in range(n_k):
            nxt = gate_up(k + 1) if k + 1 < n_k else None
            a = activate(gus)
            rows = slice(k * MXU_DIM, (k + 1) * MXU_DIM)
            if s == 0:
                wdt = wd_ref[0, rows, :].astype(BF16)
                wd_bf[rows, :] = wdt
            else:
                wdt = wd_bf[rows, :]
            cur = acc[groups(s)].reshape(SUB_ROWS, D_MODEL)
            cur = cur + jnp.dot(a, wdt, preferred_element_type=F32)
            acc[groups(s)] = cur.reshape(SUB_GROUPS, SUBLANES, D_MODEL)
            gus = nxt

    for_subs(nc, sub_block)

    nxt_item = jnp.minimum(w + 1, n_w - 1)

    @pl.when((w + 1 < n_w) & (f < item_nc_ref[nxt_item]))
    def _():
        gather_sub(nxt_item, f)

    @pl.when(f == n_f - 1)
    def _():
        g0 = item_g0_ref[w]
        for_subs(no, lambda s: out_copy(s, g0).start())

        @pl.when(w == n_w - 1)
        def _():
            for_subs(no, lambda s: out_copy(s, 0).wait())


def _unpack_matrix():
    lane = jnp.arange(MXU_DIM)
    n, r = lane // LANES, lane % LANES
    natural = n * LANES + (r % 2) * (LANES // 2) + r // 2
    return (natural[:, None] == jnp.arange(MXU_DIM)[None, :]).astype(BF16)


def _expert_items(counts, n_asg):
    E = N_EXPERTS
    item_rows = SUB_ROWS * ITEM_SUBS
    n_items_max = n_asg // item_rows + E
    n_rows = n_asg + E * SUB_ROWS
    pcounts = (counts + SUB_ROWS - 1) // SUB_ROWS * SUB_ROWS
    pend = jnp.cumsum(pcounts)
    pstart = pend - pcounts
    n_it = (pcounts + item_rows - 1) // item_rows
    it_end = jnp.cumsum(n_it)
    it_start = it_end - n_it
    n_items = it_end[-1]
    wv = jnp.arange(n_items_max, dtype=jnp.int32)
    e_w = jnp.minimum(jnp.sum(wv[:, None] >= it_end[None, :], axis=1), E - 1)
    row0 = pstart[e_w] + (wv - it_start[e_w]) * item_rows
    nsb = jnp.clip((pend[e_w] - row0) // SUB_ROWS, 0, ITEM_SUBS)
    used = wv < n_items
    tail0 = pend[-1] + (wv - n_items) * item_rows
    nz = jnp.clip((n_rows - tail0) // SUB_ROWS, 0, ITEM_SUBS)
    item_row0 = jnp.clip(jnp.where(used, row0, tail0), 0, n_rows - SUB_ROWS)
    item_e = jnp.where(used, e_w, e_w[jnp.maximum(n_items - 1, 0)])
    i32 = lambda a: a.astype(jnp.int32)
    return (i32(item_e), i32(item_row0 // SUBLANES), i32(jnp.where(used, nsb, 0)),
            i32(jnp.where(used, nsb, nz)), pstart, n_rows)


def _experts(item_e, item_g0, item_nc, item_no, row_tok, h2, wgu, wd, bgu, bd, tf):
    n_rows = row_tok.shape[0]
    n_items = item_e.shape[0]
    n_f = EXPERT_FF // tf
    item_rows = SUB_ROWS * ITEM_SUBS
    assert tf % MXU_DIM == 0 and n_f == ITEM_SUBS

    def f_eff(w, f, nc):
        return jnp.where(nc[w] > 0, f, n_f - 1)

    grid_spec = pltpu.PrefetchScalarGridSpec(
        num_scalar_prefetch=5,
        grid=(n_items, n_f),
        in_specs=[
            pl.BlockSpec(memory_space=pl.ANY),
            pl.BlockSpec((1, D_MODEL, 2 * tf), lambda w, f, ie, g0, nc, no, rt: (ie[w], 0, f_eff(w, f, nc))),
            pl.BlockSpec((1, tf, D_MODEL), lambda w, f, ie, g0, nc, no, rt: (ie[w], f_eff(w, f, nc), 0)),
            pl.BlockSpec((1, 1, 2 * tf), lambda w, f, ie, g0, nc, no, rt: (ie[w], 0, f_eff(w, f, nc))),
            pl.BlockSpec((1, 1, D_MODEL), lambda w, f, ie, g0, nc, no, rt: (ie[w], 0, 0)),
            pl.BlockSpec((MXU_DIM, MXU_DIM), lambda w, f, ie, g0, nc, no, rt: (0, 0)),
        ],
        out_specs=pl.BlockSpec(memory_space=pl.ANY),
        scratch_shapes=[
            pltpu.VMEM((item_rows // SUBLANES, SUBLANES, D_MODEL), F32),
            pltpu.VMEM((item_rows, D_MODEL), BF16),
            pltpu.VMEM((item_rows // SUBLANES, SUBLANES, D_MODEL), F32),
            pltpu.VMEM((D_MODEL, 2 * tf), BF16),
            pltpu.VMEM((tf, D_MODEL), BF16),
            pltpu.SemaphoreType.DMA((1,)),
            pltpu.SemaphoreType.DMA((1,)),
        ],
    )
    return pl.pallas_call(
        _expert_kernel,
        name="experts",
        grid_spec=grid_spec,
        out_shape=jax.ShapeDtypeStruct((n_rows // SUBLANES, SUBLANES, D_MODEL), F32),
        compiler_params=_cparams(("arbitrary", "arbitrary"), 60),
    )(item_e, item_g0, item_nc, item_no, row_tok, h2, wgu, wd, bgu, bd, _unpack_matrix())


def _combine_kernel(dest_ref, yb_hbm, gate_ref, x1_ref, p_ref, wpg_ref, wpp_ref, g3_ref, gf_ref,
                    o_ref, gbuf, sem):
    i = pl.program_id(0)
    n = pl.num_programs(0)
    tm = x1_ref.shape[0]

    def issue(tile):
        base = tile * tm * TOP_K

        for k in range(TOP_K):
            _gather_rows(yb_hbm, gbuf.at[k], sem.at[0], lambda r, k=k: dest_ref[base + r * TOP_K + k],
                         tm // SUBLANES, 0)

    @pl.when(i == 0)
    def _():
        issue(0)

    for k in range(TOP_K):
        pltpu.make_async_copy(yb_hbm.at[pl.ds(0, tm // SUBLANES)], gbuf.at[k], sem.at[0]).wait()
    gates = gate_ref[...]
    moe = gates[:, 0:1] * gbuf[0].reshape(tm, D_MODEL)
    for k in range(1, TOP_K):
        moe = moe + gates[:, k:k + 1] * gbuf[k].reshape(tm, D_MODEL)
    x2 = x1_ref[...] + moe

    @pl.when(i + 1 < n)
    def _():
        issue(i + 1)

    h3 = _rms(x2, g3_ref[...]).astype(BF16)
    pg = _sigmoid(jnp.dot(h3, wpg_ref[...], preferred_element_type=F32))
    pp = jnp.dot(p_ref[...].astype(BF16), wpp_ref[...], preferred_element_type=F32)
    x3 = x2 + pg * pp
    o_ref[...] = _rms(x3, gf_ref[...])


def _combine(dest, yb, gates, x1, p2, wpg, wpp, g3, gf, tm):
    L = x1.shape[0]
    const = lambda *shape: pl.BlockSpec(shape, lambda i, d: (0,) * len(shape), pipeline_mode=pl.Buffered(1))
    row = lambda w: pl.BlockSpec((tm, w), lambda i, d: (i, 0))
    grid_spec = pltpu.PrefetchScalarGridSpec(
        num_scalar_prefetch=1,
        grid=(L // tm,),
        in_specs=[
            pl.BlockSpec(memory_space=pl.ANY),
            row(ROUTER_PAD), row(D_MODEL), row(PLE_DIM),
            const(D_MODEL, D_MODEL), const(PLE_DIM, D_MODEL), const(1, D_MODEL), const(1, D_MODEL),
        ],
        out_specs=row(D_MODEL),
        scratch_shapes=[
            pltpu.VMEM((TOP_K, tm // SUBLANES, SUBLANES, D_MODEL), F32),
            pltpu.SemaphoreType.DMA((1,)),
        ],
    )
    return pl.pallas_call(
        _combine_kernel,
        name="combine_ple",
        grid_spec=grid_spec,
        out_shape=jax.ShapeDtypeStruct((L, D_MODEL), F32),
        compiler_params=_cparams(("arbitrary",), 48),
    )(dest, yb, gates, x1, p2, wpg, wpp, g3, gf)


def _tile(L, pref):
    return min(pref, L)


def _layer(x2, p2, pos, norm1_g, w_in, lam_re, lam_im, log_dt, b_re, b_im, c_re, c_im, d_skip, w_glu,
           b_glu, w_bs, w_br, w_out, norm2_g, router_w, router_b, w_gu, b_gu, w_dn, b_dn, norm3_g,
           w_pg, w_pp, out_g):
    L = x2.shape[0]
    G, N, E = SSM_GROUPS, SSM_STATE, N_EXPERTS

    cuts = SSM_WIDTH + 4 * RET_WIDTH
    w_in_bf = jnp.concatenate([w_in[:, cuts:], w_in[:, :cuts]], axis=1).astype(BF16)
    z = _inproj(x2, norm1_g[None, :], w_in_bf, _tile(L, 1024), 1024)
    ms_blk, mr_blk = 0, 1
    u_blk, q_blk, k_blk, v_blk, g_blk = 4, 5, 6, 7, 8

    ab_re, ab_im, bb_re, bb_im = _s5_prep(lam_re, lam_im, log_dt, b_re, b_im)
    eye = jnp.eye(SUBLANES, dtype=F32)
    gpb = LANES // SSM_GROUP
    def bpack(bb):
        t = bb.reshape(N_UBLK, gpb, SSM_GROUP, N)
        return jnp.einsum('bgcn,gh->bgchn', t, eye).reshape(N_UBLK, LANES, ST_PER_UBLK)
    def cpack(c):
        t = c.reshape(N_UBLK, gpb, SSM_GROUP, N)
        return jnp.einsum('bgcn,gh->bgnhc', t, eye).reshape(N_UBLK, ST_PER_UBLK, LANES)
    bblk = jnp.concatenate([bpack(bb_re), bpack(bb_im)], axis=-1).astype(BF16)
    cblk = jnp.concatenate([cpack(c_re), -cpack(c_im)], axis=1).astype(BF16)
    atile = lambda a: a[::SSM_GROUP].reshape(N_SVREG, SUBLANES, LANES)
    ys = _s5(z, u_blk, bblk, cblk, atile(ab_re), atile(ab_im), d_skip.reshape(1, SSM_WIDTH),
             w_glu.astype(BF16), b_glu[None, :], _tile(L, 512))

    yr = _retention(z, (q_blk, k_blk, v_blk, g_blk), pos, _tile(L, 512))

    rw = jnp.pad(router_w, ((0, 0), (0, ROUTER_PAD - E)))
    rwh = rw.astype(BF16)
    rwl = (rw - rwh.astype(F32)).astype(BF16)
    rb = jnp.pad(router_b, (0, ROUTER_PAD - E), constant_values=NEG_BIG)[None, :]
    x1, h2, idx, gates, rank, cnt = _merge(
        ys, yr, z, ms_blk, mr_blk, x2, w_bs.astype(BF16), w_br.astype(BF16), w_out.astype(BF16),
        norm2_g[None, :], rwh, rwl, rb, _tile(L, 256))

    item_e, item_g0, item_nc, item_no, pstart, n_rows = _expert_items(cnt[0, :E], L * TOP_K)
    idx4, rank4 = idx[:, :TOP_K], rank[:, :TOP_K]
    dest = (pstart[idx4] + rank4).astype(jnp.int32)
    tok = jnp.broadcast_to(jnp.arange(L, dtype=jnp.int32)[:, None], (L, TOP_K))
    row_tok = jnp.zeros((n_rows,), jnp.int32).at[dest.reshape(-1)].set(tok.reshape(-1))

    yb = _experts(item_e, item_g0, item_nc, item_no, row_tok, h2, w_gu, w_dn, b_gu[:, None, :],
                  b_dn[:, None, :], EXPERT_FF // ITEM_SUBS)

    return _combine(dest.reshape(-1), yb, gates, x1, p2, w_pg.astype(BF16), w_pp.astype(BF16),
                    norm3_g[None, :], out_g[None, :], _tile(L, 256))


def kernel(x, p, positions, norm1_g, w_in, ssm_lam_re, ssm_lam_im, ssm_log_dt, ssm_b_re, ssm_b_im, ssm_c_re, ssm_c_im, ssm_d, ssm_w_glu, ssm_b_glu, w_branch_ssm, w_branch_ret, w_out, norm2_g, router_w, router_b, exp_w_gate_up, exp_b_gate_up, exp_w_down, exp_b_down, norm3_g, ple_w_gate, ple_w_proj, final_norm_g):
    bsz, L, d = x.shape
    depth = w_in.shape[0]
    assert bsz == 1 and depth == 1 and d == D_MODEL
    out = _layer(
        x[0], p[0, 0], positions.reshape(L, 1), norm1_g[0], w_in[0], ssm_lam_re[0], ssm_lam_im[0],
        ssm_log_dt[0], ssm_b_re[0], ssm_b_im[0], ssm_c_re[0], ssm_c_im[0], ssm_d[0], ssm_w_glu[0],
        ssm_b_glu[0], w_branch_ssm[0], w_branch_ret[0], w_out[0], norm2_g[0], router_w[0], router_b[0],
        exp_w_gate_up[0], exp_b_gate_up[0], exp_w_down[0], exp_b_down[0], norm3_g[0], ple_w_gate[0],
        ple_w_proj[0], final_norm_g)
    return out[None]
```

```python
import functools
import math

import jax
import jax.numpy as jnp
from jax import lax
from jax.experimental import pallas as pl
from jax.experimental.pallas import tpu as pltpu

F32 = jnp.float32
BF16 = jnp.bfloat16

D_MODEL = 2048
PLE_DIM = 256
SSM_WIDTH = D_MODEL // 2
SSM_GROUP = 16
SSM_GROUPS = SSM_WIDTH // SSM_GROUP
SSM_STATE = 64
RET_HEADS = 8
RET_HEAD_DIM = 128
RET_WIDTH = RET_HEADS * RET_HEAD_DIM
RET_CHUNK = 128
ROPE_BASE = 10000.0
N_EXPERTS = 32
TOP_K = 4
EXPERT_FF = D_MODEL
SWIGLU_LIMIT = 7.0
SWIGLU_ALPHA = 1.702
NORM_EPS = 1e-6
IN_COLS = SSM_WIDTH + 4 * RET_WIDTH + 2 * D_MODEL

LANES = 128
SUBLANES = 8
MXU_DIM = 256
ROUTER_PAD = LANES
NEG_BIG = -1e30

MIB = 1024 * 1024


def _cparams(sem, vmem_mib):
    return pltpu.CompilerParams(dimension_semantics=sem, vmem_limit_bytes=vmem_mib * MIB)


def _rms(xf, g):
    ms = jnp.mean(xf * xf, axis=-1, keepdims=True)
    return xf * lax.rsqrt(ms + NORM_EPS) * g


def _sigmoid(x):
    return 1.0 / (1.0 + jnp.exp(-x))


def _gather_rows(src3, dst3, sem, index_of, n_groups, dst_group0, inline=False, first_group=0):
    def body(j, c):
        for u in range(SUBLANES):
            row = index_of(j * SUBLANES + u)
            pltpu.make_async_copy(src3.at[row >> 3, pl.ds(row & (SUBLANES - 1), 1), :],
                                  dst3.at[dst_group0 + j, pl.ds(u, 1), :], sem).start()
        return c

    if inline:
        for j in range(first_group, first_group + n_groups):
            body(j, 0)
    else:
        lax.fori_loop(first_group, first_group + n_groups, body, 0)


def _inproj_kernel(x_ref, g_ref, w_ref, o_ref, h_scr):
    @pl.when(pl.program_id(1) == 0)
    def _():
        h_scr[...] = _rms(x_ref[...], g_ref[...]).astype(BF16)

    o_ref[...] = jnp.dot(h_scr[...], w_ref[...], preferred_element_type=F32).astype(o_ref.dtype)


def _inproj(x2, g1, w_in_bf, tm, tn):
    L = x2.shape[0]
    n_cols = w_in_bf.shape[1]
    return pl.pallas_call(
        _inproj_kernel,
        name="inproj",
        grid=(L // tm, n_cols // tn),
        in_specs=[
            pl.BlockSpec((tm, D_MODEL), lambda i, j: (i, 0)),
            pl.BlockSpec((1, D_MODEL), lambda i, j: (0, 0)),
            pl.BlockSpec((D_MODEL, tn), lambda i, j: (0, j)),
        ],
        out_specs=pl.BlockSpec((tm, tn), lambda i, j: (i, j)),
        out_shape=jax.ShapeDtypeStruct((L, n_cols), BF16),
        scratch_shapes=[pltpu.VMEM((tm, D_MODEL), BF16)],
        compiler_params=_cparams(("arbitrary", "arbitrary"), 48),
    )(x2, g1, w_in_bf)


def _s5_prep_kernel(lr_ref, li_ref, ldt_ref, bre_ref, bim_ref, abre_ref, abim_ref, bbre_ref, bbim_ref):
    lr, li = lr_ref[...], li_ref[...]
    dt = jnp.exp(ldt_ref[...])
    mag = jnp.exp(lr * dt)
    ab_re, ab_im = mag * jnp.cos(li * dt), mag * jnp.sin(li * dt)
    den = lr * lr + li * li
    nr, ni = ab_re - 1.0, ab_im
    f_re = (nr * lr + ni * li) / den
    f_im = (ni * lr - nr * li) / den
    bre, bim = bre_ref[...], bim_ref[...]
    abre_ref[...] = ab_re
    abim_ref[...] = ab_im
    bbre_ref[...] = f_re * bre - f_im * bim
    bbim_ref[...] = f_re * bim + f_im * bre


def _s5_prep(lam_re, lam_im, log_dt, b_re, b_im):
    rep = lambda a: jnp.repeat(a, SSM_GROUP, axis=0)
    bt = lambda b: jnp.transpose(b, (0, 2, 1)).reshape(SSM_WIDTH, SSM_STATE)
    shp = jax.ShapeDtypeStruct((SSM_WIDTH, SSM_STATE), F32)
    return pl.pallas_call(_s5_prep_kernel, name="s5_prep", out_shape=(shp, shp, shp, shp))(
        rep(lam_re), rep(lam_im), rep(log_dt[:, None]), bt(b_re), bt(b_im))


def _gelu_tanh(x):
    return 0.5 * x * (1.0 + jnp.tanh(math.sqrt(2.0 / math.pi) * (x + 0.044715 * (x * x * x))))


N_UBLK = SSM_WIDTH // LANES
ST_PER_UBLK = (LANES // SSM_GROUP) * SSM_STATE
TILES_PER_UBLK = ST_PER_UBLK // LANES
N_SVREG = SSM_GROUPS * SSM_STATE // (LANES * SUBLANES)


def _s5_kernel(u_ref, bblk_ref, cblk_ref, are_ref, aim_ref, d_ref, wglu_ref, bglu_ref, o_ref,
               sre, sim, carry):
    tm = u_ref.shape[0]

    @pl.when(pl.program_id(0) == 0)
    def _():
        carry[...] = jnp.zeros_like(carry)

    u = u_ref[...]
    for b in range(N_UBLK):
        bu = jnp.dot(u[:, b * LANES:(b + 1) * LANES], bblk_ref[b], preferred_element_type=F32)
        v = (b * TILES_PER_UBLK) // SUBLANES
        for q in range(TILES_PER_UBLK):
            k = (b * TILES_PER_UBLK + q) % SUBLANES
            sre[v, pl.ds(k, tm, stride=SUBLANES), :] = bu[:, q * LANES:(q + 1) * LANES]
            sim[v, pl.ds(k, tm, stride=SUBLANES), :] = bu[:, ST_PER_UBLK + q * LANES:ST_PER_UBLK + (q + 1) * LANES]

    are, aim = are_ref[...], aim_ref[...]

    def step(t, c):
        cre, cim = c
        r0 = pl.multiple_of(t * SUBLANES, SUBLANES)
        bre = sre[:, pl.ds(r0, SUBLANES), :]
        bim = sim[:, pl.ds(r0, SUBLANES), :]
        nre = are * cre - aim * cim + bre
        nim = are * cim + aim * cre + bim
        sre[:, pl.ds(r0, SUBLANES), :] = nre
        sim[:, pl.ds(r0, SUBLANES), :] = nim
        return nre, nim

    cre, cim = lax.fori_loop(0, tm, step, (carry[0], carry[1]), unroll=4)
    carry[0] = cre
    carry[1] = cim

    ys = []
    for b in range(N_UBLK):
        v = (b * TILES_PER_UBLK) // SUBLANES
        k0 = (b * TILES_PER_UBLK) % SUBLANES
        parts = [sre[v, pl.ds(k0 + q, tm, stride=SUBLANES), :] for q in range(TILES_PER_UBLK)]
        parts += [sim[v, pl.ds(k0 + q, tm, stride=SUBLANES), :] for q in range(TILES_PER_UBLK)]
        sb = jnp.concatenate(parts, axis=-1).astype(BF16)
        ys.append(jnp.dot(sb, cblk_ref[b], preferred_element_type=F32))
    y = jnp.concatenate(ys, axis=-1) + d_ref[...] * u.astype(F32)
    y = _gelu_tanh(y)
    gl = jnp.dot(y.astype(BF16), wglu_ref[...], preferred_element_type=F32) + bglu_ref[...]
    o_ref[...] = (y * _sigmoid(gl)).astype(o_ref.dtype)


def _s5(z, u_blk, bblk, cblk, are, aim, d_skip, wglu, bglu, tm):
    L = z.shape[0]
    const = lambda *shape: pl.BlockSpec(shape, lambda i: (0,) * len(shape))
    return pl.pallas_call(
        _s5_kernel,
        name="s5_branch",
        grid=(L // tm,),
        in_specs=[
            pl.BlockSpec((tm, SSM_WIDTH), lambda i: (i, u_blk)),
            const(N_UBLK, LANES, 2 * ST_PER_UBLK),
            const(N_UBLK, 2 * ST_PER_UBLK, LANES),
            const(N_SVREG, SUBLANES, LANES),
            const(N_SVREG, SUBLANES, LANES),
            const(1, SSM_WIDTH),
            const(SSM_WIDTH, SSM_WIDTH),
            const(1, SSM_WIDTH),
        ],
        out_specs=pl.BlockSpec((tm, SSM_WIDTH), lambda i: (i, 0)),
        out_shape=jax.ShapeDtypeStruct((L, SSM_WIDTH), BF16),
        scratch_shapes=[
            pltpu.VMEM((N_SVREG, tm * SUBLANES, LANES), F32),
            pltpu.VMEM((N_SVREG, tm * SUBLANES, LANES), F32),
            pltpu.VMEM((2, N_SVREG, SUBLANES, LANES), F32),
        ],
        compiler_params=_cparams(("arbitrary",), 48),
    )(z, bblk, cblk, are, aim, d_skip, wglu, bglu)


def _ret_kernel(q_ref, k_ref, v_ref, g_ref, pos_ref, inv_ref, decay_ref, zeta_ref, xi_ref, cd_ref,
                o_ref, r_scr):
    tm = q_ref.shape[0]
    C, dk = RET_CHUNK, RET_HEAD_DIM

    @pl.when(pl.program_id(0) == 0)
    def _():
        r_scr[...] = jnp.zeros_like(r_scr)

    ang = pos_ref[...].astype(F32) * inv_ref[...]
    cs, sn = jnp.cos(ang), jnp.sin(ang)
    cf = jnp.concatenate([cs, cs], axis=-1)
    sf = jnp.concatenate([-sn, sn], axis=-1)

    def rot(xh):
        return xh * cf + pltpu.roll(xh, dk // 2, axis=1) * sf

    nt = (((1,), (1,)), ((), ()))
    for h in range(RET_HEADS):
        cols = slice(h * dk, (h + 1) * dk)
        qr = rot(q_ref[:, cols].astype(F32))
        kr = rot(k_ref[:, cols].astype(F32))
        for n in range(tm // C):
            rows = slice(n * C, (n + 1) * C)
            qc = qr[rows].astype(BF16)
            kcf = kr[rows]
            vc = v_ref[rows, cols]
            sc = lax.dot_general(qc, kcf.astype(BF16), nt, preferred_element_type=F32) * decay_ref[h]
            inner = jnp.dot(sc.astype(BF16), vc, preferred_element_type=F32)
            r_prev = r_scr[h]
            cross = jnp.dot(qc, r_prev.astype(BF16), preferred_element_type=F32) * xi_ref[h]
            o = inner + cross
            kzt = jnp.transpose(kcf * zeta_ref[h]).astype(BF16)
            r_scr[h] = r_prev * cd_ref[h] + jnp.dot(kzt, vc, preferred_element_type=F32)
            mu = jnp.mean(o, axis=-1, keepdims=True)
            oc = o - mu
            on = oc * lax.rsqrt(jnp.mean(oc * oc, axis=-1, keepdims=True) + NORM_EPS)
            gate = g_ref[rows, cols].astype(F32)
            o_ref[rows, cols] = (gate * _sigmoid(gate) * on).astype(o_ref.dtype)


def _retention(z, blks, pos, tm):
    L = z.shape[0]
    H, C, dk = RET_HEADS, RET_CHUNK, RET_HEAD_DIM
    half = dk // 2
    inv = (ROPE_BASE ** (-jnp.arange(half, dtype=F32) / half))[None, :]
    log_gamma = jnp.log(1.0 - jnp.exp2(-5.0 - jnp.arange(H, dtype=F32)))
    idx = jnp.arange(C, dtype=F32)
    rel = idx[:, None] - idx[None, :]
    scale = dk ** -0.5
    decay = jnp.where(rel >= 0, jnp.exp(jnp.maximum(rel, 0.0)[None] * log_gamma[:, None, None]), 0.0) * scale
    zeta = jnp.exp((C - 1.0 - idx)[None, :] * log_gamma[:, None]) * scale
    xi = jnp.exp((idx + 1.0)[None, :] * log_gamma[:, None])
    bc = lambda a: jnp.broadcast_to(a[:, :, None], (H, C, dk))
    cd = jnp.broadcast_to(jnp.exp(C * log_gamma)[:, None, None], (H, 1, dk))
    const = lambda *shape: pl.BlockSpec(shape, lambda i: (0,) * len(shape))
    qb, kb, vb, gb = blks
    return pl.pallas_call(
        _ret_kernel,
        name="retention",
        grid=(L // tm,),
        in_specs=[
            pl.BlockSpec((tm, RET_WIDTH), lambda i: (i, qb)),
            pl.BlockSpec((tm, RET_WIDTH), lambda i: (i, kb)),
            pl.BlockSpec((tm, RET_WIDTH), lambda i: (i, vb)),
            pl.BlockSpec((tm, RET_WIDTH), lambda i: (i, gb)),
            pl.BlockSpec((tm, 1), lambda i: (i, 0)),
            const(1, half),
            const(H, C, C),
            const(H, C, dk),
            const(H, C, dk),
            const(H, 1, dk),
        ],
        out_specs=pl.BlockSpec((tm, RET_WIDTH), lambda i: (i, 0)),
        out_shape=jax.ShapeDtypeStruct((L, RET_WIDTH), BF16),
        scratch_shapes=[pltpu.VMEM((H, dk, dk), F32)],
        compiler_params=_cparams(("arbitrary",), 48),
    )(z, z, z, z, pos, inv, decay, bc(zeta), bc(xi), cd)


def _merge_kernel(ys_ref, yr_ref, ms_ref, mr_ref, x_ref, ps_ref, pr_ref, wo_ref, g2_ref,
                  rwh_ref, rwl_ref, rb_ref, tril_ref,
                  x1_ref, h2_ref, idx_ref, gate_ref, rank_ref, cnt_ref, carry):
    tm = x_ref.shape[0]

    @pl.when(pl.program_id(0) == 0)
    def _():
        carry[...] = jnp.zeros_like(carry)

    a = jnp.dot(ys_ref[...], ps_ref[...], preferred_element_type=F32)
    b = jnp.dot(yr_ref[...], pr_ref[...], preferred_element_type=F32)
    merged = _sigmoid(ms_ref[...].astype(F32)) * a + _sigmoid(mr_ref[...].astype(F32)) * b
    x1 = x_ref[...] + jnp.dot(merged.astype(BF16), wo_ref[...], preferred_element_type=F32)
    x1_ref[...] = x1
    h2 = _rms(x1, g2_ref[...])
    h2_ref[...] = h2.reshape(h2_ref.shape)

    hh = h2.astype(BF16)
    hl = (h2 - hh.astype(F32)).astype(BF16)
    logits = (jnp.dot(hh, rwh_ref[...], preferred_element_type=F32)
              + jnp.dot(hh, rwl_ref[...], preferred_element_type=F32)
              + jnp.dot(hl, rwh_ref[...], preferred_element_type=F32)) + rb_ref[...]

    lane = lax.broadcasted_iota(jnp.int32, (tm, ROUTER_PAD), 1)
    work = logits
    vals, idxs = [], []
    for _ in range(TOP_K):
        m = jnp.max(work, axis=-1, keepdims=True)
        ix = jnp.min(jnp.where(work == m, lane, ROUTER_PAD), axis=-1, keepdims=True)
        vals.append(m)
        idxs.append(ix)
        work = jnp.where(lane == ix, -jnp.inf, work)
    es = [jnp.exp(v - vals[0]) for v in vals]
    den = es[0] + es[1] + es[2] + es[3]

    onehot = jnp.zeros((tm, ROUTER_PAD), F32)
    for ix in idxs:
        onehot = onehot + (lane == ix).astype(F32)
    cum = jnp.dot(tril_ref[...], onehot.astype(BF16), preferred_element_type=F32) + carry[...]
    carry[...] = carry[...] + jnp.sum(onehot, axis=0, keepdims=True)
    cnt_ref[...] = jnp.broadcast_to(carry[...], cnt_ref.shape).astype(jnp.int32)

    idx_o = jnp.zeros((tm, ROUTER_PAD), jnp.int32)
    gate_o = jnp.zeros((tm, ROUTER_PAD), F32)
    rank_o = jnp.zeros((tm, ROUTER_PAD), jnp.int32)
    for k in range(TOP_K):
        rk = jnp.sum(jnp.where(lane == idxs[k], cum, 0.0), axis=-1, keepdims=True).astype(jnp.int32)
        idx_o = jnp.where(lane == k, idxs[k], idx_o)
        gate_o = jnp.where(lane == k, es[k] / den, gate_o)
        rank_o = jnp.where(lane == k, rk, rank_o)
    idx_ref[...] = idx_o
    gate_ref[...] = gate_o
    rank_ref[...] = rank_o


def _merge(ys, yr, z, ms_blk, mr_blk, x2, ps, pr, wo, g2, rwh, rwl, rb, tm):
    L = x2.shape[0]
    tril = (jnp.arange(tm)[:, None] > jnp.arange(tm)[None, :]).astype(BF16)
    const = lambda *shape: pl.BlockSpec(shape, lambda i: (0,) * len(shape), pipeline_mode=pl.Buffered(1))
    row = lambda w: pl.BlockSpec((tm, w), lambda i: (i, 0))
    return pl.pallas_call(
        _merge_kernel,
        name="merge_router",
        grid=(L // tm,),
        in_specs=[
            row(SSM_WIDTH), row(RET_WIDTH),
            pl.BlockSpec((tm, D_MODEL), lambda i: (i, ms_blk)),
            pl.BlockSpec((tm, D_MODEL), lambda i: (i, mr_blk)),
            row(D_MODEL),
            const(SSM_WIDTH, D_MODEL), const(RET_WIDTH, D_MODEL), const(D_MODEL, D_MODEL),
            const(1, D_MODEL),
            const(D_MODEL, ROUTER_PAD), const(D_MODEL, ROUTER_PAD), const(1, ROUTER_PAD),
            const(tm, tm),
        ],
        out_specs=[
            row(D_MODEL), pl.BlockSpec((tm // SUBLANES, SUBLANES, D_MODEL), lambda i: (i, 0, 0)),
            row(ROUTER_PAD), row(ROUTER_PAD), row(ROUTER_PAD),
            pl.BlockSpec((SUBLANES, ROUTER_PAD), lambda i: (0, 0)),
        ],
        out_shape=[
            jax.ShapeDtypeStruct((L, D_MODEL), F32),
            jax.ShapeDtypeStruct((L // SUBLANES, SUBLANES, D_MODEL), F32),
            jax.ShapeDtypeStruct((L, ROUTER_PAD), jnp.int32),
            jax.ShapeDtypeStruct((L, ROUTER_PAD), F32),
            jax.ShapeDtypeStruct((L, ROUTER_PAD), jnp.int32),
            jax.ShapeDtypeStruct((SUBLANES, ROUTER_PAD), jnp.int32),
        ],
        scratch_shapes=[pltpu.VMEM((1, ROUTER_PAD), F32)],
        compiler_params=_cparams(("arbitrary",), 56),
    )(ys, yr, z, z, x2, ps, pr, wo, g2, rwh, rwl, rb, tril)


SUB_ROWS = 256
ITEM_SUBS = 4
SUB_GROUPS = SUB_ROWS // SUBLANES
ACC_COLS = 512


def _expert_kernel(item_e_ref, item_g0_ref, item_nc_ref, item_no_ref, rowtok_ref,
                   h2_hbm, wgu_ref, wd_ref, bgu_ref, bd_ref, unpack_ref, yb_hbm,
                   xbuf, xb, acc, wgu_bf, wd_bf, gsem, osem):
    w = pl.program_id(0)
    f = pl.program_id(1)
    n_w = pl.num_programs(0)
    n_f = pl.num_programs(1)
    tf = wd_ref.shape[1]
    nc = item_nc_ref[w]
    no = item_no_ref[w]

    def groups(s):
        return pl.ds(s * SUB_GROUPS, SUB_GROUPS)

    def gather_sub(item, s):
        base = item_g0_ref[item] * SUBLANES + s * SUB_ROWS
        _gather_rows(h2_hbm, xbuf, gsem.at[0], lambda r: rowtok_ref[base + r], SUB_GROUPS, s * SUB_GROUPS)

    def gather_wait(s):
        pltpu.make_async_copy(h2_hbm.at[pl.ds(0, SUB_GROUPS)], xbuf.at[groups(s)], gsem.at[0]).wait()

    def out_copy(s, g0):
        return pltpu.make_async_copy(acc.at[groups(s)], yb_hbm.at[pl.ds(g0 + s * SUB_GROUPS, SUB_GROUPS)],
                                     osem.at[0])

    def for_subs(count, fn):
        for s in range(ITEM_SUBS):
            pl.when(s < count)(functools.partial(fn, s))

    @pl.when(f == 0)
    def _():
        @pl.when(w == 0)
        def _():
            for_subs(nc, lambda s: gather_sub(0, s))

        for_subs(nc, gather_wait)

        def stage(s):
            xb[s * SUB_ROWS:(s + 1) * SUB_ROWS, :] = xbuf[groups(s)].reshape(SUB_ROWS, D_MODEL).astype(BF16)

        for_subs(nc, stage)

        @pl.when(w > 0)
        def _():
            for_subs(item_no_ref[jnp.maximum(w - 1, 0)], lambda s: out_copy(s, 0).wait())

        init = bd_ref[0] * (nc > 0).astype(F32)

        def init_acc(s):
            acc[groups(s)] = jnp.broadcast_to(init, (SUB_ROWS, D_MODEL)).reshape(SUB_GROUPS, SUBLANES, D_MODEL)

        for_subs(no, init_acc)

    nxt_item = jnp.minimum(w + 1, n_w - 1)

    @pl.when((w + 1 < n_w) & (f < item_nc_ref[nxt_item]))
    def _():
        gather_sub(nxt_item, f)

    def act(guc):
        g = jnp.minimum(guc, SWIGLU_LIMIT)
        u = jnp.clip(pltpu.roll(guc, LANES - 1, axis=1), -SWIGLU_LIMIT, SWIGLU_LIMIT)
        return g * _sigmoid(SWIGLU_ALPHA * g) * (u + 1.0)

    def rows_block(r0, m, cast):
        x = xb[r0:r0 + m, :]
        n_k = tf // MXU_DIM
        even = (lax.broadcasted_iota(jnp.int32, (m, LANES), 1) & 1) == 0
        rg = slice(r0 // SUBLANES, (r0 + m) // SUBLANES)

        def gate_up(k):
            outs = []
            for n in range(2):
                cols = slice((2 * k + n) * MXU_DIM, (2 * k + n + 1) * MXU_DIM)
                if cast:
                    wt = wgu_ref[0, :, cols].astype(BF16)
                    wgu_bf[:, cols] = wt
                else:
                    wt = wgu_bf[:, cols]
                outs.append(jnp.dot(x, wt, preferred_element_type=F32) + bgu_ref[0, :, cols])
            return outs

        def activate(gus):
            parts = [jnp.where(even, act(gu[:, :LANES]), pltpu.roll(act(gu[:, LANES:]), 1, axis=1))
                     for gu in gus]
            packed = jnp.concatenate(parts, axis=-1).astype(BF16)
            return jnp.dot(packed, unpack_ref[...], preferred_element_type=F32).astype(BF16)

        gus = gate_up(0)
        for k in range(n_k):
            nxt = gate_up(k + 1) if k + 1 < n_k else None
            a = activate(gus)
            rows = slice(k * MXU_DIM, (k + 1) * MXU_DIM)
            if cast:
                wdt = wd_ref[0, rows, :].astype(BF16)
                wd_bf[rows, :] = wdt
            else:
                wdt = wd_bf[rows, :]
            for c in range(D_MODEL // ACC_COLS):
                cs = slice(c * ACC_COLS, (c + 1) * ACC_COLS)
                cur = acc[rg, :, cs].reshape(m, ACC_COLS)
                cur = cur + jnp.dot(a, wdt[:, cs], preferred_element_type=F32)
                acc[rg, :, cs] = cur.reshape(m // SUBLANES, SUBLANES, ACC_COLS)
            gus = nxt

    g0 = item_g0_ref[w]
    item_rows = ITEM_SUBS * SUB_ROWS
    full = nc == ITEM_SUBS
    last = f == n_f - 1

    @pl.when(full & jnp.logical_not(last))
    def _():
        rows_block(0, item_rows, True)

    @pl.when(full & last)
    def _():
        half = ITEM_SUBS // 2
        rows_block(0, item_rows // 2, True)
        for s in range(half):
            out_copy(s, g0).start()
        rows_block(item_rows // 2, item_rows // 2, False)
        for s in range(half, ITEM_SUBS):
            out_copy(s, g0).start()

    @pl.when((nc > 0) & jnp.logical_not(full))
    def _():
        def partial_sub(s):
            rows_block(s * SUB_ROWS, SUB_ROWS, s == 0)
            pl.when(last)(lambda: out_copy(s, g0).start())

        for s in range(ITEM_SUBS - 1):
            pl.when(s < nc)(functools.partial(partial_sub, s))

    @pl.when((nc == 0) & last)
    def _():
        for_subs(no, lambda s: out_copy(s, g0).start())

    @pl.when(last & (w == n_w - 1))
    def _():
        for_subs(no, lambda s: out_copy(s, 0).wait())


def _unpack_matrix():
    lane = jnp.arange(MXU_DIM)
    n, r = lane // LANES, lane % LANES
    natural = n * LANES + (r % 2) * (LANES // 2) + r // 2
    return (natural[:, None] == jnp.arange(MXU_DIM)[None, :]).astype(BF16)


def _expert_items(counts, n_asg):
    E = N_EXPERTS
    item_rows = SUB_ROWS * ITEM_SUBS
    n_items_max = n_asg // item_rows + E
    n_rows = n_asg + E * SUB_ROWS
    pcounts = (counts + SUB_ROWS - 1) // SUB_ROWS * SUB_ROWS
    pend = jnp.cumsum(pcounts)
    pstart = pend - pcounts
    n_it = (pcounts + item_rows - 1) // item_rows
    it_end = jnp.cumsum(n_it)
    it_start = it_end - n_it
    n_items = it_end[-1]
    wv = jnp.arange(n_items_max, dtype=jnp.int32)
    e_w = jnp.minimum(jnp.sum(wv[:, None] >= it_end[None, :], axis=1), E - 1)
    row0 = pstart[e_w] + (wv - it_start[e_w]) * item_rows
    nsb = jnp.clip((pend[e_w] - row0) // SUB_ROWS, 0, ITEM_SUBS)
    used = wv < n_items
    tail0 = pend[-1] + (wv - n_items) * item_rows
    nz = jnp.clip((n_rows - tail0) // SUB_ROWS, 0, ITEM_SUBS)
    item_row0 = jnp.clip(jnp.where(used, row0, tail0), 0, n_rows - SUB_ROWS)
    item_e = jnp.where(used, e_w, e_w[jnp.maximum(n_items - 1, 0)])
    i32 = lambda a: a.astype(jnp.int32)
    return (i32(item_e), i32(item_row0 // SUBLANES), i32(jnp.where(used, nsb, 0)),
            i32(jnp.where(used, nsb, nz)), pstart, n_rows)


def _experts(item_e, item_g0, item_nc, item_no, row_tok, h2, wgu, wd, bgu, bd, tf):
    n_rows = row_tok.shape[0]
    n_items = item_e.shape[0]
    n_f = EXPERT_FF // tf
    item_rows = SUB_ROWS * ITEM_SUBS
    assert tf % MXU_DIM == 0 and n_f == ITEM_SUBS

    def f_eff(w, f, nc):
        return jnp.where(nc[w] > 0, f, n_f - 1)

    grid_spec = pltpu.PrefetchScalarGridSpec(
        num_scalar_prefetch=5,
        grid=(n_items, n_f),
        in_specs=[
            pl.BlockSpec(memory_space=pl.ANY),
            pl.BlockSpec((1, D_MODEL, 2 * tf), lambda w, f, ie, g0, nc, no, rt: (ie[w], 0, f_eff(w, f, nc))),
            pl.BlockSpec((1, tf, D_MODEL), lambda w, f, ie, g0, nc, no, rt: (ie[w], f_eff(w, f, nc), 0)),
            pl.BlockSpec((1, 1, 2 * tf), lambda w, f, ie, g0, nc, no, rt: (ie[w], 0, f_eff(w, f, nc))),
            pl.BlockSpec((1, 1, D_MODEL), lambda w, f, ie, g0, nc, no, rt: (ie[w], 0, 0)),
            pl.BlockSpec((MXU_DIM, MXU_DIM), lambda w, f, ie, g0, nc, no, rt: (0, 0)),
        ],
        out_specs=pl.BlockSpec(memory_space=pl.ANY),
        scratch_shapes=[
            pltpu.VMEM((item_rows // SUBLANES, SUBLANES, D_MODEL), F32),
            pltpu.VMEM((item_rows, D_MODEL), BF16),
            pltpu.VMEM((item_rows // SUBLANES, SUBLANES, D_MODEL), F32),
            pltpu.VMEM((D_MODEL, 2 * tf), BF16),
            pltpu.VMEM((tf, D_MODEL), BF16),
            pltpu.SemaphoreType.DMA((1,)),
            pltpu.SemaphoreType.DMA((1,)),
        ],
    )
    return pl.pallas_call(
        _expert_kernel,
        name="experts",
        grid_spec=grid_spec,
        out_shape=jax.ShapeDtypeStruct((n_rows // SUBLANES, SUBLANES, D_MODEL), F32),
        compiler_params=_cparams(("arbitrary", "arbitrary"), 60),
    )(item_e, item_g0, item_nc, item_no, row_tok, h2, wgu, wd, bgu, bd, _unpack_matrix())


def _combine_kernel(dest_ref, yb_hbm, gate_ref, x1_ref, p_ref, wpg_ref, wpp_ref, g3_ref, gf_ref,
                    o_ref, gbuf, sem):
    i = pl.program_id(0)
    n = pl.num_programs(0)
    tm = x1_ref.shape[0]

    def issue(tile):
        base = tile * tm * TOP_K

        for k in range(TOP_K):
            _gather_rows(yb_hbm, gbuf.at[k], sem.at[0], lambda r, k=k: dest_ref[base + r * TOP_K + k],
                         tm // SUBLANES, 0)

    @pl.when(i == 0)
    def _():
        issue(0)

    for k in range(TOP_K):
        pltpu.make_async_copy(yb_hbm.at[pl.ds(0, tm // SUBLANES)], gbuf.at[k], sem.at[0]).wait()
    gates = gate_ref[...]
    moe = gates[:, 0:1] * gbuf[0].reshape(tm, D_MODEL)
    for k in range(1, TOP_K):
        moe = moe + gates[:, k:k + 1] * gbuf[k].reshape(tm, D_MODEL)
    x2 = x1_ref[...] + moe

    @pl.when(i + 1 < n)
    def _():
        issue(i + 1)

    h3 = _rms(x2, g3_ref[...]).astype(BF16)
    pg = _sigmoid(jnp.dot(h3, wpg_ref[...], preferred_element_type=F32))
    pp = jnp.dot(p_ref[...].astype(BF16), wpp_ref[...], preferred_element_type=F32)
    x3 = x2 + pg * pp
    o_ref[...] = _rms(x3, gf_ref[...])


def _combine(dest, yb, gates, x1, p2, wpg, wpp, g3, gf, tm):
    L = x1.shape[0]
    const = lambda *shape: pl.BlockSpec(shape, lambda i, d: (0,) * len(shape), pipeline_mode=pl.Buffered(1))
    row = lambda w: pl.BlockSpec((tm, w), lambda i, d: (i, 0))
    grid_spec = pltpu.PrefetchScalarGridSpec(
        num_scalar_prefetch=1,
        grid=(L // tm,),
        in_specs=[
            pl.BlockSpec(memory_space=pl.ANY),
            row(ROUTER_PAD), row(D_MODEL), row(PLE_DIM),
            const(D_MODEL, D_MODEL), const(PLE_DIM, D_MODEL), const(1, D_MODEL), const(1, D_MODEL),
        ],
        out_specs=row(D_MODEL),
        scratch_shapes=[
            pltpu.VMEM((TOP_K, tm // SUBLANES, SUBLANES, D_MODEL), F32),
            pltpu.SemaphoreType.DMA((1,)),
        ],
    )
    return pl.pallas_call(
        _combine_kernel,
        name="combine_ple",
        grid_spec=grid_spec,
        out_shape=jax.ShapeDtypeStruct((L, D_MODEL), F32),
        compiler_params=_cparams(("arbitrary",), 48),
    )(dest, yb, gates, x1, p2, wpg, wpp, g3, gf)


def _tile(L, pref):
    return min(pref, L)


def _layer(x2, p2, pos, norm1_g, w_in, lam_re, lam_im, log_dt, b_re, b_im, c_re, c_im, d_skip, w_glu,
           b_glu, w_bs, w_br, w_out, norm2_g, router_w, router_b, w_gu, b_gu, w_dn, b_dn, norm3_g,
           w_pg, w_pp, out_g):
    L = x2.shape[0]
    G, N, E = SSM_GROUPS, SSM_STATE, N_EXPERTS

    cuts = SSM_WIDTH + 4 * RET_WIDTH
    w_in_bf = jnp.concatenate([w_in[:, cuts:], w_in[:, :cuts]], axis=1).astype(BF16)
    z = _inproj(x2, norm1_g[None, :], w_in_bf, _tile(L, 1024), 1024)
    ms_blk, mr_blk = 0, 1
    u_blk, q_blk, k_blk, v_blk, g_blk = 4, 5, 6, 7, 8

    ab_re, ab_im, bb_re, bb_im = _s5_prep(lam_re, lam_im, log_dt, b_re, b_im)
    eye = jnp.eye(SUBLANES, dtype=F32)
    gpb = LANES // SSM_GROUP
    def bpack(bb):
        t = bb.reshape(N_UBLK, gpb, SSM_GROUP, N)
        return jnp.einsum('bgcn,gh->bgchn', t, eye).reshape(N_UBLK, LANES, ST_PER_UBLK)
    def cpack(c):
        t = c.reshape(N_UBLK, gpb, SSM_GROUP, N)
        return jnp.einsum('bgcn,gh->bgnhc', t, eye).reshape(N_UBLK, ST_PER_UBLK, LANES)
    bblk = jnp.concatenate([bpack(bb_re), bpack(bb_im)], axis=-1).astype(BF16)
    cblk = jnp.concatenate([cpack(c_re), -cpack(c_im)], axis=1).astype(BF16)
    atile = lambda a: a[::SSM_GROUP].reshape(N_SVREG, SUBLANES, LANES)
    ys = _s5(z, u_blk, bblk, cblk, atile(ab_re), atile(ab_im), d_skip.reshape(1, SSM_WIDTH),
             w_glu.astype(BF16), b_glu[None, :], _tile(L, 512))

    yr = _retention(z, (q_blk, k_blk, v_blk, g_blk), pos, _tile(L, 512))

    rw = jnp.pad(router_w, ((0, 0), (0, ROUTER_PAD - E)))
    rwh = rw.astype(BF16)
    rwl = (rw - rwh.astype(F32)).astype(BF16)
    rb = jnp.pad(router_b, (0, ROUTER_PAD - E), constant_values=NEG_BIG)[None, :]
    x1, h2, idx, gates, rank, cnt = _merge(
        ys, yr, z, ms_blk, mr_blk, x2, w_bs.astype(BF16), w_br.astype(BF16), w_out.astype(BF16),
        norm2_g[None, :], rwh, rwl, rb, _tile(L, 256))

    item_e, item_g0, item_nc, item_no, pstart, n_rows = _expert_items(cnt[0, :E], L * TOP_K)
    idx4, rank4 = idx[:, :TOP_K], rank[:, :TOP_K]
    dest = (pstart[idx4] + rank4).astype(jnp.int32)
    tok = jnp.broadcast_to(jnp.arange(L, dtype=jnp.int32)[:, None], (L, TOP_K))
    row_tok = jnp.zeros((n_rows,), jnp.int32).at[dest.reshape(-1)].set(tok.reshape(-1))

    yb = _experts(item_e, item_g0, item_nc, item_no, row_tok, h2, w_gu, w_dn, b_gu[:, None, :],
                  b_dn[:, None, :], EXPERT_FF // ITEM_SUBS)

    return _combine(dest.reshape(-1), yb, gates, x1, p2, w_pg.astype(BF16), w_pp.astype(BF16),
                    norm3_g[None, :], out_g[None, :], _tile(L, 256))


def kernel(x, p, positions, norm1_g, w_in, ssm_lam_re, ssm_lam_im, ssm_log_dt, ssm_b_re, ssm_b_im, ssm_c_re, ssm_c_im, ssm_d, ssm_w_glu, ssm_b_glu, w_branch_ssm, w_branch_ret, w_out, norm2_g, router_w, router_b, exp_w_gate_up, exp_b_gate_up, exp_w_down, exp_b_down, norm3_g, ple_w_gate, ple_w_proj, final_norm_g):
    bsz, L, d = x.shape
    depth = w_in.shape[0]
    assert bsz == 1 and depth == 1 and d == D_MODEL
    out = _layer(
        x[0], p[0, 0], positions.reshape(L, 1), norm1_g[0], w_in[0], ssm_lam_re[0], ssm_lam_im[0],
        ssm_log_dt[0], ssm_b_re[0], ssm_b_im[0], ssm_c_re[0], ssm_c_im[0], ssm_d[0], ssm_w_glu[0],
        ssm_b_glu[0], w_branch_ssm[0], w_branch_ret[0], w_out[0], norm2_g[0], router_w[0], router_b[0],
        exp_w_gate_up[0], exp_b_gate_up[0], exp_w_down[0], exp_b_down[0], norm3_g[0], ple_w_gate[0],
        ple_w_proj[0], final_norm_g)
    return out[None]
```

```python
import functools
import math

import jax
import jax.numpy as jnp
from jax import lax
from jax.experimental import pallas as pl
from jax.experimental.pallas import tpu as pltpu

F32 = jnp.float32
BF16 = jnp.bfloat16

D_MODEL = 2048
PLE_DIM = 256
SSM_WIDTH = D_MODEL // 2
SSM_GROUP = 16
SSM_GROUPS = SSM_WIDTH // SSM_GROUP
SSM_STATE = 64
RET_HEADS = 8
RET_HEAD_DIM = 128
RET_WIDTH = RET_HEADS * RET_HEAD_DIM
RET_CHUNK = 128
ROPE_BASE = 10000.0
N_EXPERTS = 32
TOP_K = 4
EXPERT_FF = D_MODEL
SWIGLU_LIMIT = 7.0
SWIGLU_ALPHA = 1.702
NORM_EPS = 1e-6
IN_COLS = SSM_WIDTH + 4 * RET_WIDTH + 2 * D_MODEL

LANES = 128
SUBLANES = 8
MXU_DIM = 256
ROUTER_PAD = LANES
NEG_BIG = -1e30

MIB = 1024 * 1024


def _cparams(sem, vmem_mib):
    return pltpu.CompilerParams(dimension_semantics=sem, vmem_limit_bytes=vmem_mib * MIB)


def _rms(xf, g):
    ms = jnp.mean(xf * xf, axis=-1, keepdims=True)
    return xf * lax.rsqrt(ms + NORM_EPS) * g


def _sigmoid(x):
    return 1.0 / (1.0 + jnp.exp(-x))


def _gather_rows(src2, dst3, sem, index_of, n_groups, dst_group0, inline=False, first_group=0):
    def body(j, c):
        for u in range(SUBLANES):
            row = index_of(j * SUBLANES + u)
            pltpu.make_async_copy(src2.at[pl.ds(row, 1), :],
                                  dst3.at[dst_group0 + j, pl.ds(u, 1), :], sem).start()
        return c

    if inline:
        for j in range(first_group, first_group + n_groups):
            body(j, 0)
    else:
        lax.fori_loop(first_group, first_group + n_groups, body, 0)


def _inproj_kernel(x_ref, g_ref, w_ref, o_ref, h_scr):
    @pl.when(pl.program_id(1) == 0)
    def _():
        h_scr[...] = _rms(x_ref[...], g_ref[...]).astype(BF16)

    o_ref[...] = jnp.dot(h_scr[...], w_ref[...], preferred_element_type=F32).astype(o_ref.dtype)


def _inproj(x2, g1, w_in_bf, tm, tn):
    L = x2.shape[0]
    n_cols = w_in_bf.shape[1]
    return pl.pallas_call(
        _inproj_kernel,
        name="inproj",
        grid=(L // tm, n_cols // tn),
        in_specs=[
            pl.BlockSpec((tm, D_MODEL), lambda i, j: (i, 0)),
            pl.BlockSpec((1, D_MODEL), lambda i, j: (0, 0)),
            pl.BlockSpec((D_MODEL, tn), lambda i, j: (0, j)),
        ],
        out_specs=pl.BlockSpec((tm, tn), lambda i, j: (i, j)),
        out_shape=jax.ShapeDtypeStruct((L, n_cols), BF16),
        scratch_shapes=[pltpu.VMEM((tm, D_MODEL), BF16)],
        compiler_params=_cparams(("arbitrary", "arbitrary"), 48),
    )(x2, g1, w_in_bf)


def _s5_prep_kernel(lr_ref, li_ref, ldt_ref, bre_ref, bim_ref, abre_ref, abim_ref, bbre_ref, bbim_ref):
    lr, li = lr_ref[...], li_ref[...]
    dt = jnp.exp(ldt_ref[...])
    mag = jnp.exp(lr * dt)
    ab_re, ab_im = mag * jnp.cos(li * dt), mag * jnp.sin(li * dt)
    den = lr * lr + li * li
    nr, ni = ab_re - 1.0, ab_im
    f_re = (nr * lr + ni * li) / den
    f_im = (ni * lr - nr * li) / den
    bre, bim = bre_ref[...], bim_ref[...]
    abre_ref[...] = ab_re
    abim_ref[...] = ab_im
    bbre_ref[...] = f_re * bre - f_im * bim
    bbim_ref[...] = f_re * bim + f_im * bre


def _s5_prep(lam_re, lam_im, log_dt, b_re, b_im):
    rep = lambda a: jnp.repeat(a, SSM_GROUP, axis=0)
    bt = lambda b: jnp.transpose(b, (0, 2, 1)).reshape(SSM_WIDTH, SSM_STATE)
    shp = jax.ShapeDtypeStruct((SSM_WIDTH, SSM_STATE), F32)
    return pl.pallas_call(_s5_prep_kernel, name="s5_prep", out_shape=(shp, shp, shp, shp))(
        rep(lam_re), rep(lam_im), rep(log_dt[:, None]), bt(b_re), bt(b_im))


def _gelu_tanh(x):
    return 0.5 * x * (1.0 + jnp.tanh(math.sqrt(2.0 / math.pi) * (x + 0.044715 * (x * x * x))))


N_UBLK = SSM_WIDTH // LANES
ST_PER_UBLK = (LANES // SSM_GROUP) * SSM_STATE
TILES_PER_UBLK = ST_PER_UBLK // LANES
N_SVREG = SSM_GROUPS * SSM_STATE // (LANES * SUBLANES)


def _s5_kernel(u_ref, bblk_ref, cblk_ref, are_ref, aim_ref, d_ref, wglu_ref, bglu_ref, o_ref,
               sre, sim, carry):
    tm = u_ref.shape[0]

    @pl.when(pl.program_id(0) == 0)
    def _():
        carry[...] = jnp.zeros_like(carry)

    u = u_ref[...]
    for b in range(N_UBLK):
        bu = jnp.dot(u[:, b * LANES:(b + 1) * LANES], bblk_ref[b], preferred_element_type=F32)
        v = (b * TILES_PER_UBLK) // SUBLANES
        for q in range(TILES_PER_UBLK):
            k = (b * TILES_PER_UBLK + q) % SUBLANES
            sre[v, pl.ds(k, tm, stride=SUBLANES), :] = bu[:, q * LANES:(q + 1) * LANES]
            sim[v, pl.ds(k, tm, stride=SUBLANES), :] = bu[:, ST_PER_UBLK + q * LANES:ST_PER_UBLK + (q + 1) * LANES]

    are, aim = are_ref[...], aim_ref[...]

    def step(t, c):
        cre, cim = c
        r0 = pl.multiple_of(t * SUBLANES, SUBLANES)
        bre = sre[:, pl.ds(r0, SUBLANES), :]
        bim = sim[:, pl.ds(r0, SUBLANES), :]
        nre = are * cre - aim * cim + bre
        nim = are * cim + aim * cre + bim
        sre[:, pl.ds(r0, SUBLANES), :] = nre
        sim[:, pl.ds(r0, SUBLANES), :] = nim
        return nre, nim

    cre, cim = lax.fori_loop(0, tm, step, (carry[0], carry[1]), unroll=4)
    carry[0] = cre
    carry[1] = cim

    ys = []
    for b in range(N_UBLK):
        v = (b * TILES_PER_UBLK) // SUBLANES
        k0 = (b * TILES_PER_UBLK) % SUBLANES
        parts = [sre[v, pl.ds(k0 + q, tm, stride=SUBLANES), :] for q in range(TILES_PER_UBLK)]
        parts += [sim[v, pl.ds(k0 + q, tm, stride=SUBLANES), :] for q in range(TILES_PER_UBLK)]
        sb = jnp.concatenate(parts, axis=-1).astype(BF16)
        ys.append(jnp.dot(sb, cblk_ref[b], preferred_element_type=F32))
    y = jnp.concatenate(ys, axis=-1) + d_ref[...] * u.astype(F32)
    y = _gelu_tanh(y)
    gl = jnp.dot(y.astype(BF16), wglu_ref[...], preferred_element_type=F32) + bglu_ref[...]
    o_ref[...] = (y * _sigmoid(gl)).astype(o_ref.dtype)


def _s5(z, u_blk, bblk, cblk, are, aim, d_skip, wglu, bglu, tm):
    L = z.shape[0]
    const = lambda *shape: pl.BlockSpec(shape, lambda i: (0,) * len(shape))
    return pl.pallas_call(
        _s5_kernel,
        name="s5_branch",
        grid=(L // tm,),
        in_specs=[
            pl.BlockSpec((tm, SSM_WIDTH), lambda i: (i, u_blk)),
            const(N_UBLK, LANES, 2 * ST_PER_UBLK),
            const(N_UBLK, 2 * ST_PER_UBLK, LANES),
            const(N_SVREG, SUBLANES, LANES),
            const(N_SVREG, SUBLANES, LANES),
            const(1, SSM_WIDTH),
            const(SSM_WIDTH, SSM_WIDTH),
            const(1, SSM_WIDTH),
        ],
        out_specs=pl.BlockSpec((tm, SSM_WIDTH), lambda i: (i, 0)),
        out_shape=jax.ShapeDtypeStruct((L, SSM_WIDTH), BF16),
        scratch_shapes=[
            pltpu.VMEM((N_SVREG, tm * SUBLANES, LANES), F32),
            pltpu.VMEM((N_SVREG, tm * SUBLANES, LANES), F32),
            pltpu.VMEM((2, N_SVREG, SUBLANES, LANES), F32),
        ],
        compiler_params=_cparams(("arbitrary",), 48),
    )(z, bblk, cblk, are, aim, d_skip, wglu, bglu)


def _ret_kernel(q_ref, k_ref, v_ref, g_ref, pos_ref, inv_ref, decay_ref, zeta_ref, xi_ref, cd_ref,
                o_ref, r_scr):
    tm = q_ref.shape[0]
    C, dk = RET_CHUNK, RET_HEAD_DIM

    @pl.when(pl.program_id(0) == 0)
    def _():
        r_scr[...] = jnp.zeros_like(r_scr)

    ang = pos_ref[...].astype(F32) * inv_ref[...]
    cs, sn = jnp.cos(ang), jnp.sin(ang)
    cf = jnp.concatenate([cs, cs], axis=-1)
    sf = jnp.concatenate([-sn, sn], axis=-1)

    def rot(xh):
        return xh * cf + pltpu.roll(xh, dk // 2, axis=1) * sf

    nt = (((1,), (1,)), ((), ()))
    for h in range(RET_HEADS):
        cols = slice(h * dk, (h + 1) * dk)
        qr = rot(q_ref[:, cols].astype(F32))
        kr = rot(k_ref[:, cols].astype(F32))
        for n in range(tm // C):
            rows = slice(n * C, (n + 1) * C)
            qc = qr[rows].astype(BF16)
            kcf = kr[rows]
            vc = v_ref[rows, cols]
            sc = lax.dot_general(qc, kcf.astype(BF16), nt, preferred_element_type=F32) * decay_ref[h]
            inner = jnp.dot(sc.astype(BF16), vc, preferred_element_type=F32)
            r_prev = r_scr[h]
            cross = jnp.dot(qc, r_prev.astype(BF16), preferred_element_type=F32) * xi_ref[h]
            o = inner + cross
            kzt = jnp.transpose(kcf * zeta_ref[h]).astype(BF16)
            r_scr[h] = r_prev * cd_ref[h] + jnp.dot(kzt, vc, preferred_element_type=F32)
            mu = jnp.mean(o, axis=-1, keepdims=True)
            oc = o - mu
            on = oc * lax.rsqrt(jnp.mean(oc * oc, axis=-1, keepdims=True) + NORM_EPS)
            gate = g_ref[rows, cols].astype(F32)
            o_ref[rows, cols] = (gate * _sigmoid(gate) * on).astype(o_ref.dtype)


def _retention(z, blks, pos, tm):
    L = z.shape[0]
    H, C, dk = RET_HEADS, RET_CHUNK, RET_HEAD_DIM
    half = dk // 2
    inv = (ROPE_BASE ** (-jnp.arange(half, dtype=F32) / half))[None, :]
    log_gamma = jnp.log(1.0 - jnp.exp2(-5.0 - jnp.arange(H, dtype=F32)))
    idx = jnp.arange(C, dtype=F32)
    rel = idx[:, None] - idx[None, :]
    scale = dk ** -0.5
    decay = jnp.where(rel >= 0, jnp.exp(jnp.maximum(rel, 0.0)[None] * log_gamma[:, None, None]), 0.0) * scale
    zeta = jnp.exp((C - 1.0 - idx)[None, :] * log_gamma[:, None]) * scale
    xi = jnp.exp((idx + 1.0)[None, :] * log_gamma[:, None])
    bc = lambda a: jnp.broadcast_to(a[:, :, None], (H, C, dk))
    cd = jnp.broadcast_to(jnp.exp(C * log_gamma)[:, None, None], (H, 1, dk))
    const = lambda *shape: pl.BlockSpec(shape, lambda i: (0,) * len(shape))
    qb, kb, vb, gb = blks
    return pl.pallas_call(
        _ret_kernel,
        name="retention",
        grid=(L // tm,),
        in_specs=[
            pl.BlockSpec((tm, RET_WIDTH), lambda i: (i, qb)),
            pl.BlockSpec((tm, RET_WIDTH), lambda i: (i, kb)),
            pl.BlockSpec((tm, RET_WIDTH), lambda i: (i, vb)),
            pl.BlockSpec((tm, RET_WIDTH), lambda i: (i, gb)),
            pl.BlockSpec((tm, 1), lambda i: (i, 0)),
            const(1, half),
            const(H, C, C),
            const(H, C, dk),
            const(H, C, dk),
            const(H, 1, dk),
        ],
        out_specs=pl.BlockSpec((tm, RET_WIDTH), lambda i: (i, 0)),
        out_shape=jax.ShapeDtypeStruct((L, RET_WIDTH), BF16),
        scratch_shapes=[pltpu.VMEM((H, dk, dk), F32)],
        compiler_params=_cparams(("arbitrary",), 48),
    )(z, z, z, z, pos, inv, decay, bc(zeta), bc(xi), cd)


def _merge_kernel(ys_ref, yr_ref, ms_ref, mr_ref, x_ref, ps_ref, pr_ref, wo_ref, g2_ref,
                  rwh_ref, rwl_ref, rb_ref, tril_ref,
                  x1_ref, h2_ref, idx_ref, gate_ref, rank_ref, cnt_ref, carry):
    tm = x_ref.shape[0]

    @pl.when(pl.program_id(0) == 0)
    def _():
        carry[...] = jnp.zeros_like(carry)

    a = jnp.dot(ys_ref[...], ps_ref[...], preferred_element_type=F32)
    b = jnp.dot(yr_ref[...], pr_ref[...], preferred_element_type=F32)
    merged = _sigmoid(ms_ref[...].astype(F32)) * a + _sigmoid(mr_ref[...].astype(F32)) * b
    x1 = x_ref[...] + jnp.dot(merged.astype(BF16), wo_ref[...], preferred_element_type=F32)
    x1_ref[...] = x1
    h2 = _rms(x1, g2_ref[...])
    h2_ref[...] = h2

    hh = h2.astype(BF16)
    hl = (h2 - hh.astype(F32)).astype(BF16)
    logits = (jnp.dot(hh, rwh_ref[...], preferred_element_type=F32)
              + jnp.dot(hh, rwl_ref[...], preferred_element_type=F32)
              + jnp.dot(hl, rwh_ref[...], preferred_element_type=F32)) + rb_ref[...]

    lane = lax.broadcasted_iota(jnp.int32, (tm, ROUTER_PAD), 1)
    work = logits
    vals, idxs = [], []
    for _ in range(TOP_K):
        m = jnp.max(work, axis=-1, keepdims=True)
        ix = jnp.min(jnp.where(work == m, lane, ROUTER_PAD), axis=-1, keepdims=True)
        vals.append(m)
        idxs.append(ix)
        work = jnp.where(lane == ix, -jnp.inf, work)
    es = [jnp.exp(v - vals[0]) for v in vals]
    den = es[0] + es[1] + es[2] + es[3]

    onehot = jnp.zeros((tm, ROUTER_PAD), F32)
    for ix in idxs:
        onehot = onehot + (lane == ix).astype(F32)
    cum = jnp.dot(tril_ref[...], onehot.astype(BF16), preferred_element_type=F32) + carry[...]
    carry[...] = carry[...] + jnp.sum(onehot, axis=0, keepdims=True)
    cnt_ref[...] = jnp.broadcast_to(carry[...], cnt_ref.shape).astype(jnp.int32)

    idx_o = jnp.zeros((tm, ROUTER_PAD), jnp.int32)
    gate_o = jnp.zeros((tm, ROUTER_PAD), F32)
    rank_o = jnp.zeros((tm, ROUTER_PAD), jnp.int32)
    for k in range(TOP_K):
        rk = jnp.sum(jnp.where(lane == idxs[k], cum, 0.0), axis=-1, keepdims=True).astype(jnp.int32)
        idx_o = jnp.where(lane == k, idxs[k], idx_o)
        gate_o = jnp.where(lane == k, es[k] / den, gate_o)
        rank_o = jnp.where(lane == k, rk, rank_o)
    idx_ref[...] = idx_o
    gate_ref[...] = gate_o
    rank_ref[...] = rank_o


def _merge(ys, yr, z, ms_blk, mr_blk, x2, ps, pr, wo, g2, rwh, rwl, rb, tm):
    L = x2.shape[0]
    tril = (jnp.arange(tm)[:, None] > jnp.arange(tm)[None, :]).astype(BF16)
    const = lambda *shape: pl.BlockSpec(shape, lambda i: (0,) * len(shape), pipeline_mode=pl.Buffered(1))
    row = lambda w: pl.BlockSpec((tm, w), lambda i: (i, 0))
    return pl.pallas_call(
        _merge_kernel,
        name="merge_router",
        grid=(L // tm,),
        in_specs=[
            row(SSM_WIDTH), row(RET_WIDTH),
            pl.BlockSpec((tm, D_MODEL), lambda i: (i, ms_blk)),
            pl.BlockSpec((tm, D_MODEL), lambda i: (i, mr_blk)),
            row(D_MODEL),
            const(SSM_WIDTH, D_MODEL), const(RET_WIDTH, D_MODEL), const(D_MODEL, D_MODEL),
            const(1, D_MODEL),
            const(D_MODEL, ROUTER_PAD), const(D_MODEL, ROUTER_PAD), const(1, ROUTER_PAD),
            const(tm, tm),
        ],
        out_specs=[
            row(D_MODEL), row(D_MODEL),
            row(ROUTER_PAD), row(ROUTER_PAD), row(ROUTER_PAD),
            pl.BlockSpec((SUBLANES, ROUTER_PAD), lambda i: (0, 0)),
        ],
        out_shape=[
            jax.ShapeDtypeStruct((L, D_MODEL), F32),
            jax.ShapeDtypeStruct((L, D_MODEL), F32),
            jax.ShapeDtypeStruct((L, ROUTER_PAD), jnp.int32),
            jax.ShapeDtypeStruct((L, ROUTER_PAD), F32),
            jax.ShapeDtypeStruct((L, ROUTER_PAD), jnp.int32),
            jax.ShapeDtypeStruct((SUBLANES, ROUTER_PAD), jnp.int32),
        ],
        scratch_shapes=[pltpu.VMEM((1, ROUTER_PAD), F32)],
        compiler_params=_cparams(("arbitrary",), 56),
    )(ys, yr, z, z, x2, ps, pr, wo, g2, rwh, rwl, rb, tril)


def _row_tokens_kernel(dest_ref, out_ref):
    def clear(i, c):
        out_ref[i] = 0
        return c

    def place(t, c):
        for k in range(TOP_K):
            out_ref[dest_ref[t * TOP_K + k]] = t
        return c

    lax.fori_loop(0, out_ref.shape[0], clear, 0, unroll=8)
    lax.fori_loop(0, dest_ref.shape[0] // TOP_K, place, 0, unroll=4)


def _row_tokens(dest, n_rows):
    smem = pl.BlockSpec(memory_space=pltpu.SMEM)
    return pl.pallas_call(
        _row_tokens_kernel, name="row_tokens", in_specs=[smem], out_specs=smem,
        out_shape=jax.ShapeDtypeStruct((n_rows,), jnp.int32))(dest)


SUB_ROWS = 256
ITEM_SUBS = 4
SUB_GROUPS = SUB_ROWS // SUBLANES
ACC_COLS = 512


def _expert_kernel(item_e_ref, item_g0_ref, item_nc_ref, item_no_ref, rowtok_ref,
                   h2_hbm, wgu_ref, wd_ref, bgu_ref, bd_ref, unpack_ref, yb_hbm,
                   xbuf, xb, acc, wgu_bf, wd_bf, gsem, osem):
    w = pl.program_id(0)
    f = pl.program_id(1)
    n_w = pl.num_programs(0)
    n_f = pl.num_programs(1)
    tf = wd_ref.shape[1]
    nc = item_nc_ref[w]
    no = item_no_ref[w]

    def groups(s):
        return pl.ds(s * SUB_GROUPS, SUB_GROUPS)

    def gather_sub(item, s):
        base = item_g0_ref[item] * SUBLANES + s * SUB_ROWS
        _gather_rows(h2_hbm, xbuf, gsem.at[0], lambda r: rowtok_ref[base + r], SUB_GROUPS, s * SUB_GROUPS)

    def gather_wait(s):
        pltpu.make_async_copy(xbuf.at[groups(s)], xbuf.at[groups(s)], gsem.at[0]).wait()

    def out_copy(s, g0):
        row0 = pl.multiple_of((g0 + s * SUB_GROUPS) * SUBLANES, SUBLANES)
        return pltpu.make_async_copy(acc.at[pl.ds(s * SUB_ROWS, SUB_ROWS), :],
                                     yb_hbm.at[pl.ds(row0, SUB_ROWS), :], osem.at[0])

    def for_subs(count, fn):
        for s in range(ITEM_SUBS):
            pl.when(s < count)(functools.partial(fn, s))

    @pl.when(f == 0)
    def _():
        @pl.when(w == 0)
        def _():
            for_subs(nc, lambda s: gather_sub(0, s))

        for_subs(nc, gather_wait)

        def stage(s):
            xb[s * SUB_ROWS:(s + 1) * SUB_ROWS, :] = xbuf[groups(s)].reshape(SUB_ROWS, D_MODEL).astype(BF16)

        for_subs(nc, stage)

        @pl.when(w > 0)
        def _():
            for_subs(item_no_ref[jnp.maximum(w - 1, 0)], lambda s: out_copy(s, 0).wait())

        init = bd_ref[0] * (nc > 0).astype(F32)

        def init_acc(s):
            acc[s * SUB_ROWS:(s + 1) * SUB_ROWS, :] = jnp.broadcast_to(init, (SUB_ROWS, D_MODEL))

        for_subs(no, init_acc)

    nxt_item = jnp.minimum(w + 1, n_w - 1)

    @pl.when((w + 1 < n_w) & (f < item_nc_ref[nxt_item]))
    def _():
        gather_sub(nxt_item, f)

    def act(guc):
        g = jnp.minimum(guc, SWIGLU_LIMIT)
        u = jnp.clip(pltpu.roll(guc, LANES - 1, axis=1), -SWIGLU_LIMIT, SWIGLU_LIMIT)
        return g * _sigmoid(SWIGLU_ALPHA * g) * (u + 1.0)

    def rows_block(r0, m, cast):
        x = xb[r0:r0 + m, :]
        n_k = tf // MXU_DIM
        even = (lax.broadcasted_iota(jnp.int32, (m, LANES), 1) & 1) == 0

        def gate_up(k):
            outs = []
            for n in range(2):
                cols = slice((2 * k + n) * MXU_DIM, (2 * k + n + 1) * MXU_DIM)
                if cast:
                    wt = wgu_ref[0, :, cols].astype(BF16)
                    wgu_bf[:, cols] = wt
                else:
                    wt = wgu_bf[:, cols]
                outs.append(jnp.dot(x, wt, preferred_element_type=F32) + bgu_ref[0, :, cols])
            return outs

        def activate(gus):
            parts = [jnp.where(even, act(gu[:, :LANES]), pltpu.roll(act(gu[:, LANES:]), 1, axis=1))
                     for gu in gus]
            packed = jnp.concatenate(parts, axis=-1).astype(BF16)
            return jnp.dot(packed, unpack_ref[...], preferred_element_type=F32).astype(BF16)

        gus = gate_up(0)
        for k in range(n_k):
            nxt = gate_up(k + 1) if k + 1 < n_k else None
            a = activate(gus)
            rows = slice(k * MXU_DIM, (k + 1) * MXU_DIM)
            if cast:
                wdt = wd_ref[0, rows, :].astype(BF16)
                wd_bf[rows, :] = wdt
            else:
                wdt = wd_bf[rows, :]
            for c in range(D_MODEL // ACC_COLS):
                cs = slice(c * ACC_COLS, (c + 1) * ACC_COLS)
                acc[r0:r0 + m, cs] += jnp.dot(a, wdt[:, cs], preferred_element_type=F32)
            gus = nxt

    g0 = item_g0_ref[w]
    item_rows = ITEM_SUBS * SUB_ROWS
    full = nc == ITEM_SUBS
    last = f == n_f - 1

    @pl.when(full & jnp.logical_not(last))
    def _():
        rows_block(0, item_rows, True)

    @pl.when(full & last)
    def _():
        half = ITEM_SUBS // 2
        rows_block(0, item_rows // 2, True)
        for s in range(half):
            out_copy(s, g0).start()
        rows_block(item_rows // 2, item_rows // 2, False)
        for s in range(half, ITEM_SUBS):
            out_copy(s, g0).start()

    @pl.when((nc > 0) & jnp.logical_not(full))
    def _():
        def partial_sub(s):
            rows_block(s * SUB_ROWS, SUB_ROWS, s == 0)
            pl.when(last)(lambda: out_copy(s, g0).start())

        for s in range(ITEM_SUBS - 1):
            pl.when(s < nc)(functools.partial(partial_sub, s))

    @pl.when((nc == 0) & last)
    def _():
        for_subs(no, lambda s: out_copy(s, g0).start())

    @pl.when(last & (w == n_w - 1))
    def _():
        for_subs(no, lambda s: out_copy(s, 0).wait())


def _unpack_matrix():
    lane = jnp.arange(MXU_DIM)
    n, r = lane // LANES, lane % LANES
    natural = n * LANES + (r % 2) * (LANES // 2) + r // 2
    return (natural[:, None] == jnp.arange(MXU_DIM)[None, :]).astype(BF16)


def _expert_items(counts, n_asg):
    E = N_EXPERTS
    item_rows = SUB_ROWS * ITEM_SUBS
    n_items_max = n_asg // item_rows + E
    n_rows = n_asg + E * SUB_ROWS
    pcounts = (counts + SUB_ROWS - 1) // SUB_ROWS * SUB_ROWS
    pend = jnp.cumsum(pcounts)
    pstart = pend - pcounts
    n_it = (pcounts + item_rows - 1) // item_rows
    it_end = jnp.cumsum(n_it)
    it_start = it_end - n_it
    n_items = it_end[-1]
    wv = jnp.arange(n_items_max, dtype=jnp.int32)
    e_w = jnp.minimum(jnp.sum(wv[:, None] >= it_end[None, :], axis=1), E - 1)
    row0 = pstart[e_w] + (wv - it_start[e_w]) * item_rows
    nsb = jnp.clip((pend[e_w] - row0) // SUB_ROWS, 0, ITEM_SUBS)
    used = wv < n_items
    tail0 = pend[-1] + (wv - n_items) * item_rows
    nz = jnp.clip((n_rows - tail0) // SUB_ROWS, 0, ITEM_SUBS)
    item_row0 = jnp.clip(jnp.where(used, row0, tail0), 0, n_rows - SUB_ROWS)
    item_e = jnp.where(used, e_w, e_w[jnp.maximum(n_items - 1, 0)])
    i32 = lambda a: a.astype(jnp.int32)
    return (i32(item_e), i32(item_row0 // SUBLANES), i32(jnp.where(used, nsb, 0)),
            i32(jnp.where(used, nsb, nz)), pstart, n_rows)


def _experts(item_e, item_g0, item_nc, item_no, row_tok, h2, wgu, wd, bgu, bd, tf):
    n_rows = row_tok.shape[0]
    n_items = item_e.shape[0]
    n_f = EXPERT_FF // tf
    item_rows = SUB_ROWS * ITEM_SUBS
    assert tf % MXU_DIM == 0 and n_f == ITEM_SUBS

    def f_eff(w, f, nc):
        return jnp.where(nc[w] > 0, f, n_f - 1)

    grid_spec = pltpu.PrefetchScalarGridSpec(
        num_scalar_prefetch=5,
        grid=(n_items, n_f),
        in_specs=[
            pl.BlockSpec(memory_space=pl.ANY),
            pl.BlockSpec((1, D_MODEL, 2 * tf), lambda w, f, ie, g0, nc, no, rt: (ie[w], 0, f_eff(w, f, nc))),
            pl.BlockSpec((1, tf, D_MODEL), lambda w, f, ie, g0, nc, no, rt: (ie[w], f_eff(w, f, nc), 0)),
            pl.BlockSpec((1, 1, 2 * tf), lambda w, f, ie, g0, nc, no, rt: (ie[w], 0, f_eff(w, f, nc))),
            pl.BlockSpec((1, 1, D_MODEL), lambda w, f, ie, g0, nc, no, rt: (ie[w], 0, 0)),
            pl.BlockSpec((MXU_DIM, MXU_DIM), lambda w, f, ie, g0, nc, no, rt: (0, 0)),
        ],
        out_specs=pl.BlockSpec(memory_space=pl.ANY),
        scratch_shapes=[
            pltpu.VMEM((item_rows // SUBLANES, SUBLANES, D_MODEL), F32),
            pltpu.VMEM((item_rows, D_MODEL), BF16),
            pltpu.VMEM((item_rows, D_MODEL), F32),
            pltpu.VMEM((D_MODEL, 2 * tf), BF16),
            pltpu.VMEM((tf, D_MODEL), BF16),
            pltpu.SemaphoreType.DMA((1,)),
            pltpu.SemaphoreType.DMA((1,)),
        ],
    )
    return pl.pallas_call(
        _expert_kernel,
        name="experts",
        grid_spec=grid_spec,
        out_shape=jax.ShapeDtypeStruct((n_rows, D_MODEL), F32),
        compiler_params=_cparams(("arbitrary", "arbitrary"), 60),
    )(item_e, item_g0, item_nc, item_no, row_tok, h2, wgu, wd, bgu, bd, _unpack_matrix())


def _combine_kernel(dest_ref, yb_hbm, gate_ref, x1_ref, p_ref, wpg_ref, wpp_ref, g3_ref, gf_ref,
                    o_ref, gbuf, sem):
    i = pl.program_id(0)
    n = pl.num_programs(0)
    tm = x1_ref.shape[0]

    def issue(tile):
        base = tile * tm * TOP_K

        for k in range(TOP_K):
            _gather_rows(yb_hbm, gbuf.at[k], sem.at[0], lambda r, k=k: dest_ref[base + r * TOP_K + k],
                         tm // SUBLANES, 0)

    @pl.when(i == 0)
    def _():
        issue(0)

    for k in range(TOP_K):
        pltpu.make_async_copy(gbuf.at[k], gbuf.at[k], sem.at[0]).wait()
    gates = gate_ref[...]
    moe = gates[:, 0:1] * gbuf[0].reshape(tm, D_MODEL)
    for k in range(1, TOP_K):
        moe = moe + gates[:, k:k + 1] * gbuf[k].reshape(tm, D_MODEL)
    x2 = x1_ref[...] + moe

    @pl.when(i + 1 < n)
    def _():
        issue(i + 1)

    h3 = _rms(x2, g3_ref[...]).astype(BF16)
    pg = _sigmoid(jnp.dot(h3, wpg_ref[...], preferred_element_type=F32))
    pp = jnp.dot(p_ref[...].astype(BF16), wpp_ref[...], preferred_element_type=F32)
    x3 = x2 + pg * pp
    o_ref[...] = _rms(x3, gf_ref[...])


def _combine(dest, yb, gates, x1, p2, wpg, wpp, g3, gf, tm):
    L = x1.shape[0]
    const = lambda *shape: pl.BlockSpec(shape, lambda i, d: (0,) * len(shape), pipeline_mode=pl.Buffered(1))
    row = lambda w: pl.BlockSpec((tm, w), lambda i, d: (i, 0))
    grid_spec = pltpu.PrefetchScalarGridSpec(
        num_scalar_prefetch=1,
        grid=(L // tm,),
        in_specs=[
            pl.BlockSpec(memory_space=pl.ANY),
            row(ROUTER_PAD), row(D_MODEL), row(PLE_DIM),
            const(D_MODEL, D_MODEL), const(PLE_DIM, D_MODEL), const(1, D_MODEL), const(1, D_MODEL),
        ],
        out_specs=row(D_MODEL),
        scratch_shapes=[
            pltpu.VMEM((TOP_K, tm // SUBLANES, SUBLANES, D_MODEL), F32),
            pltpu.SemaphoreType.DMA((1,)),
        ],
    )
    return pl.pallas_call(
        _combine_kernel,
        name="combine_ple",
        grid_spec=grid_spec,
        out_shape=jax.ShapeDtypeStruct((L, D_MODEL), F32),
        compiler_params=_cparams(("arbitrary",), 48),
    )(dest, yb, gates, x1, p2, wpg, wpp, g3, gf)


def _tile(L, pref):
    return min(pref, L)


def _layer(x2, p2, pos, norm1_g, w_in, lam_re, lam_im, log_dt, b_re, b_im, c_re, c_im, d_skip, w_glu,
           b_glu, w_bs, w_br, w_out, norm2_g, router_w, router_b, w_gu, b_gu, w_dn, b_dn, norm3_g,
           w_pg, w_pp, out_g):
    L = x2.shape[0]
    G, N, E = SSM_GROUPS, SSM_STATE, N_EXPERTS

    cuts = SSM_WIDTH + 4 * RET_WIDTH
    w_in_bf = jnp.concatenate([w_in[:, cuts:], w_in[:, :cuts]], axis=1).astype(BF16)
    z = _inproj(x2, norm1_g[None, :], w_in_bf, _tile(L, 1024), 1024)
    ms_blk, mr_blk = 0, 1
    u_blk, q_blk, k_blk, v_blk, g_blk = 4, 5, 6, 7, 8

    ab_re, ab_im, bb_re, bb_im = _s5_prep(lam_re, lam_im, log_dt, b_re, b_im)
    eye = jnp.eye(SUBLANES, dtype=F32)
    gpb = LANES // SSM_GROUP
    def bpack(bb):
        t = bb.reshape(N_UBLK, gpb, SSM_GROUP, N)
        return jnp.einsum('bgcn,gh->bgchn', t, eye).reshape(N_UBLK, LANES, ST_PER_UBLK)
    def cpack(c):
        t = c.reshape(N_UBLK, gpb, SSM_GROUP, N)
        return jnp.einsum('bgcn,gh->bgnhc', t, eye).reshape(N_UBLK, ST_PER_UBLK, LANES)
    bblk = jnp.concatenate([bpack(bb_re), bpack(bb_im)], axis=-1).astype(BF16)
    cblk = jnp.concatenate([cpack(c_re), -cpack(c_im)], axis=1).astype(BF16)
    atile = lambda a: a[::SSM_GROUP].reshape(N_SVREG, SUBLANES, LANES)
    ys = _s5(z, u_blk, bblk, cblk, atile(ab_re), atile(ab_im), d_skip.reshape(1, SSM_WIDTH),
             w_glu.astype(BF16), b_glu[None, :], _tile(L, 512))

    yr = _retention(z, (q_blk, k_blk, v_blk, g_blk), pos, _tile(L, 512))

    rw = jnp.pad(router_w, ((0, 0), (0, ROUTER_PAD - E)))
    rwh = rw.astype(BF16)
    rwl = (rw - rwh.astype(F32)).astype(BF16)
    rb = jnp.pad(router_b, (0, ROUTER_PAD - E), constant_values=NEG_BIG)[None, :]
    x1, h2, idx, gates, rank, cnt = _merge(
        ys, yr, z, ms_blk, mr_blk, x2, w_bs.astype(BF16), w_br.astype(BF16), w_out.astype(BF16),
        norm2_g[None, :], rwh, rwl, rb, _tile(L, 256))

    item_e, item_g0, item_nc, item_no, pstart, n_rows = _expert_items(cnt[0, :E], L * TOP_K)
    idx4, rank4 = idx[:, :TOP_K], rank[:, :TOP_K]
    dest = (pstart[idx4] + rank4).astype(jnp.int32).reshape(-1)
    row_tok = _row_tokens(dest, n_rows)

    yb = _experts(item_e, item_g0, item_nc, item_no, row_tok, h2, w_gu, w_dn, b_gu[:, None, :],
                  b_dn[:, None, :], EXPERT_FF // ITEM_SUBS)

    return _combine(dest, yb, gates, x1, p2, w_pg.astype(BF16), w_pp.astype(BF16),
                    norm3_g[None, :], out_g[None, :], _tile(L, 256))


def kernel(x, p, positions, norm1_g, w_in, ssm_lam_re, ssm_lam_im, ssm_log_dt, ssm_b_re, ssm_b_im, ssm_c_re, ssm_c_im, ssm_d, ssm_w_glu, ssm_b_glu, w_branch_ssm, w_branch_ret, w_out, norm2_g, router_w, router_b, exp_w_gate_up, exp_b_gate_up, exp_w_down, exp_b_down, norm3_g, ple_w_gate, ple_w_proj, final_norm_g):
    bsz, L, d = x.shape
    depth = w_in.shape[0]
    assert bsz == 1 and depth == 1 and d == D_MODEL
    out = _layer(
        x[0], p[0, 0], positions.reshape(L, 1), norm1_g[0], w_in[0], ssm_lam_re[0], ssm_lam_im[0],
        ssm_log_dt[0], ssm_b_re[0], ssm_b_im[0], ssm_c_re[0], ssm_c_im[0], ssm_d[0], ssm_w_glu[0],
        ssm_b_glu[0], w_branch_ssm[0], w_branch_ret[0], w_out[0], norm2_g[0], router_w[0], router_b[0],
        exp_w_gate_up[0], exp_b_gate_up[0], exp_w_down[0], exp_b_down[0], norm3_g[0], ple_w_gate[0],
        ple_w_proj[0], final_norm_g)
    return out[None]
```

```python
import functools
import math

import jax
import jax.numpy as jnp
from jax import lax
from jax.experimental import pallas as pl
from jax.experimental.pallas import tpu as pltpu

F32 = jnp.float32
BF16 = jnp.bfloat16

D_MODEL = 2048
PLE_DIM = 256
SSM_WIDTH = D_MODEL // 2
SSM_GROUP = 16
SSM_GROUPS = SSM_WIDTH // SSM_GROUP
SSM_STATE = 64
RET_HEADS = 8
RET_HEAD_DIM = 128
RET_WIDTH = RET_HEADS * RET_HEAD_DIM
RET_CHUNK = 128
ROPE_BASE = 10000.0
N_EXPERTS = 32
TOP_K = 4
EXPERT_FF = D_MODEL
SWIGLU_LIMIT = 7.0
SWIGLU_ALPHA = 1.702
NORM_EPS = 1e-6
IN_COLS = SSM_WIDTH + 4 * RET_WIDTH + 2 * D_MODEL

LANES = 128
SUBLANES = 8
MXU_DIM = 256
ROUTER_PAD = LANES
NEG_BIG = -1e30

MIB = 1024 * 1024


def _cparams(sem, vmem_mib):
    return pltpu.CompilerParams(dimension_semantics=sem, vmem_limit_bytes=vmem_mib * MIB)


def _rms(xf, g):
    ms = jnp.mean(xf * xf, axis=-1, keepdims=True)
    return xf * lax.rsqrt(ms + NORM_EPS) * g


def _sigmoid(x):
    return 1.0 / (1.0 + jnp.exp(-x))


GATHER_UNROLL = 4


def _gather_rows(src2, dst3, sem, index_of, n_groups, dst_group0, inline=False, first_group=0):
    def body(j, c):
        for u in range(SUBLANES):
            row = index_of(j * SUBLANES + u)
            pltpu.make_async_copy(src2.at[pl.ds(row, 1), :],
                                  dst3.at[dst_group0 + j, pl.ds(u, 1), :], sem).start()
        return c

    if inline:
        for j in range(first_group, first_group + n_groups):
            body(j, 0)
    else:
        lax.fori_loop(first_group, first_group + n_groups, body, 0, unroll=GATHER_UNROLL)


def _inproj_kernel(x_ref, g_ref, w_ref, o_ref, h_scr):
    @pl.when(pl.program_id(1) == 0)
    def _():
        h_scr[...] = _rms(x_ref[...], g_ref[...]).astype(BF16)

    o_ref[...] = jnp.dot(h_scr[...], w_ref[...], preferred_element_type=F32).astype(o_ref.dtype)


def _inproj(x2, g1, w_in_bf, tm, tn):
    L = x2.shape[0]
    n_cols = w_in_bf.shape[1]
    return pl.pallas_call(
        _inproj_kernel,
        name="inproj",
        grid=(L // tm, n_cols // tn),
        in_specs=[
            pl.BlockSpec((tm, D_MODEL), lambda i, j: (i, 0)),
            pl.BlockSpec((1, D_MODEL), lambda i, j: (0, 0)),
            pl.BlockSpec((D_MODEL, tn), lambda i, j: (0, j)),
        ],
        out_specs=pl.BlockSpec((tm, tn), lambda i, j: (i, j)),
        out_shape=jax.ShapeDtypeStruct((L, n_cols), BF16),
        scratch_shapes=[pltpu.VMEM((tm, D_MODEL), BF16)],
        compiler_params=_cparams(("arbitrary", "arbitrary"), 48),
    )(x2, g1, w_in_bf)


def _s5_prep_kernel(lr_ref, li_ref, ldt_ref, bre_ref, bim_ref, abre_ref, abim_ref, bbre_ref, bbim_ref):
    lr, li = lr_ref[...], li_ref[...]
    dt = jnp.exp(ldt_ref[...])
    mag = jnp.exp(lr * dt)
    ab_re, ab_im = mag * jnp.cos(li * dt), mag * jnp.sin(li * dt)
    den = lr * lr + li * li
    nr, ni = ab_re - 1.0, ab_im
    f_re = (nr * lr + ni * li) / den
    f_im = (ni * lr - nr * li) / den
    bre, bim = bre_ref[...], bim_ref[...]
    abre_ref[...] = ab_re
    abim_ref[...] = ab_im
    bbre_ref[...] = f_re * bre - f_im * bim
    bbim_ref[...] = f_re * bim + f_im * bre


def _s5_prep(lam_re, lam_im, log_dt, b_re, b_im):
    rep = lambda a: jnp.repeat(a, SSM_GROUP, axis=0)
    bt = lambda b: jnp.transpose(b, (0, 2, 1)).reshape(SSM_WIDTH, SSM_STATE)
    shp = jax.ShapeDtypeStruct((SSM_WIDTH, SSM_STATE), F32)
    return pl.pallas_call(_s5_prep_kernel, name="s5_prep", out_shape=(shp, shp, shp, shp))(
        rep(lam_re), rep(lam_im), rep(log_dt[:, None]), bt(b_re), bt(b_im))


def _gelu_tanh(x):
    return 0.5 * x * (1.0 + jnp.tanh(math.sqrt(2.0 / math.pi) * (x + 0.044715 * (x * x * x))))


N_UBLK = SSM_WIDTH // LANES
ST_PER_UBLK = (LANES // SSM_GROUP) * SSM_STATE
TILES_PER_UBLK = ST_PER_UBLK // LANES
N_SVREG = SSM_GROUPS * SSM_STATE // (LANES * SUBLANES)


def _s5_kernel(u_ref, bblk_ref, cblk_ref, are_ref, aim_ref, d_ref, wglu_ref, bglu_ref, o_ref,
               sre, sim, carry):
    tm = u_ref.shape[0]

    @pl.when(pl.program_id(0) == 0)
    def _():
        carry[...] = jnp.zeros_like(carry)

    u = u_ref[...]
    for b in range(N_UBLK):
        bu = jnp.dot(u[:, b * LANES:(b + 1) * LANES], bblk_ref[b], preferred_element_type=F32)
        v = (b * TILES_PER_UBLK) // SUBLANES
        for q in range(TILES_PER_UBLK):
            k = (b * TILES_PER_UBLK + q) % SUBLANES
            sre[v, pl.ds(k, tm, stride=SUBLANES), :] = bu[:, q * LANES:(q + 1) * LANES]
            sim[v, pl.ds(k, tm, stride=SUBLANES), :] = bu[:, ST_PER_UBLK + q * LANES:ST_PER_UBLK + (q + 1) * LANES]

    are, aim = are_ref[...], aim_ref[...]

    def step(t, c):
        cre, cim = c
        r0 = pl.multiple_of(t * SUBLANES, SUBLANES)
        bre = sre[:, pl.ds(r0, SUBLANES), :]
        bim = sim[:, pl.ds(r0, SUBLANES), :]
        nre = are * cre - aim * cim + bre
        nim = are * cim + aim * cre + bim
        sre[:, pl.ds(r0, SUBLANES), :] = nre
        sim[:, pl.ds(r0, SUBLANES), :] = nim
        return nre, nim

    cre, cim = lax.fori_loop(0, tm, step, (carry[0], carry[1]), unroll=4)
    carry[0] = cre
    carry[1] = cim

    ys = []
    for b in range(N_UBLK):
        v = (b * TILES_PER_UBLK) // SUBLANES
        k0 = (b * TILES_PER_UBLK) % SUBLANES
        parts = [sre[v, pl.ds(k0 + q, tm, stride=SUBLANES), :] for q in range(TILES_PER_UBLK)]
        parts += [sim[v, pl.ds(k0 + q, tm, stride=SUBLANES), :] for q in range(TILES_PER_UBLK)]
        sb = jnp.concatenate(parts, axis=-1).astype(BF16)
        ys.append(jnp.dot(sb, cblk_ref[b], preferred_element_type=F32))
    y = jnp.concatenate(ys, axis=-1) + d_ref[...] * u.astype(F32)
    y = _gelu_tanh(y)
    gl = jnp.dot(y.astype(BF16), wglu_ref[...], preferred_element_type=F32) + bglu_ref[...]
    o_ref[...] = (y * _sigmoid(gl)).astype(o_ref.dtype)


def _s5(z, u_blk, bblk, cblk, are, aim, d_skip, wglu, bglu, tm):
    L = z.shape[0]
    const = lambda *shape: pl.BlockSpec(shape, lambda i: (0,) * len(shape))
    return pl.pallas_call(
        _s5_kernel,
        name="s5_branch",
        grid=(L // tm,),
        in_specs=[
            pl.BlockSpec((tm, SSM_WIDTH), lambda i: (i, u_blk)),
            const(N_UBLK, LANES, 2 * ST_PER_UBLK),
            const(N_UBLK, 2 * ST_PER_UBLK, LANES),
            const(N_SVREG, SUBLANES, LANES),
            const(N_SVREG, SUBLANES, LANES),
            const(1, SSM_WIDTH),
            const(SSM_WIDTH, SSM_WIDTH),
            const(1, SSM_WIDTH),
        ],
        out_specs=pl.BlockSpec((tm, SSM_WIDTH), lambda i: (i, 0)),
        out_shape=jax.ShapeDtypeStruct((L, SSM_WIDTH), BF16),
        scratch_shapes=[
            pltpu.VMEM((N_SVREG, tm * SUBLANES, LANES), F32),
            pltpu.VMEM((N_SVREG, tm * SUBLANES, LANES), F32),
            pltpu.VMEM((2, N_SVREG, SUBLANES, LANES), F32),
        ],
        compiler_params=_cparams(("arbitrary",), 48),
    )(z, bblk, cblk, are, aim, d_skip, wglu, bglu)


def _ret_kernel(q_ref, k_ref, v_ref, g_ref, pos_ref, inv_ref, decay_ref, zeta_ref, xi_ref, cd_ref,
                o_ref, r_scr):
    tm = q_ref.shape[0]
    C, dk = RET_CHUNK, RET_HEAD_DIM

    @pl.when(pl.program_id(0) == 0)
    def _():
        r_scr[...] = jnp.zeros_like(r_scr)

    ang = pos_ref[...].astype(F32) * inv_ref[...]
    cs, sn = jnp.cos(ang), jnp.sin(ang)
    cf = jnp.concatenate([cs, cs], axis=-1)
    sf = jnp.concatenate([-sn, sn], axis=-1)

    def rot(xh):
        return xh * cf + pltpu.roll(xh, dk // 2, axis=1) * sf

    nt = (((1,), (1,)), ((), ()))
    for h in range(RET_HEADS):
        cols = slice(h * dk, (h + 1) * dk)
        qr = rot(q_ref[:, cols].astype(F32))
        kr = rot(k_ref[:, cols].astype(F32))
        for n in range(tm // C):
            rows = slice(n * C, (n + 1) * C)
            qc = qr[rows].astype(BF16)
            kcf = kr[rows]
            vc = v_ref[rows, cols]
            sc = lax.dot_general(qc, kcf.astype(BF16), nt, preferred_element_type=F32) * decay_ref[h]
            inner = jnp.dot(sc.astype(BF16), vc, preferred_element_type=F32)
            r_prev = r_scr[h]
            cross = jnp.dot(qc, r_prev.astype(BF16), preferred_element_type=F32) * xi_ref[h]
            o = inner + cross
            kzt = jnp.transpose(kcf * zeta_ref[h]).astype(BF16)
            r_scr[h] = r_prev * cd_ref[h] + jnp.dot(kzt, vc, preferred_element_type=F32)
            mu = jnp.mean(o, axis=-1, keepdims=True)
            oc = o - mu
            on = oc * lax.rsqrt(jnp.mean(oc * oc, axis=-1, keepdims=True) + NORM_EPS)
            gate = g_ref[rows, cols].astype(F32)
            o_ref[rows, cols] = (gate * _sigmoid(gate) * on).astype(o_ref.dtype)


def _retention(z, blks, pos, tm):
    L = z.shape[0]
    H, C, dk = RET_HEADS, RET_CHUNK, RET_HEAD_DIM
    half = dk // 2
    inv = (ROPE_BASE ** (-jnp.arange(half, dtype=F32) / half))[None, :]
    log_gamma = jnp.log(1.0 - jnp.exp2(-5.0 - jnp.arange(H, dtype=F32)))
    idx = jnp.arange(C, dtype=F32)
    rel = idx[:, None] - idx[None, :]
    scale = dk ** -0.5
    decay = jnp.where(rel >= 0, jnp.exp(jnp.maximum(rel, 0.0)[None] * log_gamma[:, None, None]), 0.0) * scale
    zeta = jnp.exp((C - 1.0 - idx)[None, :] * log_gamma[:, None]) * scale
    xi = jnp.exp((idx + 1.0)[None, :] * log_gamma[:, None])
    bc = lambda a: jnp.broadcast_to(a[:, :, None], (H, C, dk))
    cd = jnp.broadcast_to(jnp.exp(C * log_gamma)[:, None, None], (H, 1, dk))
    const = lambda *shape: pl.BlockSpec(shape, lambda i: (0,) * len(shape))
    qb, kb, vb, gb = blks
    return pl.pallas_call(
        _ret_kernel,
        name="retention",
        grid=(L // tm,),
        in_specs=[
            pl.BlockSpec((tm, RET_WIDTH), lambda i: (i, qb)),
            pl.BlockSpec((tm, RET_WIDTH), lambda i: (i, kb)),
            pl.BlockSpec((tm, RET_WIDTH), lambda i: (i, vb)),
            pl.BlockSpec((tm, RET_WIDTH), lambda i: (i, gb)),
            pl.BlockSpec((tm, 1), lambda i: (i, 0)),
            const(1, half),
            const(H, C, C),
            const(H, C, dk),
            const(H, C, dk),
            const(H, 1, dk),
        ],
        out_specs=pl.BlockSpec((tm, RET_WIDTH), lambda i: (i, 0)),
        out_shape=jax.ShapeDtypeStruct((L, RET_WIDTH), BF16),
        scratch_shapes=[pltpu.VMEM((H, dk, dk), F32)],
        compiler_params=_cparams(("arbitrary",), 48),
    )(z, z, z, z, pos, inv, decay, bc(zeta), bc(xi), cd)


def _merge_kernel(ys_ref, yr_ref, ms_ref, mr_ref, x_ref, ps_ref, pr_ref, wo_ref, g2_ref,
                  rwh_ref, rwl_ref, rb_ref, tril_ref,
                  x1_ref, h2_ref, idx_ref, gate_ref, rank_ref, cnt_ref, carry):
    tm = x_ref.shape[0]

    @pl.when(pl.program_id(0) == 0)
    def _():
        carry[...] = jnp.zeros_like(carry)

    a = jnp.dot(ys_ref[...], ps_ref[...], preferred_element_type=F32)
    b = jnp.dot(yr_ref[...], pr_ref[...], preferred_element_type=F32)
    merged = _sigmoid(ms_ref[...].astype(F32)) * a + _sigmoid(mr_ref[...].astype(F32)) * b
    x1 = x_ref[...] + jnp.dot(merged.astype(BF16), wo_ref[...], preferred_element_type=F32)
    x1_ref[...] = x1
    h2 = _rms(x1, g2_ref[...])
    h2_ref[...] = h2

    hh = h2.astype(BF16)
    hl = (h2 - hh.astype(F32)).astype(BF16)
    logits = (jnp.dot(hh, rwh_ref[...], preferred_element_type=F32)
              + jnp.dot(hh, rwl_ref[...], preferred_element_type=F32)
              + jnp.dot(hl, rwh_ref[...], preferred_element_type=F32)) + rb_ref[...]

    lane = lax.broadcasted_iota(jnp.int32, (tm, ROUTER_PAD), 1)
    work = logits
    vals, idxs = [], []
    for _ in range(TOP_K):
        m = jnp.max(work, axis=-1, keepdims=True)
        ix = jnp.min(jnp.where(work == m, lane, ROUTER_PAD), axis=-1, keepdims=True)
        vals.append(m)
        idxs.append(ix)
        work = jnp.where(lane == ix, -jnp.inf, work)
    es = [jnp.exp(v - vals[0]) for v in vals]
    den = es[0] + es[1] + es[2] + es[3]

    onehot = jnp.zeros((tm, ROUTER_PAD), F32)
    for ix in idxs:
        onehot = onehot + (lane == ix).astype(F32)
    cum = jnp.dot(tril_ref[...], onehot.astype(BF16), preferred_element_type=F32) + carry[...]
    carry[...] = carry[...] + jnp.sum(onehot, axis=0, keepdims=True)
    cnt_ref[...] = jnp.broadcast_to(carry[...], cnt_ref.shape).astype(jnp.int32)

    idx_o = jnp.zeros((tm, ROUTER_PAD), jnp.int32)
    gate_o = jnp.zeros((tm, ROUTER_PAD), F32)
    rank_o = jnp.zeros((tm, ROUTER_PAD), jnp.int32)
    for k in range(TOP_K):
        rk = jnp.sum(jnp.where(lane == idxs[k], cum, 0.0), axis=-1, keepdims=True).astype(jnp.int32)
        idx_o = jnp.where(lane == k, idxs[k], idx_o)
        gate_o = jnp.where(lane == k, es[k] / den, gate_o)
        rank_o = jnp.where(lane == k, rk, rank_o)
    idx_ref[...] = idx_o
    gate_ref[...] = gate_o
    rank_ref[...] = rank_o


def _merge(ys, yr, z, ms_blk, mr_blk, x2, ps, pr, wo, g2, rwh, rwl, rb, tm):
    L = x2.shape[0]
    tril = (jnp.arange(tm)[:, None] > jnp.arange(tm)[None, :]).astype(BF16)
    const = lambda *shape: pl.BlockSpec(shape, lambda i: (0,) * len(shape), pipeline_mode=pl.Buffered(1))
    row = lambda w: pl.BlockSpec((tm, w), lambda i: (i, 0))
    return pl.pallas_call(
        _merge_kernel,
        name="merge_router",
        grid=(L // tm,),
        in_specs=[
            row(SSM_WIDTH), row(RET_WIDTH),
            pl.BlockSpec((tm, D_MODEL), lambda i: (i, ms_blk)),
            pl.BlockSpec((tm, D_MODEL), lambda i: (i, mr_blk)),
            row(D_MODEL),
            const(SSM_WIDTH, D_MODEL), const(RET_WIDTH, D_MODEL), const(D_MODEL, D_MODEL),
            const(1, D_MODEL),
            const(D_MODEL, ROUTER_PAD), const(D_MODEL, ROUTER_PAD), const(1, ROUTER_PAD),
            const(tm, tm),
        ],
        out_specs=[
            row(D_MODEL), row(D_MODEL),
            row(ROUTER_PAD), row(ROUTER_PAD), row(ROUTER_PAD),
            pl.BlockSpec((SUBLANES, ROUTER_PAD), lambda i: (0, 0)),
        ],
        out_shape=[
            jax.ShapeDtypeStruct((L, D_MODEL), F32),
            jax.ShapeDtypeStruct((L, D_MODEL), F32),
            jax.ShapeDtypeStruct((L, ROUTER_PAD), jnp.int32),
            jax.ShapeDtypeStruct((L, ROUTER_PAD), F32),
            jax.ShapeDtypeStruct((L, ROUTER_PAD), jnp.int32),
            jax.ShapeDtypeStruct((SUBLANES, ROUTER_PAD), jnp.int32),
        ],
        scratch_shapes=[pltpu.VMEM((1, ROUTER_PAD), F32)],
        compiler_params=_cparams(("arbitrary",), 56),
    )(ys, yr, z, z, x2, ps, pr, wo, g2, rwh, rwl, rb, tril)


def _row_tokens_kernel(dest_ref, out_ref):
    def clear(i, c):
        out_ref[i] = 0
        return c

    def place(t, c):
        for k in range(TOP_K):
            out_ref[dest_ref[t * TOP_K + k]] = t
        return c

    lax.fori_loop(0, out_ref.shape[0], clear, 0, unroll=8)
    lax.fori_loop(0, dest_ref.shape[0] // TOP_K, place, 0, unroll=4)


def _row_tokens(dest, n_rows):
    smem = pl.BlockSpec(memory_space=pltpu.SMEM)
    return pl.pallas_call(
        _row_tokens_kernel, name="row_tokens", in_specs=[smem], out_specs=smem,
        out_shape=jax.ShapeDtypeStruct((n_rows,), jnp.int32))(dest)


SUB_ROWS = 128
ITEM_SUBS = 9
BLOCK_SUBS = 8
N_F = 4
SUB_GROUPS = SUB_ROWS // SUBLANES
ACC_COLS = 512


def _expert_kernel(item_e_ref, item_g0_ref, item_nc_ref, item_no_ref, rowtok_ref,
                   h2_hbm, wgu_ref, wd_ref, bgu_ref, bd_ref, unpack_ref, yb_hbm,
                   xbuf, xb, acc, wgu_bf, wd_bf, gsem, osem):
    w = pl.program_id(0)
    f = pl.program_id(1)
    n_w = pl.num_programs(0)
    n_f = pl.num_programs(1)
    tf = wd_ref.shape[1]
    nc = item_nc_ref[w]
    no = item_no_ref[w]

    def groups(s):
        return pl.ds(s * SUB_GROUPS, SUB_GROUPS)

    def gather_sub(item, s):
        base = item_g0_ref[item] * SUBLANES + s * SUB_ROWS
        _gather_rows(h2_hbm, xbuf, gsem.at[0], lambda r: rowtok_ref[base + r], SUB_GROUPS, s * SUB_GROUPS)

    def gather_wait(s):
        pltpu.make_async_copy(xbuf.at[groups(s)], xbuf.at[groups(s)], gsem.at[0]).wait()

    def out_copy(s, g0):
        row0 = pl.multiple_of((g0 + s * SUB_GROUPS) * SUBLANES, SUBLANES)
        return pltpu.make_async_copy(acc.at[pl.ds(s * SUB_ROWS, SUB_ROWS), :],
                                     yb_hbm.at[pl.ds(row0, SUB_ROWS), :], osem.at[0])

    def for_subs(count, fn):
        for s in range(ITEM_SUBS):
            pl.when(s < count)(functools.partial(fn, s))

    @pl.when(f == 0)
    def _():
        @pl.when(w == 0)
        def _():
            for_subs(nc, lambda s: gather_sub(0, s))

        for_subs(nc, gather_wait)

        def stage(s):
            xb[s * SUB_ROWS:(s + 1) * SUB_ROWS, :] = xbuf[groups(s)].reshape(SUB_ROWS, D_MODEL).astype(BF16)

        for_subs(nc, stage)

        @pl.when(w > 0)
        def _():
            for_subs(item_no_ref[jnp.maximum(w - 1, 0)], lambda s: out_copy(s, 0).wait())

        init = bd_ref[0] * (nc > 0).astype(F32)

        def init_acc(s):
            acc[s * SUB_ROWS:(s + 1) * SUB_ROWS, :] = jnp.broadcast_to(init, (SUB_ROWS, D_MODEL))

        for_subs(no, init_acc)

    nxt_item = jnp.minimum(w + 1, n_w - 1)
    nxt_nc = jnp.where(w + 1 < n_w, item_nc_ref[nxt_item], 0)
    last = f == n_f - 1

    per_step = BLOCK_SUBS // N_F
    for i in range(per_step):
        s_dyn = f * per_step + i
        pl.when(s_dyn < nxt_nc)(functools.partial(gather_sub, nxt_item, s_dyn))

    for s in range(BLOCK_SUBS, ITEM_SUBS):
        pl.when(last & (s < nxt_nc))(functools.partial(gather_sub, nxt_item, s))

    def act(guc):
        g = jnp.minimum(guc, SWIGLU_LIMIT)
        u = jnp.clip(pltpu.roll(guc, LANES - 1, axis=1), -SWIGLU_LIMIT, SWIGLU_LIMIT)
        return g * _sigmoid(SWIGLU_ALPHA * g) * (u + 1.0)

    def rows_block(r0, m, cast):
        x = xb[r0:r0 + m, :]
        n_k = tf // MXU_DIM
        even = (lax.broadcasted_iota(jnp.int32, (m, LANES), 1) & 1) == 0

        def gate_up(k):
            outs = []
            for n in range(2):
                cols = slice((2 * k + n) * MXU_DIM, (2 * k + n + 1) * MXU_DIM)
                if cast:
                    wt = wgu_ref[0, :, cols].astype(BF16)
                    wgu_bf[:, cols] = wt
                else:
                    wt = wgu_bf[:, cols]
                outs.append(jnp.dot(x, wt, preferred_element_type=F32) + bgu_ref[0, :, cols])
            return outs

        def activate(gus):
            parts = [jnp.where(even, act(gu[:, :LANES]), pltpu.roll(act(gu[:, LANES:]), 1, axis=1))
                     for gu in gus]
            packed = jnp.concatenate(parts, axis=-1).astype(BF16)
            return jnp.dot(packed, unpack_ref[...], preferred_element_type=F32).astype(BF16)

        gus = gate_up(0)
        for k in range(n_k):
            nxt = gate_up(k + 1) if k + 1 < n_k else None
            a = activate(gus)
            rows = slice(k * MXU_DIM, (k + 1) * MXU_DIM)
            if cast:
                wdt = wd_ref[0, rows, :].astype(BF16)
                wd_bf[rows, :] = wdt
            else:
                wdt = wd_bf[rows, :]
            for c in range(D_MODEL // ACC_COLS):
                cs = slice(c * ACC_COLS, (c + 1) * ACC_COLS)
                acc[r0:r0 + m, cs] += jnp.dot(a, wdt[:, cs], preferred_element_type=F32)
            gus = nxt

    g0 = item_g0_ref[w]
    block_rows = BLOCK_SUBS * SUB_ROWS
    big = nc >= BLOCK_SUBS

    def extra_sub(s, start_out):
        rows_block(s * SUB_ROWS, SUB_ROWS, False)
        if start_out:
            out_copy(s, g0).start()

    @pl.when(big & jnp.logical_not(last))
    def _():
        rows_block(0, block_rows, True)
        for s in range(BLOCK_SUBS, ITEM_SUBS):
            pl.when(s < nc)(functools.partial(extra_sub, s, False))

    @pl.when(big & last)
    def _():
        half = BLOCK_SUBS // 2
        rows_block(0, block_rows // 2, True)
        for s in range(half):
            out_copy(s, g0).start()
        rows_block(block_rows // 2, block_rows // 2, False)
        for s in range(half, BLOCK_SUBS):
            out_copy(s, g0).start()
        for s in range(BLOCK_SUBS, ITEM_SUBS):
            pl.when(s < nc)(functools.partial(extra_sub, s, True))

    @pl.when((nc > 0) & jnp.logical_not(big))
    def _():
        def partial_sub(s):
            rows_block(s * SUB_ROWS, SUB_ROWS, s == 0)
            pl.when(last)(lambda: out_copy(s, g0).start())

        for s in range(BLOCK_SUBS - 1):
            pl.when(s < nc)(functools.partial(partial_sub, s))

    @pl.when((nc == 0) & last)
    def _():
        for_subs(no, lambda s: out_copy(s, g0).start())

    @pl.when(last & (w == n_w - 1))
    def _():
        for_subs(no, lambda s: out_copy(s, 0).wait())


def _unpack_matrix():
    lane = jnp.arange(MXU_DIM)
    n, r = lane // LANES, lane % LANES
    natural = n * LANES + (r % 2) * (LANES // 2) + r // 2
    return (natural[:, None] == jnp.arange(MXU_DIM)[None, :]).astype(BF16)


def _expert_items(counts, n_asg):
    E = N_EXPERTS
    item_rows = SUB_ROWS * ITEM_SUBS
    n_items_max = (n_asg // SUB_ROWS + E) // ITEM_SUBS + E + 1
    n_rows = n_asg + E * SUB_ROWS
    pcounts = (counts + SUB_ROWS - 1) // SUB_ROWS * SUB_ROWS
    pend = jnp.cumsum(pcounts)
    pstart = pend - pcounts
    nsb_e = pcounts // SUB_ROWS
    n_it = (nsb_e + ITEM_SUBS - 1) // ITEM_SUBS
    base_e = nsb_e // jnp.maximum(n_it, 1)
    extra_e = nsb_e - base_e * n_it
    it_end = jnp.cumsum(n_it)
    it_start = it_end - n_it
    n_items = it_end[-1]
    wv = jnp.arange(n_items_max, dtype=jnp.int32)
    e_w = jnp.minimum(jnp.sum(wv[:, None] >= it_end[None, :], axis=1), E - 1)
    j_w = wv - it_start[e_w]
    sb0 = j_w * base_e[e_w] + jnp.minimum(j_w, extra_e[e_w])
    row0 = pstart[e_w] + sb0 * SUB_ROWS
    nsb = base_e[e_w] + (j_w < extra_e[e_w])
    used = wv < n_items
    tail0 = pend[-1] + (wv - n_items) * item_rows
    nz = jnp.clip((n_rows - tail0) // SUB_ROWS, 0, ITEM_SUBS)
    item_row0 = jnp.clip(jnp.where(used, row0, tail0), 0, n_rows - SUB_ROWS)
    item_e = jnp.where(used, e_w, e_w[jnp.maximum(n_items - 1, 0)])
    i32 = lambda a: a.astype(jnp.int32)
    return (i32(item_e), i32(item_row0 // SUBLANES), i32(jnp.where(used, nsb, 0)),
            i32(jnp.where(used, nsb, nz)), pstart, n_rows)


def _experts(item_e, item_g0, item_nc, item_no, row_tok, h2, wgu, wd, bgu, bd, tf):
    n_rows = row_tok.shape[0]
    n_items = item_e.shape[0]
    n_f = EXPERT_FF // tf
    item_rows = SUB_ROWS * ITEM_SUBS
    assert tf % MXU_DIM == 0 and n_f == N_F and N_F <= BLOCK_SUBS <= ITEM_SUBS

    def f_eff(w, f, nc):
        return jnp.where(nc[w] > 0, f, n_f - 1)

    grid_spec = pltpu.PrefetchScalarGridSpec(
        num_scalar_prefetch=5,
        grid=(n_items, n_f),
        in_specs=[
            pl.BlockSpec(memory_space=pl.ANY),
            pl.BlockSpec((1, D_MODEL, 2 * tf), lambda w, f, ie, g0, nc, no, rt: (ie[w], 0, f_eff(w, f, nc))),
            pl.BlockSpec((1, tf, D_MODEL), lambda w, f, ie, g0, nc, no, rt: (ie[w], f_eff(w, f, nc), 0)),
            pl.BlockSpec((1, 1, 2 * tf), lambda w, f, ie, g0, nc, no, rt: (ie[w], 0, f_eff(w, f, nc))),
            pl.BlockSpec((1, 1, D_MODEL), lambda w, f, ie, g0, nc, no, rt: (ie[w], 0, 0)),
            pl.BlockSpec((MXU_DIM, MXU_DIM), lambda w, f, ie, g0, nc, no, rt: (0, 0)),
        ],
        out_specs=pl.BlockSpec(memory_space=pl.ANY),
        scratch_shapes=[
            pltpu.VMEM((item_rows // SUBLANES, SUBLANES, D_MODEL), F32),
            pltpu.VMEM((item_rows, D_MODEL), BF16),
            pltpu.VMEM((item_rows, D_MODEL), F32),
            pltpu.VMEM((D_MODEL, 2 * tf), BF16),
            pltpu.VMEM((tf, D_MODEL), BF16),
            pltpu.SemaphoreType.DMA((1,)),
            pltpu.SemaphoreType.DMA((1,)),
        ],
    )
    return pl.pallas_call(
        _expert_kernel,
        name="experts",
        grid_spec=grid_spec,
        out_shape=jax.ShapeDtypeStruct((n_rows, D_MODEL), F32),
        compiler_params=_cparams(("arbitrary", "arbitrary"), 60),
    )(item_e, item_g0, item_nc, item_no, row_tok, h2, wgu, wd, bgu, bd, _unpack_matrix())


def _combine_kernel(dest_ref, yb_hbm, gate_ref, x1_ref, p_ref, wpg_ref, wpp_ref, g3_ref, gf_ref,
                    o_ref, gbuf, sem):
    i = pl.program_id(0)
    n = pl.num_programs(0)
    tm = x1_ref.shape[0]

    def issue(tile):
        base = tile * tm * TOP_K

        for k in range(TOP_K):
            _gather_rows(yb_hbm, gbuf.at[k], sem.at[0], lambda r, k=k: dest_ref[base + r * TOP_K + k],
                         tm // SUBLANES, 0)

    @pl.when(i == 0)
    def _():
        issue(0)

    for k in range(TOP_K):
        pltpu.make_async_copy(gbuf.at[k], gbuf.at[k], sem.at[0]).wait()
    gates = gate_ref[...]
    moe = gates[:, 0:1] * gbuf[0].reshape(tm, D_MODEL)
    for k in range(1, TOP_K):
        moe = moe + gates[:, k:k + 1] * gbuf[k].reshape(tm, D_MODEL)
    x2 = x1_ref[...] + moe

    @pl.when(i + 1 < n)
    def _():
        issue(i + 1)

    h3 = _rms(x2, g3_ref[...]).astype(BF16)
    pg = _sigmoid(jnp.dot(h3, wpg_ref[...], preferred_element_type=F32))
    pp = jnp.dot(p_ref[...].astype(BF16), wpp_ref[...], preferred_element_type=F32)
    x3 = x2 + pg * pp
    o_ref[...] = _rms(x3, gf_ref[...])


def _combine(dest, yb, gates, x1, p2, wpg, wpp, g3, gf, tm):
    L = x1.shape[0]
    const = lambda *shape: pl.BlockSpec(shape, lambda i, d: (0,) * len(shape), pipeline_mode=pl.Buffered(1))
    row = lambda w: pl.BlockSpec((tm, w), lambda i, d: (i, 0))
    grid_spec = pltpu.PrefetchScalarGridSpec(
        num_scalar_prefetch=1,
        grid=(L // tm,),
        in_specs=[
            pl.BlockSpec(memory_space=pl.ANY),
            row(ROUTER_PAD), row(D_MODEL), row(PLE_DIM),
            const(D_MODEL, D_MODEL), const(PLE_DIM, D_MODEL), const(1, D_MODEL), const(1, D_MODEL),
        ],
        out_specs=row(D_MODEL),
        scratch_shapes=[
            pltpu.VMEM((TOP_K, tm // SUBLANES, SUBLANES, D_MODEL), F32),
            pltpu.SemaphoreType.DMA((1,)),
        ],
    )
    return pl.pallas_call(
        _combine_kernel,
        name="combine_ple",
        grid_spec=grid_spec,
        out_shape=jax.ShapeDtypeStruct((L, D_MODEL), F32),
        compiler_params=_cparams(("arbitrary",), 48),
    )(dest, yb, gates, x1, p2, wpg, wpp, g3, gf)


def _tile(L, pref):
    return min(pref, L)


def _layer(x2, p2, pos, norm1_g, w_in, lam_re, lam_im, log_dt, b_re, b_im, c_re, c_im, d_skip, w_glu,
           b_glu, w_bs, w_br, w_out, norm2_g, router_w, router_b, w_gu, b_gu, w_dn, b_dn, norm3_g,
           w_pg, w_pp, out_g):
    L = x2.shape[0]
    G, N, E = SSM_GROUPS, SSM_STATE, N_EXPERTS

    cuts = SSM_WIDTH + 4 * RET_WIDTH
    w_in_bf = jnp.concatenate([w_in[:, cuts:], w_in[:, :cuts]], axis=1).astype(BF16)
    z = _inproj(x2, norm1_g[None, :], w_in_bf, _tile(L, 1024), 1024)
    ms_blk, mr_blk = 0, 1
    u_blk, q_blk, k_blk, v_blk, g_blk = 4, 5, 6, 7, 8

    ab_re, ab_im, bb_re, bb_im = _s5_prep(lam_re, lam_im, log_dt, b_re, b_im)
    eye = jnp.eye(SUBLANES, dtype=F32)
    gpb = LANES // SSM_GROUP
    def bpack(bb):
        t = bb.reshape(N_UBLK, gpb, SSM_GROUP, N)
        return jnp.einsum('bgcn,gh->bgchn', t, eye).reshape(N_UBLK, LANES, ST_PER_UBLK)
    def cpack(c):
        t = c.reshape(N_UBLK, gpb, SSM_GROUP, N)
        return jnp.einsum('bgcn,gh->bgnhc', t, eye).reshape(N_UBLK, ST_PER_UBLK, LANES)
    bblk = jnp.concatenate([bpack(bb_re), bpack(bb_im)], axis=-1).astype(BF16)
    cblk = jnp.concatenate([cpack(c_re), -cpack(c_im)], axis=1).astype(BF16)
    atile = lambda a: a[::SSM_GROUP].reshape(N_SVREG, SUBLANES, LANES)
    ys = _s5(z, u_blk, bblk, cblk, atile(ab_re), atile(ab_im), d_skip.reshape(1, SSM_WIDTH),
             w_glu.astype(BF16), b_glu[None, :], _tile(L, 512))

    yr = _retention(z, (q_blk, k_blk, v_blk, g_blk), pos, _tile(L, 512))

    rw = jnp.pad(router_w, ((0, 0), (0, ROUTER_PAD - E)))
    rwh = rw.astype(BF16)
    rwl = (rw - rwh.astype(F32)).astype(BF16)
    rb = jnp.pad(router_b, (0, ROUTER_PAD - E), constant_values=NEG_BIG)[None, :]
    x1, h2, idx, gates, rank, cnt = _merge(
        ys, yr, z, ms_blk, mr_blk, x2, w_bs.astype(BF16), w_br.astype(BF16), w_out.astype(BF16),
        norm2_g[None, :], rwh, rwl, rb, _tile(L, 256))

    item_e, item_g0, item_nc, item_no, pstart, n_rows = _expert_items(cnt[0, :E], L * TOP_K)
    idx4, rank4 = idx[:, :TOP_K], rank[:, :TOP_K]
    dest = (pstart[idx4] + rank4).astype(jnp.int32).reshape(-1)
    row_tok = _row_tokens(dest, n_rows)

    yb = _experts(item_e, item_g0, item_nc, item_no, row_tok, h2, w_gu, w_dn, b_gu[:, None, :],
                  b_dn[:, None, :], EXPERT_FF // N_F)

    return _combine(dest, yb, gates, x1, p2, w_pg.astype(BF16), w_pp.astype(BF16),
                    norm3_g[None, :], out_g[None, :], _tile(L, 256))


def kernel(x, p, positions, norm1_g, w_in, ssm_lam_re, ssm_lam_im, ssm_log_dt, ssm_b_re, ssm_b_im, ssm_c_re, ssm_c_im, ssm_d, ssm_w_glu, ssm_b_glu, w_branch_ssm, w_branch_ret, w_out, norm2_g, router_w, router_b, exp_w_gate_up, exp_b_gate_up, exp_w_down, exp_b_down, norm3_g, ple_w_gate, ple_w_proj, final_norm_g):
    bsz, L, d = x.shape
    depth = w_in.shape[0]
    assert bsz == 1 and depth == 1 and d == D_MODEL
    out = _layer(
        x[0], p[0, 0], positions.reshape(L, 1), norm1_g[0], w_in[0], ssm_lam_re[0], ssm_lam_im[0],
        ssm_log_dt[0], ssm_b_re[0], ssm_b_im[0], ssm_c_re[0], ssm_c_im[0], ssm_d[0], ssm_w_glu[0],
        ssm_b_glu[0], w_branch_ssm[0], w_branch_ret[0], w_out[0], norm2_g[0], router_w[0], router_b[0],
        exp_w_gate_up[0], exp_b_gate_up[0], exp_w_down[0], exp_b_down[0], norm3_g[0], ple_w_gate[0],
        ple_w_proj[0], final_norm_g)
    return out[None]
```

```python
import functools
import math

import jax
import jax.numpy as jnp
from jax import lax
from jax.experimental import pallas as pl
from jax.experimental.pallas import tpu as pltpu

F32 = jnp.float32
BF16 = jnp.bfloat16

D_MODEL = 2048
PLE_DIM = 256
SSM_WIDTH = D_MODEL // 2
SSM_GROUP = 16
SSM_GROUPS = SSM_WIDTH // SSM_GROUP
SSM_STATE = 64
RET_HEADS = 8
RET_HEAD_DIM = 128
RET_WIDTH = RET_HEADS * RET_HEAD_DIM
RET_CHUNK = 128
ROPE_BASE = 10000.0
N_EXPERTS = 32
TOP_K = 4
EXPERT_FF = D_MODEL
SWIGLU_LIMIT = 7.0
SWIGLU_ALPHA = 1.702
NORM_EPS = 1e-6
IN_COLS = SSM_WIDTH + 4 * RET_WIDTH + 2 * D_MODEL

LANES = 128
SUBLANES = 8
MXU_DIM = 256
ROUTER_PAD = LANES
NEG_BIG = -1e30

MIB = 1024 * 1024


def _cparams(sem, vmem_mib):
    return pltpu.CompilerParams(dimension_semantics=sem, vmem_limit_bytes=vmem_mib * MIB)


def _rms(xf, g):
    ms = jnp.mean(xf * xf, axis=-1, keepdims=True)
    return xf * lax.rsqrt(ms + NORM_EPS) * g


def _sigmoid(x):
    return 1.0 / (1.0 + jnp.exp(-x))


GATHER_UNROLL = 4


def _gather_rows(src2, dst3, sem, index_of, n_groups, dst_group0, inline=False, first_group=0):
    def body(j, c):
        for u in range(SUBLANES):
            row = index_of(j * SUBLANES + u)
            pltpu.make_async_copy(src2.at[pl.ds(row, 1), :],
                                  dst3.at[dst_group0 + j, pl.ds(u, 1), :], sem).start()
        return c

    if inline:
        for j in range(first_group, first_group + n_groups):
            body(j, 0)
    else:
        lax.fori_loop(first_group, first_group + n_groups, body, 0, unroll=GATHER_UNROLL)


def _inproj_kernel(x_ref, g_ref, w_ref, o_ref, h_scr):
    @pl.when(pl.program_id(1) == 0)
    def _():
        h_scr[...] = _rms(x_ref[...], g_ref[...]).astype(BF16)

    o_ref[...] = jnp.dot(h_scr[...], w_ref[...], preferred_element_type=F32).astype(o_ref.dtype)


def _inproj(x2, g1, w_in_bf, tm, tn):
    L = x2.shape[0]
    n_cols = w_in_bf.shape[1]
    return pl.pallas_call(
        _inproj_kernel,
        name="inproj",
        grid=(L // tm, n_cols // tn),
        in_specs=[
            pl.BlockSpec((tm, D_MODEL), lambda i, j: (i, 0)),
            pl.BlockSpec((1, D_MODEL), lambda i, j: (0, 0)),
            pl.BlockSpec((D_MODEL, tn), lambda i, j: (0, j)),
        ],
        out_specs=pl.BlockSpec((tm, tn), lambda i, j: (i, j)),
        out_shape=jax.ShapeDtypeStruct((L, n_cols), BF16),
        scratch_shapes=[pltpu.VMEM((tm, D_MODEL), BF16)],
        compiler_params=_cparams(("arbitrary", "arbitrary"), 48),
    )(x2, g1, w_in_bf)


def _s5_prep_kernel(lr_ref, li_ref, ldt_ref, bre_ref, bim_ref, abre_ref, abim_ref, bbre_ref, bbim_ref):
    lr, li = lr_ref[...], li_ref[...]
    dt = jnp.exp(ldt_ref[...])
    mag = jnp.exp(lr * dt)
    ab_re, ab_im = mag * jnp.cos(li * dt), mag * jnp.sin(li * dt)
    den = lr * lr + li * li
    nr, ni = ab_re - 1.0, ab_im
    f_re = (nr * lr + ni * li) / den
    f_im = (ni * lr - nr * li) / den
    bre, bim = bre_ref[...], bim_ref[...]
    abre_ref[...] = ab_re
    abim_ref[...] = ab_im
    bbre_ref[...] = f_re * bre - f_im * bim
    bbim_ref[...] = f_re * bim + f_im * bre


def _s5_prep(lam_re, lam_im, log_dt, b_re, b_im):
    rep = lambda a: jnp.repeat(a, SSM_GROUP, axis=0)
    bt = lambda b: jnp.transpose(b, (0, 2, 1)).reshape(SSM_WIDTH, SSM_STATE)
    shp = jax.ShapeDtypeStruct((SSM_WIDTH, SSM_STATE), F32)
    return pl.pallas_call(_s5_prep_kernel, name="s5_prep", out_shape=(shp, shp, shp, shp))(
        rep(lam_re), rep(lam_im), rep(log_dt[:, None]), bt(b_re), bt(b_im))


def _gelu_tanh(x):
    return 0.5 * x * (1.0 + jnp.tanh(math.sqrt(2.0 / math.pi) * (x + 0.044715 * (x * x * x))))


N_UBLK = SSM_WIDTH // LANES
ST_PER_UBLK = (LANES // SSM_GROUP) * SSM_STATE
TILES_PER_UBLK = ST_PER_UBLK // LANES
N_SVREG = SSM_GROUPS * SSM_STATE // (LANES * SUBLANES)


def _s5_kernel(u_ref, bblk_ref, cblk_ref, are_ref, aim_ref, d_ref, wglu_ref, bglu_ref, o_ref,
               sre, sim, carry):
    tm = u_ref.shape[0]

    @pl.when(pl.program_id(0) == 0)
    def _():
        carry[...] = jnp.zeros_like(carry)

    u = u_ref[...]
    for b in range(N_UBLK):
        bu = jnp.dot(u[:, b * LANES:(b + 1) * LANES], bblk_ref[b], preferred_element_type=F32)
        v = (b * TILES_PER_UBLK) // SUBLANES
        for q in range(TILES_PER_UBLK):
            k = (b * TILES_PER_UBLK + q) % SUBLANES
            sre[v, pl.ds(k, tm, stride=SUBLANES), :] = bu[:, q * LANES:(q + 1) * LANES]
            sim[v, pl.ds(k, tm, stride=SUBLANES), :] = bu[:, ST_PER_UBLK + q * LANES:ST_PER_UBLK + (q + 1) * LANES]

    are, aim = are_ref[...], aim_ref[...]

    def step(t, c):
        cre, cim = c
        r0 = pl.multiple_of(t * SUBLANES, SUBLANES)
        bre = sre[:, pl.ds(r0, SUBLANES), :]
        bim = sim[:, pl.ds(r0, SUBLANES), :]
        nre = are * cre - aim * cim + bre
        nim = are * cim + aim * cre + bim
        sre[:, pl.ds(r0, SUBLANES), :] = nre
        sim[:, pl.ds(r0, SUBLANES), :] = nim
        return nre, nim

    cre, cim = lax.fori_loop(0, tm, step, (carry[0], carry[1]), unroll=4)
    carry[0] = cre
    carry[1] = cim

    ys = []
    for b in range(N_UBLK):
        v = (b * TILES_PER_UBLK) // SUBLANES
        k0 = (b * TILES_PER_UBLK) % SUBLANES
        parts = [sre[v, pl.ds(k0 + q, tm, stride=SUBLANES), :] for q in range(TILES_PER_UBLK)]
        parts += [sim[v, pl.ds(k0 + q, tm, stride=SUBLANES), :] for q in range(TILES_PER_UBLK)]
        sb = jnp.concatenate(parts, axis=-1).astype(BF16)
        ys.append(jnp.dot(sb, cblk_ref[b], preferred_element_type=F32))
    y = jnp.concatenate(ys, axis=-1) + d_ref[...] * u.astype(F32)
    y = _gelu_tanh(y)
    gl = jnp.dot(y.astype(BF16), wglu_ref[...], preferred_element_type=F32) + bglu_ref[...]
    o_ref[...] = (y * _sigmoid(gl)).astype(o_ref.dtype)


def _s5(z, u_blk, bblk, cblk, are, aim, d_skip, wglu, bglu, tm):
    L = z.shape[0]
    const = lambda *shape: pl.BlockSpec(shape, lambda i: (0,) * len(shape))
    return pl.pallas_call(
        _s5_kernel,
        name="s5_branch",
        grid=(L // tm,),
        in_specs=[
            pl.BlockSpec((tm, SSM_WIDTH), lambda i: (i, u_blk)),
            const(N_UBLK, LANES, 2 * ST_PER_UBLK),
            const(N_UBLK, 2 * ST_PER_UBLK, LANES),
            const(N_SVREG, SUBLANES, LANES),
            const(N_SVREG, SUBLANES, LANES),
            const(1, SSM_WIDTH),
            const(SSM_WIDTH, SSM_WIDTH),
            const(1, SSM_WIDTH),
        ],
        out_specs=pl.BlockSpec((tm, SSM_WIDTH), lambda i: (i, 0)),
        out_shape=jax.ShapeDtypeStruct((L, SSM_WIDTH), BF16),
        scratch_shapes=[
            pltpu.VMEM((N_SVREG, tm * SUBLANES, LANES), F32),
            pltpu.VMEM((N_SVREG, tm * SUBLANES, LANES), F32),
            pltpu.VMEM((2, N_SVREG, SUBLANES, LANES), F32),
        ],
        compiler_params=_cparams(("arbitrary",), 48),
    )(z, bblk, cblk, are, aim, d_skip, wglu, bglu)


def _ret_kernel(q_ref, k_ref, v_ref, g_ref, pos_ref, inv_ref, decay_ref, zeta_ref, xi_ref, cd_ref,
                o_ref, r_scr):
    tm = q_ref.shape[0]
    C, dk = RET_CHUNK, RET_HEAD_DIM

    @pl.when(pl.program_id(0) == 0)
    def _():
        r_scr[...] = jnp.zeros_like(r_scr)

    ang = pos_ref[...].astype(F32) * inv_ref[...]
    cs, sn = jnp.cos(ang), jnp.sin(ang)
    cf = jnp.concatenate([cs, cs], axis=-1)
    sf = jnp.concatenate([-sn, sn], axis=-1)

    def rot(xh):
        return xh * cf + pltpu.roll(xh, dk // 2, axis=1) * sf

    nt = (((1,), (1,)), ((), ()))
    for h in range(RET_HEADS):
        cols = slice(h * dk, (h + 1) * dk)
        qr = rot(q_ref[:, cols].astype(F32))
        kr = rot(k_ref[:, cols].astype(F32))
        for n in range(tm // C):
            rows = slice(n * C, (n + 1) * C)
            qc = qr[rows].astype(BF16)
            kcf = kr[rows]
            vc = v_ref[rows, cols]
            sc = lax.dot_general(qc, kcf.astype(BF16), nt, preferred_element_type=F32) * decay_ref[h]
            inner = jnp.dot(sc.astype(BF16), vc, preferred_element_type=F32)
            r_prev = r_scr[h]
            cross = jnp.dot(qc, r_prev.astype(BF16), preferred_element_type=F32) * xi_ref[h]
            o = inner + cross
            kzt = jnp.transpose(kcf * zeta_ref[h]).astype(BF16)
            r_scr[h] = r_prev * cd_ref[h] + jnp.dot(kzt, vc, preferred_element_type=F32)
            mu = jnp.mean(o, axis=-1, keepdims=True)
            oc = o - mu
            on = oc * lax.rsqrt(jnp.mean(oc * oc, axis=-1, keepdims=True) + NORM_EPS)
            gate = g_ref[rows, cols].astype(F32)
            o_ref[rows, cols] = (gate * _sigmoid(gate) * on).astype(o_ref.dtype)


def _retention(z, blks, pos, tm):
    L = z.shape[0]
    H, C, dk = RET_HEADS, RET_CHUNK, RET_HEAD_DIM
    half = dk // 2
    inv = (ROPE_BASE ** (-jnp.arange(half, dtype=F32) / half))[None, :]
    log_gamma = jnp.log(1.0 - jnp.exp2(-5.0 - jnp.arange(H, dtype=F32)))
    idx = jnp.arange(C, dtype=F32)
    rel = idx[:, None] - idx[None, :]
    scale = dk ** -0.5
    decay = jnp.where(rel >= 0, jnp.exp(jnp.maximum(rel, 0.0)[None] * log_gamma[:, None, None]), 0.0) * scale
    zeta = jnp.exp((C - 1.0 - idx)[None, :] * log_gamma[:, None]) * scale
    xi = jnp.exp((idx + 1.0)[None, :] * log_gamma[:, None])
    bc = lambda a: jnp.broadcast_to(a[:, :, None], (H, C, dk))
    cd = jnp.broadcast_to(jnp.exp(C * log_gamma)[:, None, None], (H, 1, dk))
    const = lambda *shape: pl.BlockSpec(shape, lambda i: (0,) * len(shape))
    qb, kb, vb, gb = blks
    return pl.pallas_call(
        _ret_kernel,
        name="retention",
        grid=(L // tm,),
        in_specs=[
            pl.BlockSpec((tm, RET_WIDTH), lambda i: (i, qb)),
            pl.BlockSpec((tm, RET_WIDTH), lambda i: (i, kb)),
            pl.BlockSpec((tm, RET_WIDTH), lambda i: (i, vb)),
            pl.BlockSpec((tm, RET_WIDTH), lambda i: (i, gb)),
            pl.BlockSpec((tm, 1), lambda i: (i, 0)),
            const(1, half),
            const(H, C, C),
            const(H, C, dk),
            const(H, C, dk),
            const(H, 1, dk),
        ],
        out_specs=pl.BlockSpec((tm, RET_WIDTH), lambda i: (i, 0)),
        out_shape=jax.ShapeDtypeStruct((L, RET_WIDTH), BF16),
        scratch_shapes=[pltpu.VMEM((H, dk, dk), F32)],
        compiler_params=_cparams(("arbitrary",), 48),
    )(z, z, z, z, pos, inv, decay, bc(zeta), bc(xi), cd)


def _merge_kernel(ys_ref, yr_ref, ms_ref, mr_ref, x_ref, ps_ref, pr_ref, wo_ref, g2_ref,
                  rwh_ref, rwl_ref, rb_ref, tril_ref,
                  x1_ref, h2_ref, idx_ref, gate_ref, rank_ref, cnt_ref, carry):
    tm = x_ref.shape[0]

    @pl.when(pl.program_id(0) == 0)
    def _():
        carry[...] = jnp.zeros_like(carry)

    a = jnp.dot(ys_ref[...], ps_ref[...], preferred_element_type=F32)
    b = jnp.dot(yr_ref[...], pr_ref[...], preferred_element_type=F32)
    merged = _sigmoid(ms_ref[...].astype(F32)) * a + _sigmoid(mr_ref[...].astype(F32)) * b
    x1 = x_ref[...] + jnp.dot(merged.astype(BF16), wo_ref[...], preferred_element_type=F32)
    x1_ref[...] = x1
    h2 = _rms(x1, g2_ref[...])
    h2_ref[...] = h2

    hh = h2.astype(BF16)
    hl = (h2 - hh.astype(F32)).astype(BF16)
    logits = (jnp.dot(hh, rwh_ref[...], preferred_element_type=F32)
              + jnp.dot(hh, rwl_ref[...], preferred_element_type=F32)
              + jnp.dot(hl, rwh_ref[...], preferred_element_type=F32)) + rb_ref[...]

    lane = lax.broadcasted_iota(jnp.int32, (tm, ROUTER_PAD), 1)
    work = logits
    vals, idxs = [], []
    for _ in range(TOP_K):
        m = jnp.max(work, axis=-1, keepdims=True)
        ix = jnp.min(jnp.where(work == m, lane, ROUTER_PAD), axis=-1, keepdims=True)
        vals.append(m)
        idxs.append(ix)
        work = jnp.where(lane == ix, -jnp.inf, work)
    es = [jnp.exp(v - vals[0]) for v in vals]
    den = es[0] + es[1] + es[2] + es[3]

    onehot = jnp.zeros((tm, ROUTER_PAD), F32)
    for ix in idxs:
        onehot = onehot + (lane == ix).astype(F32)
    cum = jnp.dot(tril_ref[...], onehot.astype(BF16), preferred_element_type=F32) + carry[...]
    carry[...] = carry[...] + jnp.sum(onehot, axis=0, keepdims=True)
    cnt_ref[...] = jnp.broadcast_to(carry[...], cnt_ref.shape).astype(jnp.int32)

    idx_o = jnp.zeros((tm, ROUTER_PAD), jnp.int32)
    gate_o = jnp.zeros((tm, ROUTER_PAD), F32)
    rank_o = jnp.zeros((tm, ROUTER_PAD), jnp.int32)
    for k in range(TOP_K):
        rk = jnp.sum(jnp.where(lane == idxs[k], cum, 0.0), axis=-1, keepdims=True).astype(jnp.int32)
        idx_o = jnp.where(lane == k, idxs[k], idx_o)
        gate_o = jnp.where(lane == k, es[k] / den, gate_o)
        rank_o = jnp.where(lane == k, rk, rank_o)
    idx_ref[...] = idx_o
    gate_ref[...] = gate_o
    rank_ref[...] = rank_o


def _merge(ys, yr, z, ms_blk, mr_blk, x2, ps, pr, wo, g2, rwh, rwl, rb, tm):
    L = x2.shape[0]
    tril = (jnp.arange(tm)[:, None] > jnp.arange(tm)[None, :]).astype(BF16)
    const = lambda *shape: pl.BlockSpec(shape, lambda i: (0,) * len(shape), pipeline_mode=pl.Buffered(1))
    row = lambda w: pl.BlockSpec((tm, w), lambda i: (i, 0))
    return pl.pallas_call(
        _merge_kernel,
        name="merge_router",
        grid=(L // tm,),
        in_specs=[
            row(SSM_WIDTH), row(RET_WIDTH),
            pl.BlockSpec((tm, D_MODEL), lambda i: (i, ms_blk)),
            pl.BlockSpec((tm, D_MODEL), lambda i: (i, mr_blk)),
            row(D_MODEL),
            const(SSM_WIDTH, D_MODEL), const(RET_WIDTH, D_MODEL), const(D_MODEL, D_MODEL),
            const(1, D_MODEL),
            const(D_MODEL, ROUTER_PAD), const(D_MODEL, ROUTER_PAD), const(1, ROUTER_PAD),
            const(tm, tm),
        ],
        out_specs=[
            row(D_MODEL), row(D_MODEL),
            row(ROUTER_PAD), row(ROUTER_PAD), row(ROUTER_PAD),
            pl.BlockSpec((SUBLANES, ROUTER_PAD), lambda i: (0, 0)),
        ],
        out_shape=[
            jax.ShapeDtypeStruct((L, D_MODEL), F32),
            jax.ShapeDtypeStruct((L, D_MODEL), F32),
            jax.ShapeDtypeStruct((L, ROUTER_PAD), jnp.int32),
            jax.ShapeDtypeStruct((L, ROUTER_PAD), F32),
            jax.ShapeDtypeStruct((L, ROUTER_PAD), jnp.int32),
            jax.ShapeDtypeStruct((SUBLANES, ROUTER_PAD), jnp.int32),
        ],
        scratch_shapes=[pltpu.VMEM((1, ROUTER_PAD), F32)],
        compiler_params=_cparams(("arbitrary",), 56),
    )(ys, yr, z, z, x2, ps, pr, wo, g2, rwh, rwl, rb, tril)


def _row_tokens_kernel(dest_ref, out_ref):
    def clear(i, c):
        out_ref[i] = 0
        return c

    def place(t, c):
        for k in range(TOP_K):
            out_ref[dest_ref[t * TOP_K + k]] = t
        return c

    lax.fori_loop(0, out_ref.shape[0], clear, 0, unroll=8)
    lax.fori_loop(0, dest_ref.shape[0] // TOP_K, place, 0, unroll=4)


def _row_tokens(dest, n_rows):
    smem = pl.BlockSpec(memory_space=pltpu.SMEM)
    return pl.pallas_call(
        _row_tokens_kernel, name="row_tokens", in_specs=[smem], out_specs=smem,
        out_shape=jax.ShapeDtypeStruct((n_rows,), jnp.int32))(dest)


SUB_ROWS = 128
ITEM_SUBS = 9
BLOCK_SUBS = 8
N_F = 4
SUB_GROUPS = SUB_ROWS // SUBLANES
ACC_COLS = 512


def _expert_kernel(item_e_ref, item_g0_ref, item_nc_ref, item_no_ref, rowtok_ref,
                   h2_hbm, wgu_ref, wd_ref, bgu_ref, bd_ref, unpack_ref, yb_hbm,
                   xbuf, xb, acc, wgu_bf, wd_bf, gsem, osem):
    w = pl.program_id(0)
    f = pl.program_id(1)
    n_w = pl.num_programs(0)
    n_f = pl.num_programs(1)
    tf = wd_ref.shape[1]
    nc = item_nc_ref[w]
    no = item_no_ref[w]

    def groups(s):
        return pl.ds(s * SUB_GROUPS, SUB_GROUPS)

    def gather_sub(item, s):
        base = item_g0_ref[item] * SUBLANES + s * SUB_ROWS
        _gather_rows(h2_hbm, xbuf, gsem.at[0], lambda r: rowtok_ref[base + r], SUB_GROUPS, s * SUB_GROUPS)

    def gather_wait(s):
        pltpu.make_async_copy(xbuf.at[groups(s)], xbuf.at[groups(s)], gsem.at[0]).wait()

    def out_copy(s, g0):
        row0 = pl.multiple_of((g0 + s * SUB_GROUPS) * SUBLANES, SUBLANES)
        src0 = s * SUB_ROWS if isinstance(s, int) else pl.multiple_of(s * SUB_ROWS, SUB_ROWS)
        return pltpu.make_async_copy(acc.at[pl.ds(src0, SUB_ROWS), :],
                                     yb_hbm.at[pl.ds(row0, SUB_ROWS), :], osem.at[0])

    def for_subs(count, fn):
        for s in range(ITEM_SUBS):
            pl.when(s < count)(functools.partial(fn, s))

    @pl.when(f == 0)
    def _():
        @pl.when(w == 0)
        def _():
            for_subs(nc, lambda s: gather_sub(0, s))

        for_subs(nc, gather_wait)

        def stage(s):
            xb[s * SUB_ROWS:(s + 1) * SUB_ROWS, :] = xbuf[groups(s)].reshape(SUB_ROWS, D_MODEL).astype(BF16)

        for_subs(nc, stage)

        @pl.when(w > 0)
        def _():
            for_subs(item_no_ref[jnp.maximum(w - 1, 0)], lambda s: out_copy(s, 0).wait())

        init = bd_ref[0] * (nc > 0).astype(F32)

        def init_acc(s):
            acc[s * SUB_ROWS:(s + 1) * SUB_ROWS, :] = jnp.broadcast_to(init, (SUB_ROWS, D_MODEL))

        for_subs(no, init_acc)

    nxt_item = jnp.minimum(w + 1, n_w - 1)
    nxt_nc = jnp.where(w + 1 < n_w, item_nc_ref[nxt_item], 0)
    last = f == n_f - 1

    per_step = BLOCK_SUBS // N_F
    for i in range(per_step):
        s_dyn = f * per_step + i
        pl.when(s_dyn < nxt_nc)(functools.partial(gather_sub, nxt_item, s_dyn))

    for s in range(BLOCK_SUBS, ITEM_SUBS):
        pl.when(last & (s < nxt_nc))(functools.partial(gather_sub, nxt_item, s))

    def act(guc):
        g = jnp.minimum(guc, SWIGLU_LIMIT)
        u = jnp.clip(pltpu.roll(guc, LANES - 1, axis=1), -SWIGLU_LIMIT, SWIGLU_LIMIT)
        return g * _sigmoid(SWIGLU_ALPHA * g) * (u + 1.0)

    def rows_block(r0, m, cast):
        rows = pl.ds(r0, m) if isinstance(r0, int) else pl.ds(pl.multiple_of(r0, SUB_ROWS), m)
        x = xb[rows, :]
        n_k = tf // MXU_DIM
        even = (lax.broadcasted_iota(jnp.int32, (m, LANES), 1) & 1) == 0

        def gate_up(k):
            outs = []
            for n in range(2):
                cols = slice((2 * k + n) * MXU_DIM, (2 * k + n + 1) * MXU_DIM)
                if cast:
                    wt = wgu_ref[0, :, cols].astype(BF16)
                    wgu_bf[:, cols] = wt
                else:
                    wt = wgu_bf[:, cols]
                outs.append(jnp.dot(x, wt, preferred_element_type=F32) + bgu_ref[0, :, cols])
            return outs

        def activate(gus):
            parts = [jnp.where(even, act(gu[:, :LANES]), pltpu.roll(act(gu[:, LANES:]), 1, axis=1))
                     for gu in gus]
            packed = jnp.concatenate(parts, axis=-1).astype(BF16)
            return jnp.dot(packed, unpack_ref[...], preferred_element_type=F32).astype(BF16)

        gus = gate_up(0)
        for k in range(n_k):
            nxt = gate_up(k + 1) if k + 1 < n_k else None
            a = activate(gus)
            wrows = slice(k * MXU_DIM, (k + 1) * MXU_DIM)
            if cast:
                wdt = wd_ref[0, wrows, :].astype(BF16)
                wd_bf[wrows, :] = wdt
            else:
                wdt = wd_bf[wrows, :]
            for c in range(D_MODEL // ACC_COLS):
                cs = slice(c * ACC_COLS, (c + 1) * ACC_COLS)
                acc[rows, cs] += jnp.dot(a, wdt[:, cs], preferred_element_type=F32)
            gus = nxt

    g0 = item_g0_ref[w]
    block_rows = BLOCK_SUBS * SUB_ROWS
    big = nc >= BLOCK_SUBS

    def extra_sub(s, start_out):
        rows_block(s * SUB_ROWS, SUB_ROWS, False)
        if start_out:
            out_copy(s, g0).start()

    @pl.when(big & jnp.logical_not(last))
    def _():
        rows_block(0, block_rows, True)
        for s in range(BLOCK_SUBS, ITEM_SUBS):
            pl.when(s < nc)(functools.partial(extra_sub, s, False))

    @pl.when(big & last)
    def _():
        half = BLOCK_SUBS // 2
        rows_block(0, block_rows // 2, True)
        for s in range(half):
            out_copy(s, g0).start()
        rows_block(block_rows // 2, block_rows // 2, False)
        for s in range(half, BLOCK_SUBS):
            out_copy(s, g0).start()
        for s in range(BLOCK_SUBS, ITEM_SUBS):
            pl.when(s < nc)(functools.partial(extra_sub, s, True))

    @pl.when((nc > 0) & jnp.logical_not(big))
    def _():
        def chunk(s0, n_sub, cast):
            rows_block(s0 * SUB_ROWS, n_sub * SUB_ROWS, cast)

            def start_outs():
                for i in range(n_sub):
                    out_copy(s0 + i, g0).start()

            pl.when(last)(start_outs)

        sizes = []
        n_sub = BLOCK_SUBS // 2
        while n_sub >= 1:
            sizes.append(n_sub)
            n_sub //= 2
        for i, n_sub in enumerate(sizes):
            has = (nc & n_sub) != 0
            larger = nc & ~(2 * n_sub - 1)
            pl.when(has & (larger == 0))(functools.partial(chunk, 0, n_sub, True))
            if i > 0:
                pl.when(has & (larger != 0))(functools.partial(chunk, larger, n_sub, False))

    @pl.when((nc == 0) & last)
    def _():
        for_subs(no, lambda s: out_copy(s, g0).start())

    @pl.when(last & (w == n_w - 1))
    def _():
        for_subs(no, lambda s: out_copy(s, 0).wait())


def _unpack_matrix():
    lane = jnp.arange(MXU_DIM)
    n, r = lane // LANES, lane % LANES
    natural = n * LANES + (r % 2) * (LANES // 2) + r // 2
    return (natural[:, None] == jnp.arange(MXU_DIM)[None, :]).astype(BF16)


def _expert_items(counts, n_asg):
    E = N_EXPERTS
    item_rows = SUB_ROWS * ITEM_SUBS
    n_items_max = (n_asg // SUB_ROWS + E) // ITEM_SUBS + E + 1
    n_rows = n_asg + E * SUB_ROWS
    pcounts = (counts + SUB_ROWS - 1) // SUB_ROWS * SUB_ROWS
    pend = jnp.cumsum(pcounts)
    pstart = pend - pcounts
    nsb_e = pcounts // SUB_ROWS
    n_it = (nsb_e + ITEM_SUBS - 1) // ITEM_SUBS
    extra_e = jnp.maximum(nsb_e - BLOCK_SUBS * n_it, 0)
    it_end = jnp.cumsum(n_it)
    it_start = it_end - n_it
    n_items = it_end[-1]
    wv = jnp.arange(n_items_max, dtype=jnp.int32)
    e_w = jnp.minimum(jnp.sum(wv[:, None] >= it_end[None, :], axis=1), E - 1)
    j_w = wv - it_start[e_w]
    first_sub = lambda j: BLOCK_SUBS * j + jnp.minimum(j, extra_e[e_w])
    sb0 = first_sub(j_w)
    row0 = pstart[e_w] + sb0 * SUB_ROWS
    nsb = jnp.clip(jnp.minimum(nsb_e[e_w], first_sub(j_w + 1)) - sb0, 0, ITEM_SUBS)
    used = wv < n_items
    tail0 = pend[-1] + (wv - n_items) * item_rows
    nz = jnp.clip((n_rows - tail0) // SUB_ROWS, 0, ITEM_SUBS)
    item_row0 = jnp.clip(jnp.where(used, row0, tail0), 0, n_rows - SUB_ROWS)
    item_e = jnp.where(used, e_w, e_w[jnp.maximum(n_items - 1, 0)])
    i32 = lambda a: a.astype(jnp.int32)
    return (i32(item_e), i32(item_row0 // SUBLANES), i32(jnp.where(used, nsb, 0)),
            i32(jnp.where(used, nsb, nz)), pstart, n_rows)


def _experts(item_e, item_g0, item_nc, item_no, row_tok, h2, wgu, wd, bgu, bd, tf):
    n_rows = row_tok.shape[0]
    n_items = item_e.shape[0]
    n_f = EXPERT_FF // tf
    item_rows = SUB_ROWS * ITEM_SUBS
    assert tf % MXU_DIM == 0 and n_f == N_F and N_F <= BLOCK_SUBS <= ITEM_SUBS

    def f_eff(w, f, nc):
        return jnp.where(nc[w] > 0, f, n_f - 1)

    grid_spec = pltpu.PrefetchScalarGridSpec(
        num_scalar_prefetch=5,
        grid=(n_items, n_f),
        in_specs=[
            pl.BlockSpec(memory_space=pl.ANY),
            pl.BlockSpec((1, D_MODEL, 2 * tf), lambda w, f, ie, g0, nc, no, rt: (ie[w], 0, f_eff(w, f, nc))),
            pl.BlockSpec((1, tf, D_MODEL), lambda w, f, ie, g0, nc, no, rt: (ie[w], f_eff(w, f, nc), 0)),
            pl.BlockSpec((1, 1, 2 * tf), lambda w, f, ie, g0, nc, no, rt: (ie[w], 0, f_eff(w, f, nc))),
            pl.BlockSpec((1, 1, D_MODEL), lambda w, f, ie, g0, nc, no, rt: (ie[w], 0, 0)),
            pl.BlockSpec((MXU_DIM, MXU_DIM), lambda w, f, ie, g0, nc, no, rt: (0, 0)),
        ],
        out_specs=pl.BlockSpec(memory_space=pl.ANY),
        scratch_shapes=[
            pltpu.VMEM((item_rows // SUBLANES, SUBLANES, D_MODEL), F32),
            pltpu.VMEM((item_rows, D_MODEL), BF16),
            pltpu.VMEM((item_rows, D_MODEL), F32),
            pltpu.VMEM((D_MODEL, 2 * tf), BF16),
            pltpu.VMEM((tf, D_MODEL), BF16),
            pltpu.SemaphoreType.DMA((1,)),
            pltpu.SemaphoreType.DMA((1,)),
        ],
    )
    return pl.pallas_call(
        _expert_kernel,
        name="experts",
        grid_spec=grid_spec,
        out_shape=jax.ShapeDtypeStruct((n_rows, D_MODEL), F32),
        compiler_params=_cparams(("arbitrary", "arbitrary"), 60),
    )(item_e, item_g0, item_nc, item_no, row_tok, h2, wgu, wd, bgu, bd, _unpack_matrix())


def _combine_kernel(dest_ref, yb_hbm, gate_ref, x1_ref, p_ref, wpg_ref, wpp_ref, g3_ref, gf_ref,
                    o_ref, gbuf, sem):
    i = pl.program_id(0)
    n = pl.num_programs(0)
    tm = x1_ref.shape[0]

    def issue(tile):
        base = tile * tm * TOP_K

        for k in range(TOP_K):
            _gather_rows(yb_hbm, gbuf.at[k], sem.at[0], lambda r, k=k: dest_ref[base + r * TOP_K + k],
                         tm // SUBLANES, 0)

    @pl.when(i == 0)
    def _():
        issue(0)

    for k in range(TOP_K):
        pltpu.make_async_copy(gbuf.at[k], gbuf.at[k], sem.at[0]).wait()
    gates = gate_ref[...]
    moe = gates[:, 0:1] * gbuf[0].reshape(tm, D_MODEL)
    for k in range(1, TOP_K):
        moe = moe + gates[:, k:k + 1] * gbuf[k].reshape(tm, D_MODEL)
    x2 = x1_ref[...] + moe

    @pl.when(i + 1 < n)
    def _():
        issue(i + 1)

    h3 = _rms(x2, g3_ref[...]).astype(BF16)
    pg = _sigmoid(jnp.dot(h3, wpg_ref[...], preferred_element_type=F32))
    pp = jnp.dot(p_ref[...].astype(BF16), wpp_ref[...], preferred_element_type=F32)
    x3 = x2 + pg * pp
    o_ref[...] = _rms(x3, gf_ref[...])


def _combine(dest, yb, gates, x1, p2, wpg, wpp, g3, gf, tm):
    L = x1.shape[0]
    const = lambda *shape: pl.BlockSpec(shape, lambda i, d: (0,) * len(shape), pipeline_mode=pl.Buffered(1))
    row = lambda w: pl.BlockSpec((tm, w), lambda i, d: (i, 0))
    grid_spec = pltpu.PrefetchScalarGridSpec(
        num_scalar_prefetch=1,
        grid=(L // tm,),
        in_specs=[
            pl.BlockSpec(memory_space=pl.ANY),
            row(ROUTER_PAD), row(D_MODEL), row(PLE_DIM),
            const(D_MODEL, D_MODEL), const(PLE_DIM, D_MODEL), const(1, D_MODEL), const(1, D_MODEL),
        ],
        out_specs=row(D_MODEL),
        scratch_shapes=[
            pltpu.VMEM((TOP_K, tm // SUBLANES, SUBLANES, D_MODEL), F32),
            pltpu.SemaphoreType.DMA((1,)),
        ],
    )
    return pl.pallas_call(
        _combine_kernel,
        name="combine_ple",
        grid_spec=grid_spec,
        out_shape=jax.ShapeDtypeStruct((L, D_MODEL), F32),
        compiler_params=_cparams(("arbitrary",), 48),
    )(dest, yb, gates, x1, p2, wpg, wpp, g3, gf)


def _tile(L, pref):
    return min(pref, L)


def _layer(x2, p2, pos, norm1_g, w_in, lam_re, lam_im, log_dt, b_re, b_im, c_re, c_im, d_skip, w_glu,
           b_glu, w_bs, w_br, w_out, norm2_g, router_w, router_b, w_gu, b_gu, w_dn, b_dn, norm3_g,
           w_pg, w_pp, out_g):
    L = x2.shape[0]
    G, N, E = SSM_GROUPS, SSM_STATE, N_EXPERTS

    cuts = SSM_WIDTH + 4 * RET_WIDTH
    w_in_bf = jnp.concatenate([w_in[:, cuts:], w_in[:, :cuts]], axis=1).astype(BF16)
    z = _inproj(x2, norm1_g[None, :], w_in_bf, _tile(L, 1024), 1024)
    ms_blk, mr_blk = 0, 1
    u_blk, q_blk, k_blk, v_blk, g_blk = 4, 5, 6, 7, 8

    ab_re, ab_im, bb_re, bb_im = _s5_prep(lam_re, lam_im, log_dt, b_re, b_im)
    eye = jnp.eye(SUBLANES, dtype=F32)
    gpb = LANES // SSM_GROUP
    def bpack(bb):
        t = bb.reshape(N_UBLK, gpb, SSM_GROUP, N)
        return jnp.einsum('bgcn,gh->bgchn', t, eye).reshape(N_UBLK, LANES, ST_PER_UBLK)
    def cpack(c):
        t = c.reshape(N_UBLK, gpb, SSM_GROUP, N)
        return jnp.einsum('bgcn,gh->bgnhc', t, eye).reshape(N_UBLK, ST_PER_UBLK, LANES)
    bblk = jnp.concatenate([bpack(bb_re), bpack(bb_im)], axis=-1).astype(BF16)
    cblk = jnp.concatenate([cpack(c_re), -cpack(c_im)], axis=1).astype(BF16)
    atile = lambda a: a[::SSM_GROUP].reshape(N_SVREG, SUBLANES, LANES)
    ys = _s5(z, u_blk, bblk, cblk, atile(ab_re), atile(ab_im), d_skip.reshape(1, SSM_WIDTH),
             w_glu.astype(BF16), b_glu[None, :], _tile(L, 512))

    yr = _retention(z, (q_blk, k_blk, v_blk, g_blk), pos, _tile(L, 512))

    rw = jnp.pad(router_w, ((0, 0), (0, ROUTER_PAD - E)))
    rwh = rw.astype(BF16)
    rwl = (rw - rwh.astype(F32)).astype(BF16)
    rb = jnp.pad(router_b, (0, ROUTER_PAD - E), constant_values=NEG_BIG)[None, :]
    x1, h2, idx, gates, rank, cnt = _merge(
        ys, yr, z, ms_blk, mr_blk, x2, w_bs.astype(BF16), w_br.astype(BF16), w_out.astype(BF16),
        norm2_g[None, :], rwh, rwl, rb, _tile(L, 256))

    item_e, item_g0, item_nc, item_no, pstart, n_rows = _expert_items(cnt[0, :E], L * TOP_K)
    idx4, rank4 = idx[:, :TOP_K], rank[:, :TOP_K]
    dest = (pstart[idx4] + rank4).astype(jnp.int32).reshape(-1)
    row_tok = _row_tokens(dest, n_rows)

    yb = _experts(item_e, item_g0, item_nc, item_no, row_tok, h2, w_gu, w_dn, b_gu[:, None, :],
                  b_dn[:, None, :], EXPERT_FF // N_F)

    return _combine(dest, yb, gates, x1, p2, w_pg.astype(BF16), w_pp.astype(BF16),
                    norm3_g[None, :], out_g[None, :], _tile(L, 256))


def kernel(x, p, positions, norm1_g, w_in, ssm_lam_re, ssm_lam_im, ssm_log_dt, ssm_b_re, ssm_b_im, ssm_c_re, ssm_c_im, ssm_d, ssm_w_glu, ssm_b_glu, w_branch_ssm, w_branch_ret, w_out, norm2_g, router_w, router_b, exp_w_gate_up, exp_b_gate_up, exp_w_down, exp_b_down, norm3_g, ple_w_gate, ple_w_proj, final_norm_g):
    bsz, L, d = x.shape
    depth = w_in.shape[0]
    assert bsz == 1 and depth == 1 and d == D_MODEL
    out = _layer(
        x[0], p[0, 0], positions.reshape(L, 1), norm1_g[0], w_in[0], ssm_lam_re[0], ssm_lam_im[0],
        ssm_log_dt[0], ssm_b_re[0], ssm_b_im[0], ssm_c_re[0], ssm_c_im[0], ssm_d[0], ssm_w_glu[0],
        ssm_b_glu[0], w_branch_ssm[0], w_branch_ret[0], w_out[0], norm2_g[0], router_w[0], router_b[0],
        exp_w_gate_up[0], exp_b_gate_up[0], exp_w_down[0], exp_b_down[0], norm3_g[0], ple_w_gate[0],
        ple_w_proj[0], final_norm_g)
    return out[None]
```

```python
import functools
import math

import jax
import jax.numpy as jnp
from jax import lax
from jax.experimental import pallas as pl
from jax.experimental.pallas import tpu as pltpu

F32 = jnp.float32
BF16 = jnp.bfloat16

D_MODEL = 2048
PLE_DIM = 256
SSM_WIDTH = D_MODEL // 2
SSM_GROUP = 16
SSM_GROUPS = SSM_WIDTH // SSM_GROUP
SSM_STATE = 64
RET_HEADS = 8
RET_HEAD_DIM = 128
RET_WIDTH = RET_HEADS * RET_HEAD_DIM
RET_CHUNK = 128
ROPE_BASE = 10000.0
N_EXPERTS = 32
TOP_K = 4
EXPERT_FF = D_MODEL
SWIGLU_LIMIT = 7.0
SWIGLU_ALPHA = 1.702
NORM_EPS = 1e-6
IN_COLS = SSM_WIDTH + 4 * RET_WIDTH + 2 * D_MODEL

LANES = 128
SUBLANES = 8
MXU_DIM = 256
ROUTER_PAD = LANES
NEG_BIG = -1e30

MIB = 1024 * 1024


def _cparams(sem, vmem_mib):
    return pltpu.CompilerParams(dimension_semantics=sem, vmem_limit_bytes=vmem_mib * MIB)


def _rms(xf, g):
    ms = jnp.mean(xf * xf, axis=-1, keepdims=True)
    return xf * lax.rsqrt(ms + NORM_EPS) * g


def _sigmoid(x):
    return 1.0 / (1.0 + jnp.exp(-x))


GATHER_UNROLL = 4


def _gather_rows(src2, dst3, sem, index_of, n_groups, dst_group0, inline=False, first_group=0):
    def body(j, c):
        for u in range(SUBLANES):
            row = index_of(j * SUBLANES + u)
            pltpu.make_async_copy(src2.at[pl.ds(row, 1), :],
                                  dst3.at[dst_group0 + j, pl.ds(u, 1), :], sem).start()
        return c

    if inline:
        for j in range(first_group, first_group + n_groups):
            body(j, 0)
    else:
        lax.fori_loop(first_group, first_group + n_groups, body, 0, unroll=GATHER_UNROLL)


def _inproj_kernel(x_ref, g_ref, w_ref, o_ref, h_scr):
    @pl.when(pl.program_id(1) == 0)
    def _():
        h_scr[...] = _rms(x_ref[...], g_ref[...]).astype(BF16)

    w = w_ref[...].astype(BF16)
    o_ref[...] = jnp.dot(h_scr[...], w, preferred_element_type=F32).astype(o_ref.dtype)


def _inproj(x2, g1, w_in, first_block, tm, tn):
    L = x2.shape[0]
    n_cols = w_in.shape[1]
    n_j = n_cols // tn
    return pl.pallas_call(
        _inproj_kernel,
        name="inproj",
        grid=(L // tm, n_j),
        in_specs=[
            pl.BlockSpec((tm, D_MODEL), lambda i, j: (i, 0)),
            pl.BlockSpec((1, D_MODEL), lambda i, j: (0, 0)),
            pl.BlockSpec((D_MODEL, tn), lambda i, j: (0, (j + first_block) % n_j)),
        ],
        out_specs=pl.BlockSpec((tm, tn), lambda i, j: (i, j)),
        out_shape=jax.ShapeDtypeStruct((L, n_cols), BF16),
        scratch_shapes=[pltpu.VMEM((tm, D_MODEL), BF16)],
        compiler_params=_cparams(("arbitrary", "arbitrary"), 52),
    )(x2, g1, w_in)


def _s5_prep_kernel(lr_ref, li_ref, ldt_ref, bre_ref, bim_ref, abre_ref, abim_ref, bbre_ref, bbim_ref):
    lr, li = lr_ref[...], li_ref[...]
    dt = jnp.exp(ldt_ref[...])
    mag = jnp.exp(lr * dt)
    ab_re, ab_im = mag * jnp.cos(li * dt), mag * jnp.sin(li * dt)
    den = lr * lr + li * li
    nr, ni = ab_re - 1.0, ab_im
    f_re = (nr * lr + ni * li) / den
    f_im = (ni * lr - nr * li) / den
    bre, bim = bre_ref[...], bim_ref[...]
    abre_ref[...] = ab_re
    abim_ref[...] = ab_im
    bbre_ref[...] = f_re * bre - f_im * bim
    bbim_ref[...] = f_re * bim + f_im * bre


def _s5_prep(lam_re, lam_im, log_dt, b_re, b_im):
    rep = lambda a: jnp.repeat(a, SSM_GROUP, axis=0)
    bt = lambda b: jnp.transpose(b, (0, 2, 1)).reshape(SSM_WIDTH, SSM_STATE)
    shp = jax.ShapeDtypeStruct((SSM_WIDTH, SSM_STATE), F32)
    return pl.pallas_call(_s5_prep_kernel, name="s5_prep", out_shape=(shp, shp, shp, shp))(
        rep(lam_re), rep(lam_im), rep(log_dt[:, None]), bt(b_re), bt(b_im))


def _gelu_tanh(x):
    return 0.5 * x * (1.0 + jnp.tanh(math.sqrt(2.0 / math.pi) * (x + 0.044715 * (x * x * x))))


N_UBLK = SSM_WIDTH // LANES
ST_PER_UBLK = (LANES // SSM_GROUP) * SSM_STATE
TILES_PER_UBLK = ST_PER_UBLK // LANES
N_SVREG = SSM_GROUPS * SSM_STATE // (LANES * SUBLANES)


def _s5_kernel(u_ref, bblk_ref, cblk_ref, are_ref, aim_ref, d_ref, wglu_ref, bglu_ref, o_ref,
               sre, sim, carry):
    tm = u_ref.shape[0]

    @pl.when(pl.program_id(0) == 0)
    def _():
        carry[...] = jnp.zeros_like(carry)

    u = u_ref[...]
    for b in range(N_UBLK):
        bu = jnp.dot(u[:, b * LANES:(b + 1) * LANES], bblk_ref[b], preferred_element_type=F32)
        v = (b * TILES_PER_UBLK) // SUBLANES
        for q in range(TILES_PER_UBLK):
            k = (b * TILES_PER_UBLK + q) % SUBLANES
            sre[v, pl.ds(k, tm, stride=SUBLANES), :] = bu[:, q * LANES:(q + 1) * LANES]
            sim[v, pl.ds(k, tm, stride=SUBLANES), :] = bu[:, ST_PER_UBLK + q * LANES:ST_PER_UBLK + (q + 1) * LANES]

    are, aim = are_ref[...], aim_ref[...]

    def step(t, c):
        cre, cim = c
        r0 = pl.multiple_of(t * SUBLANES, SUBLANES)
        bre = sre[:, pl.ds(r0, SUBLANES), :]
        bim = sim[:, pl.ds(r0, SUBLANES), :]
        nre = are * cre - aim * cim + bre
        nim = are * cim + aim * cre + bim
        sre[:, pl.ds(r0, SUBLANES), :] = nre
        sim[:, pl.ds(r0, SUBLANES), :] = nim
        return nre, nim

    cre, cim = lax.fori_loop(0, tm, step, (carry[0], carry[1]), unroll=4)
    carry[0] = cre
    carry[1] = cim

    ys = []
    for b in range(N_UBLK):
        v = (b * TILES_PER_UBLK) // SUBLANES
        k0 = (b * TILES_PER_UBLK) % SUBLANES
        parts = [sre[v, pl.ds(k0 + q, tm, stride=SUBLANES), :] for q in range(TILES_PER_UBLK)]
        parts += [sim[v, pl.ds(k0 + q, tm, stride=SUBLANES), :] for q in range(TILES_PER_UBLK)]
        sb = jnp.concatenate(parts, axis=-1).astype(BF16)
        ys.append(jnp.dot(sb, cblk_ref[b], preferred_element_type=F32))
    y = jnp.concatenate(ys, axis=-1) + d_ref[...] * u.astype(F32)
    y = _gelu_tanh(y)
    gl = jnp.dot(y.astype(BF16), wglu_ref[...], preferred_element_type=F32) + bglu_ref[...]
    o_ref[...] = (y * _sigmoid(gl)).astype(o_ref.dtype)


def _s5(z, u_blk, bblk, cblk, are, aim, d_skip, wglu, bglu, tm):
    L = z.shape[0]
    const = lambda *shape: pl.BlockSpec(shape, lambda i: (0,) * len(shape))
    return pl.pallas_call(
        _s5_kernel,
        name="s5_branch",
        grid=(L // tm,),
        in_specs=[
            pl.BlockSpec((tm, SSM_WIDTH), lambda i: (i, u_blk)),
            const(N_UBLK, LANES, 2 * ST_PER_UBLK),
            const(N_UBLK, 2 * ST_PER_UBLK, LANES),
            const(N_SVREG, SUBLANES, LANES),
            const(N_SVREG, SUBLANES, LANES),
            const(1, SSM_WIDTH),
            const(SSM_WIDTH, SSM_WIDTH),
            const(1, SSM_WIDTH),
        ],
        out_specs=pl.BlockSpec((tm, SSM_WIDTH), lambda i: (i, 0)),
        out_shape=jax.ShapeDtypeStruct((L, SSM_WIDTH), BF16),
        scratch_shapes=[
            pltpu.VMEM((N_SVREG, tm * SUBLANES, LANES), F32),
            pltpu.VMEM((N_SVREG, tm * SUBLANES, LANES), F32),
            pltpu.VMEM((2, N_SVREG, SUBLANES, LANES), F32),
        ],
        compiler_params=_cparams(("arbitrary",), 48),
    )(z, bblk, cblk, are, aim, d_skip, wglu, bglu)


def _ret_kernel(q_ref, k_ref, v_ref, g_ref, pos_ref, inv_ref, decay_ref, zeta_ref, xi_ref, cd_ref,
                o_ref, r_scr):
    tm = q_ref.shape[0]
    C, dk = RET_CHUNK, RET_HEAD_DIM

    @pl.when(pl.program_id(0) == 0)
    def _():
        r_scr[...] = jnp.zeros_like(r_scr)

    ang = pos_ref[...].astype(F32) * inv_ref[...]
    cs, sn = jnp.cos(ang), jnp.sin(ang)
    cf = jnp.concatenate([cs, cs], axis=-1)
    sf = jnp.concatenate([-sn, sn], axis=-1)

    def rot(xh):
        return xh * cf + pltpu.roll(xh, dk // 2, axis=1) * sf

    nt = (((1,), (1,)), ((), ()))
    for h in range(RET_HEADS):
        cols = slice(h * dk, (h + 1) * dk)
        qr = rot(q_ref[:, cols].astype(F32))
        kr = rot(k_ref[:, cols].astype(F32))
        for n in range(tm // C):
            rows = slice(n * C, (n + 1) * C)
            qc = qr[rows].astype(BF16)
            kcf = kr[rows]
            vc = v_ref[rows, cols]
            sc = lax.dot_general(qc, kcf.astype(BF16), nt, preferred_element_type=F32) * decay_ref[h]
            inner = jnp.dot(sc.astype(BF16), vc, preferred_element_type=F32)
            r_prev = r_scr[h]
            cross = jnp.dot(qc, r_prev.astype(BF16), preferred_element_type=F32) * xi_ref[h]
            o = inner + cross
            kzt = jnp.transpose(kcf * zeta_ref[h]).astype(BF16)
            r_scr[h] = r_prev * cd_ref[h] + jnp.dot(kzt, vc, preferred_element_type=F32)
            mu = jnp.mean(o, axis=-1, keepdims=True)
            oc = o - mu
            on = oc * lax.rsqrt(jnp.mean(oc * oc, axis=-1, keepdims=True) + NORM_EPS)
            gate = g_ref[rows, cols].astype(F32)
            o_ref[rows, cols] = (gate * _sigmoid(gate) * on).astype(o_ref.dtype)


def _retention(z, blks, pos, tm):
    L = z.shape[0]
    H, C, dk = RET_HEADS, RET_CHUNK, RET_HEAD_DIM
    half = dk // 2
    inv = (ROPE_BASE ** (-jnp.arange(half, dtype=F32) / half))[None, :]
    log_gamma = jnp.log(1.0 - jnp.exp2(-5.0 - jnp.arange(H, dtype=F32)))
    idx = jnp.arange(C, dtype=F32)
    rel = idx[:, None] - idx[None, :]
    scale = dk ** -0.5
    decay = jnp.where(rel >= 0, jnp.exp(jnp.maximum(rel, 0.0)[None] * log_gamma[:, None, None]), 0.0) * scale
    zeta = jnp.exp((C - 1.0 - idx)[None, :] * log_gamma[:, None]) * scale
    xi = jnp.exp((idx + 1.0)[None, :] * log_gamma[:, None])
    bc = lambda a: jnp.broadcast_to(a[:, :, None], (H, C, dk))
    cd = jnp.broadcast_to(jnp.exp(C * log_gamma)[:, None, None], (H, 1, dk))
    const = lambda *shape: pl.BlockSpec(shape, lambda i: (0,) * len(shape))
    qb, kb, vb, gb = blks
    return pl.pallas_call(
        _ret_kernel,
        name="retention",
        grid=(L // tm,),
        in_specs=[
            pl.BlockSpec((tm, RET_WIDTH), lambda i: (i, qb)),
            pl.BlockSpec((tm, RET_WIDTH), lambda i: (i, kb)),
            pl.BlockSpec((tm, RET_WIDTH), lambda i: (i, vb)),
            pl.BlockSpec((tm, RET_WIDTH), lambda i: (i, gb)),
            pl.BlockSpec((tm, 1), lambda i: (i, 0)),
            const(1, half),
            const(H, C, C),
            const(H, C, dk),
            const(H, C, dk),
            const(H, 1, dk),
        ],
        out_specs=pl.BlockSpec((tm, RET_WIDTH), lambda i: (i, 0)),
        out_shape=jax.ShapeDtypeStruct((L, RET_WIDTH), BF16),
        scratch_shapes=[pltpu.VMEM((H, dk, dk), F32)],
        compiler_params=_cparams(("arbitrary",), 48),
    )(z, z, z, z, pos, inv, decay, bc(zeta), bc(xi), cd)


def _merge_kernel(ys_ref, yr_ref, ms_ref, mr_ref, x_ref, ps_ref, pr_ref, wo_ref, g2_ref,
                  rwh_ref, rwl_ref, rb_ref, tril_ref,
                  x1_ref, h2_ref, idx_ref, gate_ref, rank_ref, cnt_ref, carry):
    tm = x_ref.shape[0]

    @pl.when(pl.program_id(0) == 0)
    def _():
        carry[...] = jnp.zeros_like(carry)

    a = jnp.dot(ys_ref[...], ps_ref[...], preferred_element_type=F32)
    b = jnp.dot(yr_ref[...], pr_ref[...], preferred_element_type=F32)
    merged = _sigmoid(ms_ref[...].astype(F32)) * a + _sigmoid(mr_ref[...].astype(F32)) * b
    x1 = x_ref[...] + jnp.dot(merged.astype(BF16), wo_ref[...], preferred_element_type=F32)
    x1_ref[...] = x1
    h2 = _rms(x1, g2_ref[...])
    h2_ref[...] = h2

    hh = h2.astype(BF16)
    hl = (h2 - hh.astype(F32)).astype(BF16)
    logits = (jnp.dot(hh, rwh_ref[...], preferred_element_type=F32)
              + jnp.dot(hh, rwl_ref[...], preferred_element_type=F32)
              + jnp.dot(hl, rwh_ref[...], preferred_element_type=F32)) + rb_ref[...]

    lane = lax.broadcasted_iota(jnp.int32, (tm, ROUTER_PAD), 1)
    work = logits
    vals, idxs = [], []
    for _ in range(TOP_K):
        m = jnp.max(work, axis=-1, keepdims=True)
        ix = jnp.min(jnp.where(work == m, lane, ROUTER_PAD), axis=-1, keepdims=True)
        vals.append(m)
        idxs.append(ix)
        work = jnp.where(lane == ix, -jnp.inf, work)
    es = [jnp.exp(v - vals[0]) for v in vals]
    den = es[0] + es[1] + es[2] + es[3]

    onehot = jnp.zeros((tm, ROUTER_PAD), F32)
    for ix in idxs:
        onehot = onehot + (lane == ix).astype(F32)
    cum = jnp.dot(tril_ref[...], onehot.astype(BF16), preferred_element_type=F32) + carry[...]
    carry[...] = carry[...] + jnp.sum(onehot, axis=0, keepdims=True)
    cnt_ref[...] = jnp.broadcast_to(carry[...], cnt_ref.shape).astype(jnp.int32)

    idx_o = jnp.zeros((tm, ROUTER_PAD), jnp.int32)
    gate_o = jnp.zeros((tm, ROUTER_PAD), F32)
    rank_o = jnp.zeros((tm, ROUTER_PAD), jnp.int32)
    for k in range(TOP_K):
        rk = jnp.sum(jnp.where(lane == idxs[k], cum, 0.0), axis=-1, keepdims=True).astype(jnp.int32)
        idx_o = jnp.where(lane == k, idxs[k], idx_o)
        gate_o = jnp.where(lane == k, es[k] / den, gate_o)
        rank_o = jnp.where(lane == k, rk, rank_o)
    idx_ref[...] = idx_o
    gate_ref[...] = gate_o
    rank_ref[...] = rank_o


def _merge(ys, yr, z, ms_blk, mr_blk, x2, ps, pr, wo, g2, rwh, rwl, rb, tm):
    L = x2.shape[0]
    tril = (jnp.arange(tm)[:, None] > jnp.arange(tm)[None, :]).astype(BF16)
    const = lambda *shape: pl.BlockSpec(shape, lambda i: (0,) * len(shape), pipeline_mode=pl.Buffered(1))
    row = lambda w: pl.BlockSpec((tm, w), lambda i: (i, 0))
    return pl.pallas_call(
        _merge_kernel,
        name="merge_router",
        grid=(L // tm,),
        in_specs=[
            row(SSM_WIDTH), row(RET_WIDTH),
            pl.BlockSpec((tm, D_MODEL), lambda i: (i, ms_blk)),
            pl.BlockSpec((tm, D_MODEL), lambda i: (i, mr_blk)),
            row(D_MODEL),
            const(SSM_WIDTH, D_MODEL), const(RET_WIDTH, D_MODEL), const(D_MODEL, D_MODEL),
            const(1, D_MODEL),
            const(D_MODEL, ROUTER_PAD), const(D_MODEL, ROUTER_PAD), const(1, ROUTER_PAD),
            const(tm, tm),
        ],
        out_specs=[
            row(D_MODEL), row(D_MODEL),
            row(ROUTER_PAD), row(ROUTER_PAD), row(ROUTER_PAD),
            pl.BlockSpec((SUBLANES, ROUTER_PAD), lambda i: (0, 0)),
        ],
        out_shape=[
            jax.ShapeDtypeStruct((L, D_MODEL), F32),
            jax.ShapeDtypeStruct((L, D_MODEL), F32),
            jax.ShapeDtypeStruct((L, ROUTER_PAD), jnp.int32),
            jax.ShapeDtypeStruct((L, ROUTER_PAD), F32),
            jax.ShapeDtypeStruct((L, ROUTER_PAD), jnp.int32),
            jax.ShapeDtypeStruct((SUBLANES, ROUTER_PAD), jnp.int32),
        ],
        scratch_shapes=[pltpu.VMEM((1, ROUTER_PAD), F32)],
        compiler_params=_cparams(("arbitrary",), 56),
    )(ys, yr, z, z, x2, ps, pr, wo, g2, rwh, rwl, rb, tril)


def _row_tokens_kernel(dest_ref, out_ref):
    def clear(i, c):
        out_ref[i] = 0
        return c

    def place(t, c):
        for k in range(TOP_K):
            out_ref[dest_ref[t * TOP_K + k]] = t
        return c

    lax.fori_loop(0, out_ref.shape[0], clear, 0, unroll=8)
    lax.fori_loop(0, dest_ref.shape[0] // TOP_K, place, 0, unroll=4)


def _row_tokens(dest, n_rows):
    smem = pl.BlockSpec(memory_space=pltpu.SMEM)
    return pl.pallas_call(
        _row_tokens_kernel, name="row_tokens", in_specs=[smem], out_specs=smem,
        out_shape=jax.ShapeDtypeStruct((n_rows,), jnp.int32))(dest)


SUB_ROWS = 128
ITEM_SUBS = 9
BLOCK_SUBS = 8
N_F = 4
SUB_GROUPS = SUB_ROWS // SUBLANES
ACC_COLS = 512


def _expert_kernel(item_e_ref, item_g0_ref, item_nc_ref, item_no_ref, rowtok_ref,
                   h2_hbm, wgu_ref, wd_ref, bgu_ref, bd_ref, unpack_ref, yb_hbm,
                   xbuf, xb, acc, wgu_bf, wd_bf, gsem, osem):
    w = pl.program_id(0)
    f = pl.program_id(1)
    n_w = pl.num_programs(0)
    n_f = pl.num_programs(1)
    tf = wd_ref.shape[1]
    nc = item_nc_ref[w]
    no = item_no_ref[w]

    def groups(s):
        return pl.ds(s * SUB_GROUPS, SUB_GROUPS)

    def gather_sub(item, s):
        base = item_g0_ref[item] * SUBLANES + s * SUB_ROWS
        _gather_rows(h2_hbm, xbuf, gsem.at[0], lambda r: rowtok_ref[base + r], SUB_GROUPS, s * SUB_GROUPS)

    def gather_wait(s):
        pltpu.make_async_copy(xbuf.at[groups(s)], xbuf.at[groups(s)], gsem.at[0]).wait()

    def out_copy(s, g0):
        row0 = pl.multiple_of((g0 + s * SUB_GROUPS) * SUBLANES, SUBLANES)
        src0 = s * SUB_ROWS if isinstance(s, int) else pl.multiple_of(s * SUB_ROWS, SUB_ROWS)
        return pltpu.make_async_copy(acc.at[pl.ds(src0, SUB_ROWS), :],
                                     yb_hbm.at[pl.ds(row0, SUB_ROWS), :], osem.at[0])

    def for_subs(count, fn):
        for s in range(ITEM_SUBS):
            pl.when(s < count)(functools.partial(fn, s))

    @pl.when(f == 0)
    def _():
        @pl.when(w == 0)
        def _():
            xbuf[...] = jnp.zeros_like(xbuf)
            for_subs(nc, lambda s: gather_sub(0, s))

        for_subs(nc, gather_wait)
        xb[...] = xbuf[...].reshape(xb.shape).astype(BF16)

        @pl.when(w > 0)
        def _():
            for_subs(item_no_ref[jnp.maximum(w - 1, 0)], lambda s: out_copy(s, 0).wait())

        acc[...] = jnp.broadcast_to(bd_ref[0] * (nc > 0).astype(F32), acc.shape)

    nxt_item = jnp.minimum(w + 1, n_w - 1)
    nxt_nc = jnp.where(w + 1 < n_w, item_nc_ref[nxt_item], 0)
    last = f == n_f - 1

    per_step = BLOCK_SUBS // N_F
    for i in range(per_step):
        s_dyn = f * per_step + i
        pl.when(s_dyn < nxt_nc)(functools.partial(gather_sub, nxt_item, s_dyn))

    for s in range(BLOCK_SUBS, ITEM_SUBS):
        pl.when(last & (s < nxt_nc))(functools.partial(gather_sub, nxt_item, s))

    def act(guc):
        g = jnp.minimum(guc, SWIGLU_LIMIT)
        u = jnp.clip(pltpu.roll(guc, LANES - 1, axis=1), -SWIGLU_LIMIT, SWIGLU_LIMIT)
        return g * _sigmoid(SWIGLU_ALPHA * g) * (u + 1.0)

    def rows_block(r0, m, cast):
        rows = pl.ds(r0, m) if isinstance(r0, int) else pl.ds(pl.multiple_of(r0, SUB_ROWS), m)
        x = xb[rows, :]
        n_k = tf // MXU_DIM
        even = (lax.broadcasted_iota(jnp.int32, (m, LANES), 1) & 1) == 0

        def gate_up(k):
            outs = []
            for n in range(2):
                cols = slice((2 * k + n) * MXU_DIM, (2 * k + n + 1) * MXU_DIM)
                if cast:
                    wt = wgu_ref[0, :, cols].astype(BF16)
                    wgu_bf[:, cols] = wt
                else:
                    wt = wgu_bf[:, cols]
                outs.append(jnp.dot(x, wt, preferred_element_type=F32) + bgu_ref[0, :, cols])
            return outs

        def activate(gus):
            parts = [jnp.where(even, act(gu[:, :LANES]), pltpu.roll(act(gu[:, LANES:]), 1, axis=1))
                     for gu in gus]
            packed = jnp.concatenate(parts, axis=-1).astype(BF16)
            return jnp.dot(packed, unpack_ref[...], preferred_element_type=F32).astype(BF16)

        gus = gate_up(0)
        for k in range(n_k):
            nxt = gate_up(k + 1) if k + 1 < n_k else None
            a = activate(gus)
            wrows = slice(k * MXU_DIM, (k + 1) * MXU_DIM)
            if cast:
                wdt = wd_ref[0, wrows, :].astype(BF16)
                wd_bf[wrows, :] = wdt
            else:
                wdt = wd_bf[wrows, :]
            for c in range(D_MODEL // ACC_COLS):
                cs = slice(c * ACC_COLS, (c + 1) * ACC_COLS)
                acc[rows, cs] += jnp.dot(a, wdt[:, cs], preferred_element_type=F32)
            gus = nxt

    g0 = item_g0_ref[w]
    block_rows = BLOCK_SUBS * SUB_ROWS
    big = nc >= BLOCK_SUBS

    def extra_sub(s, start_out):
        rows_block(s * SUB_ROWS, SUB_ROWS, False)
        if start_out:
            out_copy(s, g0).start()

    @pl.when(big & jnp.logical_not(last))
    def _():
        rows_block(0, block_rows, True)
        for s in range(BLOCK_SUBS, ITEM_SUBS):
            pl.when(s < nc)(functools.partial(extra_sub, s, False))

    @pl.when(big & last)
    def _():
        half = BLOCK_SUBS // 2
        rows_block(0, block_rows // 2, True)
        for s in range(half):
            out_copy(s, g0).start()
        rows_block(block_rows // 2, block_rows // 2, False)
        for s in range(half, BLOCK_SUBS):
            out_copy(s, g0).start()
        for s in range(BLOCK_SUBS, ITEM_SUBS):
            pl.when(s < nc)(functools.partial(extra_sub, s, True))

    @pl.when((nc > 0) & jnp.logical_not(big))
    def _():
        def chunk(s0, n_sub, cast):
            rows_block(s0 * SUB_ROWS, n_sub * SUB_ROWS, cast)

            def start_outs():
                for i in range(n_sub):
                    out_copy(s0 + i, g0).start()

            pl.when(last)(start_outs)

        sizes = []
        n_sub = BLOCK_SUBS // 2
        while n_sub >= 1:
            sizes.append(n_sub)
            n_sub //= 2
        for i, n_sub in enumerate(sizes):
            has = (nc & n_sub) != 0
            larger = nc & ~(2 * n_sub - 1)
            pl.when(has & (larger == 0))(functools.partial(chunk, 0, n_sub, True))
            if i > 0:
                pl.when(has & (larger != 0))(functools.partial(chunk, larger, n_sub, False))

    @pl.when((nc == 0) & last)
    def _():
        for_subs(no, lambda s: out_copy(s, g0).start())

    @pl.when(last & (w == n_w - 1))
    def _():
        for_subs(no, lambda s: out_copy(s, 0).wait())


def _unpack_matrix():
    lane = jnp.arange(MXU_DIM)
    n, r = lane // LANES, lane % LANES
    natural = n * LANES + (r % 2) * (LANES // 2) + r // 2
    return (natural[:, None] == jnp.arange(MXU_DIM)[None, :]).astype(BF16)


def _expert_items(counts, n_asg):
    E = N_EXPERTS
    item_rows = SUB_ROWS * ITEM_SUBS
    n_items_max = (n_asg // SUB_ROWS + E) // ITEM_SUBS + E + 1
    n_rows = n_asg + E * SUB_ROWS
    pcounts = (counts + SUB_ROWS - 1) // SUB_ROWS * SUB_ROWS
    pend = jnp.cumsum(pcounts)
    pstart = pend - pcounts
    nsb_e = pcounts // SUB_ROWS
    n_it = (nsb_e + ITEM_SUBS - 1) // ITEM_SUBS
    extra_e = jnp.maximum(nsb_e - BLOCK_SUBS * n_it, 0)
    it_end = jnp.cumsum(n_it)
    it_start = it_end - n_it
    n_items = it_end[-1]
    wv = jnp.arange(n_items_max, dtype=jnp.int32)
    e_w = jnp.minimum(jnp.sum(wv[:, None] >= it_end[None, :], axis=1), E - 1)
    j_w = wv - it_start[e_w]
    first_sub = lambda j: BLOCK_SUBS * j + jnp.minimum(j, extra_e[e_w])
    sb0 = first_sub(j_w)
    row0 = pstart[e_w] + sb0 * SUB_ROWS
    nsb = jnp.clip(jnp.minimum(nsb_e[e_w], first_sub(j_w + 1)) - sb0, 0, ITEM_SUBS)
    used = wv < n_items
    tail0 = pend[-1] + (wv - n_items) * item_rows
    nz = jnp.clip((n_rows - tail0) // SUB_ROWS, 0, ITEM_SUBS)
    item_row0 = jnp.clip(jnp.where(used, row0, tail0), 0, n_rows - SUB_ROWS)
    item_e = jnp.where(used, e_w, e_w[jnp.maximum(n_items - 1, 0)])
    i32 = lambda a: a.astype(jnp.int32)
    return (i32(item_e), i32(item_row0 // SUBLANES), i32(jnp.where(used, nsb, 0)),
            i32(jnp.where(used, nsb, nz)), pstart, n_rows)


def _experts(item_e, item_g0, item_nc, item_no, row_tok, h2, wgu, wd, bgu, bd, tf):
    n_rows = row_tok.shape[0]
    n_items = item_e.shape[0]
    n_f = EXPERT_FF // tf
    item_rows = SUB_ROWS * ITEM_SUBS
    assert tf % MXU_DIM == 0 and n_f == N_F and N_F <= BLOCK_SUBS <= ITEM_SUBS

    def f_eff(w, f, nc):
        return jnp.where(nc[w] > 0, f, n_f - 1)

    grid_spec = pltpu.PrefetchScalarGridSpec(
        num_scalar_prefetch=5,
        grid=(n_items, n_f),
        in_specs=[
            pl.BlockSpec(memory_space=pl.ANY),
            pl.BlockSpec((1, D_MODEL, 2 * tf), lambda w, f, ie, g0, nc, no, rt: (ie[w], 0, f_eff(w, f, nc))),
            pl.BlockSpec((1, tf, D_MODEL), lambda w, f, ie, g0, nc, no, rt: (ie[w], f_eff(w, f, nc), 0)),
            pl.BlockSpec((1, 1, 2 * tf), lambda w, f, ie, g0, nc, no, rt: (ie[w], 0, f_eff(w, f, nc))),
            pl.BlockSpec((1, 1, D_MODEL), lambda w, f, ie, g0, nc, no, rt: (ie[w], 0, 0)),
            pl.BlockSpec((MXU_DIM, MXU_DIM), lambda w, f, ie, g0, nc, no, rt: (0, 0)),
        ],
        out_specs=pl.BlockSpec(memory_space=pl.ANY),
        scratch_shapes=[
            pltpu.VMEM((item_rows // SUBLANES, SUBLANES, D_MODEL), F32),
            pltpu.VMEM((item_rows, D_MODEL), BF16),
            pltpu.VMEM((item_rows, D_MODEL), F32),
            pltpu.VMEM((D_MODEL, 2 * tf), BF16),
            pltpu.VMEM((tf, D_MODEL), BF16),
            pltpu.SemaphoreType.DMA((1,)),
            pltpu.SemaphoreType.DMA((1,)),
        ],
    )
    return pl.pallas_call(
        _expert_kernel,
        name="experts",
        grid_spec=grid_spec,
        out_shape=jax.ShapeDtypeStruct((n_rows, D_MODEL), F32),
        compiler_params=_cparams(("arbitrary", "arbitrary"), 60),
    )(item_e, item_g0, item_nc, item_no, row_tok, h2, wgu, wd, bgu, bd, _unpack_matrix())


def _combine_kernel(dest_ref, yb_hbm, gate_ref, x1_ref, p_ref, wpg_ref, wpp_ref, g3_ref, gf_ref,
                    o_ref, gbuf, sem):
    i = pl.program_id(0)
    n = pl.num_programs(0)
    tm = x1_ref.shape[0]

    def issue(tile):
        base = tile * tm * TOP_K

        for k in range(TOP_K):
            _gather_rows(yb_hbm, gbuf.at[k], sem.at[0], lambda r, k=k: dest_ref[base + r * TOP_K + k],
                         tm // SUBLANES, 0)

    @pl.when(i == 0)
    def _():
        issue(0)

    for k in range(TOP_K):
        pltpu.make_async_copy(gbuf.at[k], gbuf.at[k], sem.at[0]).wait()
    gates = gate_ref[...]
    moe = gates[:, 0:1] * gbuf[0].reshape(tm, D_MODEL)
    for k in range(1, TOP_K):
        moe = moe + gates[:, k:k + 1] * gbuf[k].reshape(tm, D_MODEL)
    x2 = x1_ref[...] + moe

    @pl.when(i + 1 < n)
    def _():
        issue(i + 1)

    h3 = _rms(x2, g3_ref[...]).astype(BF16)
    pg = _sigmoid(jnp.dot(h3, wpg_ref[...], preferred_element_type=F32))
    pp = jnp.dot(p_ref[...].astype(BF16), wpp_ref[...], preferred_element_type=F32)
    x3 = x2 + pg * pp
    o_ref[...] = _rms(x3, gf_ref[...])


def _combine(dest, yb, gates, x1, p2, wpg, wpp, g3, gf, tm):
    L = x1.shape[0]
    const = lambda *shape: pl.BlockSpec(shape, lambda i, d: (0,) * len(shape), pipeline_mode=pl.Buffered(1))
    row = lambda w: pl.BlockSpec((tm, w), lambda i, d: (i, 0))
    grid_spec = pltpu.PrefetchScalarGridSpec(
        num_scalar_prefetch=1,
        grid=(L // tm,),
        in_specs=[
            pl.BlockSpec(memory_space=pl.ANY),
            row(ROUTER_PAD), row(D_MODEL), row(PLE_DIM),
            const(D_MODEL, D_MODEL), const(PLE_DIM, D_MODEL), const(1, D_MODEL), const(1, D_MODEL),
        ],
        out_specs=row(D_MODEL),
        scratch_shapes=[
            pltpu.VMEM((TOP_K, tm // SUBLANES, SUBLANES, D_MODEL), F32),
            pltpu.SemaphoreType.DMA((1,)),
        ],
    )
    return pl.pallas_call(
        _combine_kernel,
        name="combine_ple",
        grid_spec=grid_spec,
        out_shape=jax.ShapeDtypeStruct((L, D_MODEL), F32),
        compiler_params=_cparams(("arbitrary",), 48),
    )(dest, yb, gates, x1, p2, wpg, wpp, g3, gf)


def _tile(L, pref):
    return min(pref, L)


def _layer(x2, p2, pos, norm1_g, w_in, lam_re, lam_im, log_dt, b_re, b_im, c_re, c_im, d_skip, w_glu,
           b_glu, w_bs, w_br, w_out, norm2_g, router_w, router_b, w_gu, b_gu, w_dn, b_dn, norm3_g,
           w_pg, w_pp, out_g):
    L = x2.shape[0]
    G, N, E = SSM_GROUPS, SSM_STATE, N_EXPERTS

    cuts = SSM_WIDTH + 4 * RET_WIDTH
    tn = 1024
    z = _inproj(x2, norm1_g[None, :], w_in, cuts // tn, _tile(L, 1024), tn)
    ms_blk, mr_blk = 0, 1
    u_blk, q_blk, k_blk, v_blk, g_blk = 4, 5, 6, 7, 8

    ab_re, ab_im, bb_re, bb_im = _s5_prep(lam_re, lam_im, log_dt, b_re, b_im)
    eye = jnp.eye(SUBLANES, dtype=F32)
    gpb = LANES // SSM_GROUP
    def bpack(bb):
        t = bb.reshape(N_UBLK, gpb, SSM_GROUP, N)
        return jnp.einsum('bgcn,gh->bgchn', t, eye).reshape(N_UBLK, LANES, ST_PER_UBLK)
    def cpack(c):
        t = c.reshape(N_UBLK, gpb, SSM_GROUP, N)
        return jnp.einsum('bgcn,gh->bgnhc', t, eye).reshape(N_UBLK, ST_PER_UBLK, LANES)
    bblk = jnp.concatenate([bpack(bb_re), bpack(bb_im)], axis=-1).astype(BF16)
    cblk = jnp.concatenate([cpack(c_re), -cpack(c_im)], axis=1).astype(BF16)
    atile = lambda a: a[::SSM_GROUP].reshape(N_SVREG, SUBLANES, LANES)
    ys = _s5(z, u_blk, bblk, cblk, atile(ab_re), atile(ab_im), d_skip.reshape(1, SSM_WIDTH),
             w_glu.astype(BF16), b_glu[None, :], _tile(L, 512))

    yr = _retention(z, (q_blk, k_blk, v_blk, g_blk), pos, _tile(L, 512))

    rw = jnp.pad(router_w, ((0, 0), (0, ROUTER_PAD - E)))
    rwh = rw.astype(BF16)
    rwl = (rw - rwh.astype(F32)).astype(BF16)
    rb = jnp.pad(router_b, (0, ROUTER_PAD - E), constant_values=NEG_BIG)[None, :]
    x1, h2, idx, gates, rank, cnt = _merge(
        ys, yr, z, ms_blk, mr_blk, x2, w_bs.astype(BF16), w_br.astype(BF16), w_out.astype(BF16),
        norm2_g[None, :], rwh, rwl, rb, _tile(L, 256))

    item_e, item_g0, item_nc, item_no, pstart, n_rows = _expert_items(cnt[0, :E], L * TOP_K)
    idx4, rank4 = idx[:, :TOP_K], rank[:, :TOP_K]
    dest = (pstart[idx4] + rank4).astype(jnp.int32).reshape(-1)
    row_tok = _row_tokens(dest, n_rows)

    yb = _experts(item_e, item_g0, item_nc, item_no, row_tok, h2, w_gu, w_dn, b_gu[:, None, :],
                  b_dn[:, None, :], EXPERT_FF // N_F)

    return _combine(dest, yb, gates, x1, p2, w_pg.astype(BF16), w_pp.astype(BF16),
                    norm3_g[None, :], out_g[None, :], _tile(L, 256))


def kernel(x, p, positions, norm1_g, w_in, ssm_lam_re, ssm_lam_im, ssm_log_dt, ssm_b_re, ssm_b_im, ssm_c_re, ssm_c_im, ssm_d, ssm_w_glu, ssm_b_glu, w_branch_ssm, w_branch_ret, w_out, norm2_g, router_w, router_b, exp_w_gate_up, exp_b_gate_up, exp_w_down, exp_b_down, norm3_g, ple_w_gate, ple_w_proj, final_norm_g):
    bsz, L, d = x.shape
    depth = w_in.shape[0]
    assert bsz == 1 and depth == 1 and d == D_MODEL
    out = _layer(
        x[0], p[0, 0], positions.reshape(L, 1), norm1_g[0], w_in[0], ssm_lam_re[0], ssm_lam_im[0],
        ssm_log_dt[0], ssm_b_re[0], ssm_b_im[0], ssm_c_re[0], ssm_c_im[0], ssm_d[0], ssm_w_glu[0],
        ssm_b_glu[0], w_branch_ssm[0], w_branch_ret[0], w_out[0], norm2_g[0], router_w[0], router_b[0],
        exp_w_gate_up[0], exp_b_gate_up[0], exp_w_down[0], exp_b_down[0], norm3_g[0], ple_w_gate[0],
        ple_w_proj[0], final_norm_g)
    return out[None]
```

```python
import functools
import math

import jax
import jax.numpy as jnp
from jax import lax
from jax.experimental import pallas as pl
from jax.experimental.pallas import tpu as pltpu

F32 = jnp.float32
BF16 = jnp.bfloat16

D_MODEL = 2048
PLE_DIM = 256
SSM_WIDTH = D_MODEL // 2
SSM_GROUP = 16
SSM_GROUPS = SSM_WIDTH // SSM_GROUP
SSM_STATE = 64
RET_HEADS = 8
RET_HEAD_DIM = 128
RET_WIDTH = RET_HEADS * RET_HEAD_DIM
RET_CHUNK = 128
ROPE_BASE = 10000.0
N_EXPERTS = 32
TOP_K = 4
EXPERT_FF = D_MODEL
SWIGLU_LIMIT = 7.0
SWIGLU_ALPHA = 1.702
NORM_EPS = 1e-6
IN_COLS = SSM_WIDTH + 4 * RET_WIDTH + 2 * D_MODEL

LANES = 128
SUBLANES = 8
MXU_DIM = 256
ROUTER_PAD = LANES
NEG_BIG = -1e30

MIB = 1024 * 1024


def _cparams(sem, vmem_mib):
    return pltpu.CompilerParams(dimension_semantics=sem, vmem_limit_bytes=vmem_mib * MIB)


def _rms(xf, g):
    ms = jnp.mean(xf * xf, axis=-1, keepdims=True)
    return xf * lax.rsqrt(ms + NORM_EPS) * g


def _sigmoid(x):
    return 1.0 / (1.0 + jnp.exp(-x))


GATHER_UNROLL = 4


def _gather_rows(src2, dst3, sem, index_of, n_groups, dst_group0, inline=False, first_group=0,
                 both_queues=False):
    def body(j, c):
        for u in range(SUBLANES):
            row = index_of(j * SUBLANES + u)
            pltpu.make_async_copy(src2.at[pl.ds(row, 1), :],
                                  dst3.at[dst_group0 + j, pl.ds(u, 1), :], sem).start(
                                      priority=u % 2 if both_queues else 0)
        return c

    if inline:
        for j in range(first_group, first_group + n_groups):
            body(j, 0)
    else:
        lax.fori_loop(first_group, first_group + n_groups, body, 0, unroll=GATHER_UNROLL)


def _inproj_kernel(x_ref, g_ref, w_ref, o_ref, h_scr):
    @pl.when(pl.program_id(1) == 0)
    def _():
        h_scr[...] = _rms(x_ref[...], g_ref[...]).astype(BF16)

    w = w_ref[...].astype(BF16)
    o_ref[...] = jnp.dot(h_scr[...], w, preferred_element_type=F32).astype(o_ref.dtype)


def _inproj(x2, g1, w_in, first_block, tm, tn):
    L = x2.shape[0]
    n_cols = w_in.shape[1]
    n_j = n_cols // tn
    return pl.pallas_call(
        _inproj_kernel,
        name="inproj",
        grid=(L // tm, n_j),
        in_specs=[
            pl.BlockSpec((tm, D_MODEL), lambda i, j: (i, 0)),
            pl.BlockSpec((1, D_MODEL), lambda i, j: (0, 0)),
            pl.BlockSpec((D_MODEL, tn), lambda i, j: (0, (j + first_block) % n_j)),
        ],
        out_specs=pl.BlockSpec((tm, tn), lambda i, j: (i, j)),
        out_shape=jax.ShapeDtypeStruct((L, n_cols), BF16),
        scratch_shapes=[pltpu.VMEM((tm, D_MODEL), BF16)],
        compiler_params=_cparams(("arbitrary", "arbitrary"), 52),
    )(x2, g1, w_in)


def _s5_prep_kernel(lr_ref, li_ref, ldt_ref, bre_ref, bim_ref, abre_ref, abim_ref, bbre_ref, bbim_ref):
    lr, li = lr_ref[...], li_ref[...]
    dt = jnp.exp(ldt_ref[...])
    mag = jnp.exp(lr * dt)
    ab_re, ab_im = mag * jnp.cos(li * dt), mag * jnp.sin(li * dt)
    den = lr * lr + li * li
    nr, ni = ab_re - 1.0, ab_im
    f_re = (nr * lr + ni * li) / den
    f_im = (ni * lr - nr * li) / den
    bre, bim = bre_ref[...], bim_ref[...]
    abre_ref[...] = ab_re
    abim_ref[...] = ab_im
    bbre_ref[...] = f_re * bre - f_im * bim
    bbim_ref[...] = f_re * bim + f_im * bre


def _s5_prep(lam_re, lam_im, log_dt, b_re, b_im):
    rep = lambda a: jnp.repeat(a, SSM_GROUP, axis=0)
    bt = lambda b: jnp.transpose(b, (0, 2, 1)).reshape(SSM_WIDTH, SSM_STATE)
    shp = jax.ShapeDtypeStruct((SSM_WIDTH, SSM_STATE), F32)
    return pl.pallas_call(_s5_prep_kernel, name="s5_prep", out_shape=(shp, shp, shp, shp))(
        rep(lam_re), rep(lam_im), rep(log_dt[:, None]), bt(b_re), bt(b_im))


def _gelu_tanh(x):
    return 0.5 * x * (1.0 + jnp.tanh(math.sqrt(2.0 / math.pi) * (x + 0.044715 * (x * x * x))))


N_UBLK = SSM_WIDTH // LANES
ST_PER_UBLK = (LANES // SSM_GROUP) * SSM_STATE
TILES_PER_UBLK = ST_PER_UBLK // LANES
N_SVREG = SSM_GROUPS * SSM_STATE // (LANES * SUBLANES)


def _s5_kernel(u_ref, bblk_ref, cblk_ref, are_ref, aim_ref, d_ref, wglu_ref, bglu_ref, o_ref,
               sre, sim, carry):
    tm = u_ref.shape[0]

    @pl.when(pl.program_id(0) == 0)
    def _():
        carry[...] = jnp.zeros_like(carry)

    u = u_ref[...]
    for b in range(N_UBLK):
        bu = jnp.dot(u[:, b * LANES:(b + 1) * LANES], bblk_ref[b], preferred_element_type=F32)
        v = (b * TILES_PER_UBLK) // SUBLANES
        for q in range(TILES_PER_UBLK):
            k = (b * TILES_PER_UBLK + q) % SUBLANES
            sre[v, pl.ds(k, tm, stride=SUBLANES), :] = bu[:, q * LANES:(q + 1) * LANES]
            sim[v, pl.ds(k, tm, stride=SUBLANES), :] = bu[:, ST_PER_UBLK + q * LANES:ST_PER_UBLK + (q + 1) * LANES]

    are, aim = are_ref[...], aim_ref[...]

    def step(t, c):
        cre, cim = c
        r0 = pl.multiple_of(t * SUBLANES, SUBLANES)
        bre = sre[:, pl.ds(r0, SUBLANES), :]
        bim = sim[:, pl.ds(r0, SUBLANES), :]
        nre = are * cre - aim * cim + bre
        nim = are * cim + aim * cre + bim
        sre[:, pl.ds(r0, SUBLANES), :] = nre
        sim[:, pl.ds(r0, SUBLANES), :] = nim
        return nre, nim

    cre, cim = lax.fori_loop(0, tm, step, (carry[0], carry[1]), unroll=4)
    carry[0] = cre
    carry[1] = cim

    ys = []
    for b in range(N_UBLK):
        v = (b * TILES_PER_UBLK) // SUBLANES
        k0 = (b * TILES_PER_UBLK) % SUBLANES
        parts = [sre[v, pl.ds(k0 + q, tm, stride=SUBLANES), :] for q in range(TILES_PER_UBLK)]
        parts += [sim[v, pl.ds(k0 + q, tm, stride=SUBLANES), :] for q in range(TILES_PER_UBLK)]
        sb = jnp.concatenate(parts, axis=-1).astype(BF16)
        ys.append(jnp.dot(sb, cblk_ref[b], preferred_element_type=F32))
    y = jnp.concatenate(ys, axis=-1) + d_ref[...] * u.astype(F32)
    y = _gelu_tanh(y)
    gl = jnp.dot(y.astype(BF16), wglu_ref[...], preferred_element_type=F32) + bglu_ref[...]
    o_ref[...] = (y * _sigmoid(gl)).astype(o_ref.dtype)


def _s5(z, u_blk, bblk, cblk, are, aim, d_skip, wglu, bglu, tm):
    L = z.shape[0]
    const = lambda *shape: pl.BlockSpec(shape, lambda i: (0,) * len(shape))
    return pl.pallas_call(
        _s5_kernel,
        name="s5_branch",
        grid=(L // tm,),
        in_specs=[
            pl.BlockSpec((tm, SSM_WIDTH), lambda i: (i, u_blk)),
            const(N_UBLK, LANES, 2 * ST_PER_UBLK),
            const(N_UBLK, 2 * ST_PER_UBLK, LANES),
            const(N_SVREG, SUBLANES, LANES),
            const(N_SVREG, SUBLANES, LANES),
            const(1, SSM_WIDTH),
            const(SSM_WIDTH, SSM_WIDTH),
            const(1, SSM_WIDTH),
        ],
        out_specs=pl.BlockSpec((tm, SSM_WIDTH), lambda i: (i, 0)),
        out_shape=jax.ShapeDtypeStruct((L, SSM_WIDTH), BF16),
        scratch_shapes=[
            pltpu.VMEM((N_SVREG, tm * SUBLANES, LANES), F32),
            pltpu.VMEM((N_SVREG, tm * SUBLANES, LANES), F32),
            pltpu.VMEM((2, N_SVREG, SUBLANES, LANES), F32),
        ],
        compiler_params=_cparams(("arbitrary",), 48),
    )(z, bblk, cblk, are, aim, d_skip, wglu, bglu)


def _ret_kernel(q_ref, k_ref, v_ref, g_ref, pos_ref, inv_ref, decay_ref, zeta_ref, xi_ref, cd_ref,
                o_ref, r_scr):
    tm = q_ref.shape[0]
    C, dk = RET_CHUNK, RET_HEAD_DIM

    @pl.when(pl.program_id(0) == 0)
    def _():
        r_scr[...] = jnp.zeros_like(r_scr)

    ang = pos_ref[...].astype(F32) * inv_ref[...]
    cs, sn = jnp.cos(ang), jnp.sin(ang)
    cf = jnp.concatenate([cs, cs], axis=-1)
    sf = jnp.concatenate([-sn, sn], axis=-1)

    def rot(xh):
        return xh * cf + pltpu.roll(xh, dk // 2, axis=1) * sf

    nt = (((1,), (1,)), ((), ()))
    for h in range(RET_HEADS):
        cols = slice(h * dk, (h + 1) * dk)
        qr = rot(q_ref[:, cols].astype(F32))
        kr = rot(k_ref[:, cols].astype(F32))
        for n in range(tm // C):
            rows = slice(n * C, (n + 1) * C)
            qc = qr[rows].astype(BF16)
            kcf = kr[rows]
            vc = v_ref[rows, cols]
            sc = lax.dot_general(qc, kcf.astype(BF16), nt, preferred_element_type=F32) * decay_ref[h]
            inner = jnp.dot(sc.astype(BF16), vc, preferred_element_type=F32)
            r_prev = r_scr[h]
            cross = jnp.dot(qc, r_prev.astype(BF16), preferred_element_type=F32) * xi_ref[h]
            o = inner + cross
            kzt = jnp.transpose(kcf * zeta_ref[h]).astype(BF16)
            r_scr[h] = r_prev * cd_ref[h] + jnp.dot(kzt, vc, preferred_element_type=F32)
            mu = jnp.mean(o, axis=-1, keepdims=True)
            oc = o - mu
            on = oc * lax.rsqrt(jnp.mean(oc * oc, axis=-1, keepdims=True) + NORM_EPS)
            gate = g_ref[rows, cols].astype(F32)
            o_ref[rows, cols] = (gate * _sigmoid(gate) * on).astype(o_ref.dtype)


def _retention(z, blks, pos, tm):
    L = z.shape[0]
    H, C, dk = RET_HEADS, RET_CHUNK, RET_HEAD_DIM
    half = dk // 2
    inv = (ROPE_BASE ** (-jnp.arange(half, dtype=F32) / half))[None, :]
    log_gamma = jnp.log(1.0 - jnp.exp2(-5.0 - jnp.arange(H, dtype=F32)))
    idx = jnp.arange(C, dtype=F32)
    rel = idx[:, None] - idx[None, :]
    scale = dk ** -0.5
    decay = jnp.where(rel >= 0, jnp.exp(jnp.maximum(rel, 0.0)[None] * log_gamma[:, None, None]), 0.0) * scale
    zeta = jnp.exp((C - 1.0 - idx)[None, :] * log_gamma[:, None]) * scale
    xi = jnp.exp((idx + 1.0)[None, :] * log_gamma[:, None])
    bc = lambda a: jnp.broadcast_to(a[:, :, None], (H, C, dk))
    cd = jnp.broadcast_to(jnp.exp(C * log_gamma)[:, None, None], (H, 1, dk))
    const = lambda *shape: pl.BlockSpec(shape, lambda i: (0,) * len(shape))
    qb, kb, vb, gb = blks
    return pl.pallas_call(
        _ret_kernel,
        name="retention",
        grid=(L // tm,),
        in_specs=[
            pl.BlockSpec((tm, RET_WIDTH), lambda i: (i, qb)),
            pl.BlockSpec((tm, RET_WIDTH), lambda i: (i, kb)),
            pl.BlockSpec((tm, RET_WIDTH), lambda i: (i, vb)),
            pl.BlockSpec((tm, RET_WIDTH), lambda i: (i, gb)),
            pl.BlockSpec((tm, 1), lambda i: (i, 0)),
            const(1, half),
            const(H, C, C),
            const(H, C, dk),
            const(H, C, dk),
            const(H, 1, dk),
        ],
        out_specs=pl.BlockSpec((tm, RET_WIDTH), lambda i: (i, 0)),
        out_shape=jax.ShapeDtypeStruct((L, RET_WIDTH), BF16),
        scratch_shapes=[pltpu.VMEM((H, dk, dk), F32)],
        compiler_params=_cparams(("arbitrary",), 48),
    )(z, z, z, z, pos, inv, decay, bc(zeta), bc(xi), cd)


def _merge_kernel(ys_ref, yr_ref, ms_ref, mr_ref, x_ref, ps_ref, pr_ref, wo_ref, g2_ref,
                  rwh_ref, rwl_ref, rb_ref, tril_ref,
                  x1_ref, h2_ref, idx_ref, gate_ref, rank_ref, cnt_ref, carry):
    tm = x_ref.shape[0]

    @pl.when(pl.program_id(0) == 0)
    def _():
        carry[...] = jnp.zeros_like(carry)

    a = jnp.dot(ys_ref[...], ps_ref[...], preferred_element_type=F32)
    b = jnp.dot(yr_ref[...], pr_ref[...], preferred_element_type=F32)
    merged = _sigmoid(ms_ref[...].astype(F32)) * a + _sigmoid(mr_ref[...].astype(F32)) * b
    x1 = x_ref[...] + jnp.dot(merged.astype(BF16), wo_ref[...], preferred_element_type=F32)
    x1_ref[...] = x1
    h2 = _rms(x1, g2_ref[...])
    h2_ref[...] = h2

    hh = h2.astype(BF16)
    hl = (h2 - hh.astype(F32)).astype(BF16)
    logits = (jnp.dot(hh, rwh_ref[...], preferred_element_type=F32)
              + jnp.dot(hh, rwl_ref[...], preferred_element_type=F32)
              + jnp.dot(hl, rwh_ref[...], preferred_element_type=F32)) + rb_ref[...]

    lane = lax.broadcasted_iota(jnp.int32, (tm, ROUTER_PAD), 1)
    work = logits
    vals, idxs = [], []
    for _ in range(TOP_K):
        m = jnp.max(work, axis=-1, keepdims=True)
        ix = jnp.min(jnp.where(work == m, lane, ROUTER_PAD), axis=-1, keepdims=True)
        vals.append(m)
        idxs.append(ix)
        work = jnp.where(lane == ix, -jnp.inf, work)
    es = [jnp.exp(v - vals[0]) for v in vals]
    den = es[0] + es[1] + es[2] + es[3]

    onehot = jnp.zeros((tm, ROUTER_PAD), F32)
    for ix in idxs:
        onehot = onehot + (lane == ix).astype(F32)
    cum = jnp.dot(tril_ref[...], onehot.astype(BF16), preferred_element_type=F32) + carry[...]
    carry[...] = carry[...] + jnp.sum(onehot, axis=0, keepdims=True)
    cnt_ref[...] = jnp.broadcast_to(carry[...], cnt_ref.shape).astype(jnp.int32)

    idx_o = jnp.zeros((tm, ROUTER_PAD), jnp.int32)
    gate_o = jnp.zeros((tm, ROUTER_PAD), F32)
    rank_o = jnp.zeros((tm, ROUTER_PAD), jnp.int32)
    for k in range(TOP_K):
        rk = jnp.sum(jnp.where(lane == idxs[k], cum, 0.0), axis=-1, keepdims=True).astype(jnp.int32)
        idx_o = jnp.where(lane == k, idxs[k], idx_o)
        gate_o = jnp.where(lane == k, es[k] / den, gate_o)
        rank_o = jnp.where(lane == k, rk, rank_o)
    idx_ref[...] = idx_o
    gate_ref[...] = gate_o
    rank_ref[...] = rank_o


def _merge(ys, yr, z, ms_blk, mr_blk, x2, ps, pr, wo, g2, rwh, rwl, rb, tm):
    L = x2.shape[0]
    tril = (jnp.arange(tm)[:, None] > jnp.arange(tm)[None, :]).astype(BF16)
    const = lambda *shape: pl.BlockSpec(shape, lambda i: (0,) * len(shape), pipeline_mode=pl.Buffered(1))
    row = lambda w: pl.BlockSpec((tm, w), lambda i: (i, 0))
    return pl.pallas_call(
        _merge_kernel,
        name="merge_router",
        grid=(L // tm,),
        in_specs=[
            row(SSM_WIDTH), row(RET_WIDTH),
            pl.BlockSpec((tm, D_MODEL), lambda i: (i, ms_blk)),
            pl.BlockSpec((tm, D_MODEL), lambda i: (i, mr_blk)),
            row(D_MODEL),
            const(SSM_WIDTH, D_MODEL), const(RET_WIDTH, D_MODEL), const(D_MODEL, D_MODEL),
            const(1, D_MODEL),
            const(D_MODEL, ROUTER_PAD), const(D_MODEL, ROUTER_PAD), const(1, ROUTER_PAD),
            const(tm, tm),
        ],
        out_specs=[
            row(D_MODEL), row(D_MODEL),
            row(ROUTER_PAD), row(ROUTER_PAD), row(ROUTER_PAD),
            pl.BlockSpec((SUBLANES, ROUTER_PAD), lambda i: (0, 0)),
        ],
        out_shape=[
            jax.ShapeDtypeStruct((L, D_MODEL), F32),
            jax.ShapeDtypeStruct((L, D_MODEL), F32),
            jax.ShapeDtypeStruct((L, ROUTER_PAD), jnp.int32),
            jax.ShapeDtypeStruct((L, ROUTER_PAD), F32),
            jax.ShapeDtypeStruct((L, ROUTER_PAD), jnp.int32),
            jax.ShapeDtypeStruct((SUBLANES, ROUTER_PAD), jnp.int32),
        ],
        scratch_shapes=[pltpu.VMEM((1, ROUTER_PAD), F32)],
        compiler_params=_cparams(("arbitrary",), 56),
    )(ys, yr, z, z, x2, ps, pr, wo, g2, rwh, rwl, rb, tril)


def _row_tokens_kernel(dest_ref, out_ref):
    def clear(i, c):
        out_ref[i] = 0
        return c

    def place(t, c):
        for k in range(TOP_K):
            out_ref[dest_ref[t * TOP_K + k]] = t
        return c

    lax.fori_loop(0, out_ref.shape[0], clear, 0, unroll=8)
    lax.fori_loop(0, dest_ref.shape[0] // TOP_K, place, 0, unroll=4)


def _row_tokens(dest, n_rows):
    smem = pl.BlockSpec(memory_space=pltpu.SMEM)
    return pl.pallas_call(
        _row_tokens_kernel, name="row_tokens", in_specs=[smem], out_specs=smem,
        out_shape=jax.ShapeDtypeStruct((n_rows,), jnp.int32))(dest)


SUB_ROWS = 128
ITEM_SUBS = 9
BLOCK_SUBS = 8
N_F = 4
SUB_GROUPS = SUB_ROWS // SUBLANES
ACC_COLS = 512


def _expert_kernel(item_e_ref, item_g0_ref, item_nc_ref, item_no_ref, rowtok_ref,
                   h2_hbm, wgu_ref, wd_ref, bgu_ref, bd_ref, unpack_ref, yb_hbm,
                   xbuf, xb, acc, wgu_bf, wd_bf, gsem, osem):
    w = pl.program_id(0)
    f = pl.program_id(1)
    n_w = pl.num_programs(0)
    n_f = pl.num_programs(1)
    tf = wd_ref.shape[1]
    nc = item_nc_ref[w]
    no = item_no_ref[w]

    def groups(s):
        return pl.ds(s * SUB_GROUPS, SUB_GROUPS)

    def gather_sub(item, s):
        base = item_g0_ref[item] * SUBLANES + s * SUB_ROWS
        _gather_rows(h2_hbm, xbuf, gsem.at[0], lambda r: rowtok_ref[base + r], SUB_GROUPS, s * SUB_GROUPS)

    def gather_wait(s):
        pltpu.make_async_copy(xbuf.at[groups(s)], xbuf.at[groups(s)], gsem.at[0]).wait()

    def out_copy(s, g0):
        row0 = pl.multiple_of((g0 + s * SUB_GROUPS) * SUBLANES, SUBLANES)
        src0 = s * SUB_ROWS if isinstance(s, int) else pl.multiple_of(s * SUB_ROWS, SUB_ROWS)
        return pltpu.make_async_copy(acc.at[pl.ds(src0, SUB_ROWS), :],
                                     yb_hbm.at[pl.ds(row0, SUB_ROWS), :], osem.at[0])

    def for_subs(count, fn):
        for s in range(ITEM_SUBS):
            pl.when(s < count)(functools.partial(fn, s))

    @pl.when(f == 0)
    def _():
        @pl.when(w == 0)
        def _():
            xbuf[...] = jnp.zeros_like(xbuf)
            for_subs(nc, lambda s: gather_sub(0, s))

        for_subs(nc, gather_wait)
        xb[...] = xbuf[...].reshape(xb.shape).astype(BF16)

        @pl.when(w > 0)
        def _():
            for_subs(item_no_ref[jnp.maximum(w - 1, 0)], lambda s: out_copy(s, 0).wait())

        acc[...] = jnp.broadcast_to(bd_ref[0] * (nc > 0).astype(F32), acc.shape)

    nxt_item = jnp.minimum(w + 1, n_w - 1)
    nxt_nc = jnp.where(w + 1 < n_w, item_nc_ref[nxt_item], 0)
    last = f == n_f - 1

    per_step = BLOCK_SUBS // N_F
    for i in range(per_step):
        s_dyn = f * per_step + i
        pl.when(s_dyn < nxt_nc)(functools.partial(gather_sub, nxt_item, s_dyn))

    for s in range(BLOCK_SUBS, ITEM_SUBS):
        pl.when(last & (s < nxt_nc))(functools.partial(gather_sub, nxt_item, s))

    def act(guc):
        g = jnp.minimum(guc, SWIGLU_LIMIT)
        u = jnp.clip(pltpu.roll(guc, LANES - 1, axis=1), -SWIGLU_LIMIT, SWIGLU_LIMIT)
        return g * _sigmoid(SWIGLU_ALPHA * g) * (u + 1.0)

    def rows_block(r0, m, cast):
        rows = pl.ds(r0, m) if isinstance(r0, int) else pl.ds(pl.multiple_of(r0, SUB_ROWS), m)
        x = xb[rows, :]
        n_k = tf // MXU_DIM
        even = (lax.broadcasted_iota(jnp.int32, (m, LANES), 1) & 1) == 0

        def gate_up(k):
            outs = []
            for n in range(2):
                cols = slice((2 * k + n) * MXU_DIM, (2 * k + n + 1) * MXU_DIM)
                if cast:
                    wt = wgu_ref[0, :, cols].astype(BF16)
                    wgu_bf[:, cols] = wt
                else:
                    wt = wgu_bf[:, cols]
                outs.append(jnp.dot(x, wt, preferred_element_type=F32) + bgu_ref[0, :, cols])
            return outs

        def activate(gus):
            parts = [jnp.where(even, act(gu[:, :LANES]), pltpu.roll(act(gu[:, LANES:]), 1, axis=1))
                     for gu in gus]
            packed = jnp.concatenate(parts, axis=-1).astype(BF16)
            return jnp.dot(packed, unpack_ref[...], preferred_element_type=F32).astype(BF16)

        gus = gate_up(0)
        for k in range(n_k):
            nxt = gate_up(k + 1) if k + 1 < n_k else None
            a = activate(gus)
            wrows = slice(k * MXU_DIM, (k + 1) * MXU_DIM)
            if cast:
                wdt = wd_ref[0, wrows, :].astype(BF16)
                wd_bf[wrows, :] = wdt
            else:
                wdt = wd_bf[wrows, :]
            for c in range(D_MODEL // ACC_COLS):
                cs = slice(c * ACC_COLS, (c + 1) * ACC_COLS)
                acc[rows, cs] += jnp.dot(a, wdt[:, cs], preferred_element_type=F32)
            gus = nxt

    g0 = item_g0_ref[w]
    block_rows = BLOCK_SUBS * SUB_ROWS
    big = nc >= BLOCK_SUBS

    def extra_sub(s, start_out):
        rows_block(s * SUB_ROWS, SUB_ROWS, False)
        if start_out:
            out_copy(s, g0).start()

    @pl.when(big & jnp.logical_not(last))
    def _():
        rows_block(0, block_rows, True)
        for s in range(BLOCK_SUBS, ITEM_SUBS):
            pl.when(s < nc)(functools.partial(extra_sub, s, False))

    @pl.when(big & last)
    def _():
        half = BLOCK_SUBS // 2
        rows_block(0, block_rows // 2, True)
        for s in range(half):
            out_copy(s, g0).start()
        rows_block(block_rows // 2, block_rows // 2, False)
        for s in range(half, BLOCK_SUBS):
            out_copy(s, g0).start()
        for s in range(BLOCK_SUBS, ITEM_SUBS):
            pl.when(s < nc)(functools.partial(extra_sub, s, True))

    @pl.when((nc > 0) & jnp.logical_not(big))
    def _():
        def chunk(s0, n_sub, cast):
            rows_block(s0 * SUB_ROWS, n_sub * SUB_ROWS, cast)

            def start_outs():
                for i in range(n_sub):
                    out_copy(s0 + i, g0).start()

            pl.when(last)(start_outs)

        sizes = []
        n_sub = BLOCK_SUBS // 2
        while n_sub >= 1:
            sizes.append(n_sub)
            n_sub //= 2
        for i, n_sub in enumerate(sizes):
            has = (nc & n_sub) != 0
            larger = nc & ~(2 * n_sub - 1)
            pl.when(has & (larger == 0))(functools.partial(chunk, 0, n_sub, True))
            if i > 0:
                pl.when(has & (larger != 0))(functools.partial(chunk, larger, n_sub, False))

    @pl.when((nc == 0) & last)
    def _():
        for_subs(no, lambda s: out_copy(s, g0).start())

    @pl.when(last & (w == n_w - 1))
    def _():
        for_subs(no, lambda s: out_copy(s, 0).wait())


def _unpack_matrix():
    lane = jnp.arange(MXU_DIM)
    n, r = lane // LANES, lane % LANES
    natural = n * LANES + (r % 2) * (LANES // 2) + r // 2
    return (natural[:, None] == jnp.arange(MXU_DIM)[None, :]).astype(BF16)


def _expert_items(counts, n_asg):
    E = N_EXPERTS
    item_rows = SUB_ROWS * ITEM_SUBS
    n_items_max = (n_asg // SUB_ROWS + E) // ITEM_SUBS + E + 1
    n_rows = n_asg + E * SUB_ROWS
    pcounts = (counts + SUB_ROWS - 1) // SUB_ROWS * SUB_ROWS
    pend = jnp.cumsum(pcounts)
    pstart = pend - pcounts
    nsb_e = pcounts // SUB_ROWS
    n_it = (nsb_e + ITEM_SUBS - 1) // ITEM_SUBS
    extra_e = jnp.maximum(nsb_e - BLOCK_SUBS * n_it, 0)
    it_end = jnp.cumsum(n_it)
    it_start = it_end - n_it
    n_items = it_end[-1]
    wv = jnp.arange(n_items_max, dtype=jnp.int32)
    e_w = jnp.minimum(jnp.sum(wv[:, None] >= it_end[None, :], axis=1), E - 1)
    j_w = wv - it_start[e_w]
    first_sub = lambda j: BLOCK_SUBS * j + jnp.minimum(j, extra_e[e_w])
    sb0 = first_sub(j_w)
    row0 = pstart[e_w] + sb0 * SUB_ROWS
    nsb = jnp.clip(jnp.minimum(nsb_e[e_w], first_sub(j_w + 1)) - sb0, 0, ITEM_SUBS)
    used = wv < n_items
    tail0 = pend[-1] + (wv - n_items) * item_rows
    nz = jnp.clip((n_rows - tail0) // SUB_ROWS, 0, ITEM_SUBS)
    item_row0 = jnp.clip(jnp.where(used, row0, tail0), 0, n_rows - SUB_ROWS)
    item_e = jnp.where(used, e_w, e_w[jnp.maximum(n_items - 1, 0)])
    i32 = lambda a: a.astype(jnp.int32)
    return (i32(item_e), i32(item_row0 // SUBLANES), i32(jnp.where(used, nsb, 0)),
            i32(jnp.where(used, nsb, nz)), pstart, n_rows)


def _experts(item_e, item_g0, item_nc, item_no, row_tok, h2, wgu, wd, bgu, bd, tf):
    n_rows = row_tok.shape[0]
    n_items = item_e.shape[0]
    n_f = EXPERT_FF // tf
    item_rows = SUB_ROWS * ITEM_SUBS
    assert tf % MXU_DIM == 0 and n_f == N_F and N_F <= BLOCK_SUBS <= ITEM_SUBS

    def f_eff(w, f, nc):
        return jnp.where(nc[w] > 0, f, n_f - 1)

    grid_spec = pltpu.PrefetchScalarGridSpec(
        num_scalar_prefetch=5,
        grid=(n_items, n_f),
        in_specs=[
            pl.BlockSpec(memory_space=pl.ANY),
            pl.BlockSpec((1, D_MODEL, 2 * tf), lambda w, f, ie, g0, nc, no, rt: (ie[w], 0, f_eff(w, f, nc))),
            pl.BlockSpec((1, tf, D_MODEL), lambda w, f, ie, g0, nc, no, rt: (ie[w], f_eff(w, f, nc), 0)),
            pl.BlockSpec((1, 1, 2 * tf), lambda w, f, ie, g0, nc, no, rt: (ie[w], 0, f_eff(w, f, nc))),
            pl.BlockSpec((1, 1, D_MODEL), lambda w, f, ie, g0, nc, no, rt: (ie[w], 0, 0)),
            pl.BlockSpec((MXU_DIM, MXU_DIM), lambda w, f, ie, g0, nc, no, rt: (0, 0)),
        ],
        out_specs=pl.BlockSpec(memory_space=pl.ANY),
        scratch_shapes=[
            pltpu.VMEM((item_rows // SUBLANES, SUBLANES, D_MODEL), F32),
            pltpu.VMEM((item_rows, D_MODEL), BF16),
            pltpu.VMEM((item_rows, D_MODEL), F32),
            pltpu.VMEM((D_MODEL, 2 * tf), BF16),
            pltpu.VMEM((tf, D_MODEL), BF16),
            pltpu.SemaphoreType.DMA((1,)),
            pltpu.SemaphoreType.DMA((1,)),
        ],
    )
    return pl.pallas_call(
        _expert_kernel,
        name="experts",
        grid_spec=grid_spec,
        out_shape=jax.ShapeDtypeStruct((n_rows, D_MODEL), F32),
        compiler_params=_cparams(("arbitrary", "arbitrary"), 60),
    )(item_e, item_g0, item_nc, item_no, row_tok, h2, wgu, wd, bgu, bd, _unpack_matrix())


def _combine_kernel(dest_ref, yb_hbm, gate_ref, x1_ref, p_ref, wpg_ref, wpp_ref, g3_ref, gf_ref,
                    o_ref, gbuf, sem):
    i = pl.program_id(0)
    n = pl.num_programs(0)
    tm = x1_ref.shape[0]

    def issue(tile):
        base = tile * tm * TOP_K

        for k in range(TOP_K):
            _gather_rows(yb_hbm, gbuf.at[k], sem.at[0], lambda r, k=k: dest_ref[base + r * TOP_K + k],
                         tm // SUBLANES, 0, both_queues=True)

    @pl.when(i == 0)
    def _():
        issue(0)

    for k in range(TOP_K):
        pltpu.make_async_copy(gbuf.at[k], gbuf.at[k], sem.at[0]).wait()
    gates = gate_ref[...]
    moe = gates[:, 0:1] * gbuf[0].reshape(tm, D_MODEL)
    for k in range(1, TOP_K):
        moe = moe + gates[:, k:k + 1] * gbuf[k].reshape(tm, D_MODEL)
    x2 = x1_ref[...] + moe

    @pl.when(i + 1 < n)
    def _():
        issue(i + 1)

    h3 = _rms(x2, g3_ref[...]).astype(BF16)
    pg = _sigmoid(jnp.dot(h3, wpg_ref[...], preferred_element_type=F32))
    pp = jnp.dot(p_ref[...].astype(BF16), wpp_ref[...], preferred_element_type=F32)
    x3 = x2 + pg * pp
    o_ref[...] = _rms(x3, gf_ref[...])


def _combine(dest, yb, gates, x1, p2, wpg, wpp, g3, gf, tm):
    L = x1.shape[0]
    const = lambda *shape: pl.BlockSpec(shape, lambda i, d: (0,) * len(shape), pipeline_mode=pl.Buffered(1))
    row = lambda w: pl.BlockSpec((tm, w), lambda i, d: (i, 0))
    grid_spec = pltpu.PrefetchScalarGridSpec(
        num_scalar_prefetch=1,
        grid=(L // tm,),
        in_specs=[
            pl.BlockSpec(memory_space=pl.ANY),
            row(ROUTER_PAD), row(D_MODEL), row(PLE_DIM),
            const(D_MODEL, D_MODEL), const(PLE_DIM, D_MODEL), const(1, D_MODEL), const(1, D_MODEL),
        ],
        out_specs=row(D_MODEL),
        scratch_shapes=[
            pltpu.VMEM((TOP_K, tm // SUBLANES, SUBLANES, D_MODEL), F32),
            pltpu.SemaphoreType.DMA((1,)),
        ],
    )
    return pl.pallas_call(
        _combine_kernel,
        name="combine_ple",
        grid_spec=grid_spec,
        out_shape=jax.ShapeDtypeStruct((L, D_MODEL), F32),
        compiler_params=_cparams(("arbitrary",), 48),
    )(dest, yb, gates, x1, p2, wpg, wpp, g3, gf)


def _tile(L, pref):
    return min(pref, L)


def _layer(x2, p2, pos, norm1_g, w_in, lam_re, lam_im, log_dt, b_re, b_im, c_re, c_im, d_skip, w_glu,
           b_glu, w_bs, w_br, w_out, norm2_g, router_w, router_b, w_gu, b_gu, w_dn, b_dn, norm3_g,
           w_pg, w_pp, out_g):
    L = x2.shape[0]
    G, N, E = SSM_GROUPS, SSM_STATE, N_EXPERTS

    cuts = SSM_WIDTH + 4 * RET_WIDTH
    tn = 1024
    z = _inproj(x2, norm1_g[None, :], w_in, cuts // tn, _tile(L, 1024), tn)
    ms_blk, mr_blk = 0, 1
    u_blk, q_blk, k_blk, v_blk, g_blk = 4, 5, 6, 7, 8

    ab_re, ab_im, bb_re, bb_im = _s5_prep(lam_re, lam_im, log_dt, b_re, b_im)
    eye = jnp.eye(SUBLANES, dtype=F32)
    gpb = LANES // SSM_GROUP
    def bpack(bb):
        t = bb.reshape(N_UBLK, gpb, SSM_GROUP, N)
        return jnp.einsum('bgcn,gh->bgchn', t, eye).reshape(N_UBLK, LANES, ST_PER_UBLK)
    def cpack(c):
        t = c.reshape(N_UBLK, gpb, SSM_GROUP, N)
        return jnp.einsum('bgcn,gh->bgnhc', t, eye).reshape(N_UBLK, ST_PER_UBLK, LANES)
    bblk = jnp.concatenate([bpack(bb_re), bpack(bb_im)], axis=-1).astype(BF16)
    cblk = jnp.concatenate([cpack(c_re), -cpack(c_im)], axis=1).astype(BF16)
    atile = lambda a: a[::SSM_GROUP].reshape(N_SVREG, SUBLANES, LANES)
    ys = _s5(z, u_blk, bblk, cblk, atile(ab_re), atile(ab_im), d_skip.reshape(1, SSM_WIDTH),
             w_glu.astype(BF16), b_glu[None, :], _tile(L, 512))

    yr = _retention(z, (q_blk, k_blk, v_blk, g_blk), pos, _tile(L, 512))

    rw = jnp.pad(router_w, ((0, 0), (0, ROUTER_PAD - E)))
    rwh = rw.astype(BF16)
    rwl = (rw - rwh.astype(F32)).astype(BF16)
    rb = jnp.pad(router_b, (0, ROUTER_PAD - E), constant_values=NEG_BIG)[None, :]
    x1, h2, idx, gates, rank, cnt = _merge(
        ys, yr, z, ms_blk, mr_blk, x2, w_bs.astype(BF16), w_br.astype(BF16), w_out.astype(BF16),
        norm2_g[None, :], rwh, rwl, rb, _tile(L, 256))

    item_e, item_g0, item_nc, item_no, pstart, n_rows = _expert_items(cnt[0, :E], L * TOP_K)
    idx4, rank4 = idx[:, :TOP_K], rank[:, :TOP_K]
    dest = (pstart[idx4] + rank4).astype(jnp.int32).reshape(-1)
    row_tok = _row_tokens(dest, n_rows)

    yb = _experts(item_e, item_g0, item_nc, item_no, row_tok, h2, w_gu, w_dn, b_gu[:, None, :],
                  b_dn[:, None, :], EXPERT_FF // N_F)

    return _combine(dest, yb, gates, x1, p2, w_pg.astype(BF16), w_pp.astype(BF16),
                    norm3_g[None, :], out_g[None, :], _tile(L, 256))


def kernel(x, p, positions, norm1_g, w_in, ssm_lam_re, ssm_lam_im, ssm_log_dt, ssm_b_re, ssm_b_im, ssm_c_re, ssm_c_im, ssm_d, ssm_w_glu, ssm_b_glu, w_branch_ssm, w_branch_ret, w_out, norm2_g, router_w, router_b, exp_w_gate_up, exp_b_gate_up, exp_w_down, exp_b_down, norm3_g, ple_w_gate, ple_w_proj, final_norm_g):
    bsz, L, d = x.shape
    depth = w_in.shape[0]
    assert bsz == 1 and depth == 1 and d == D_MODEL
    out = _layer(
        x[0], p[0, 0], positions.reshape(L, 1), norm1_g[0], w_in[0], ssm_lam_re[0], ssm_lam_im[0],
        ssm_log_dt[0], ssm_b_re[0], ssm_b_im[0], ssm_c_re[0], ssm_c_im[0], ssm_d[0], ssm_w_glu[0],
        ssm_b_glu[0], w_branch_ssm[0], w_branch_ret[0], w_out[0], norm2_g[0], router_w[0], router_b[0],
        exp_w_gate_up[0], exp_b_gate_up[0], exp_w_down[0], exp_b_down[0], norm3_g[0], ple_w_gate[0],
        ple_w_proj[0], final_norm_g)
    return out[None]
```

```python
import functools
import math

import jax
import jax.numpy as jnp
from jax import lax
from jax.experimental import pallas as pl
from jax.experimental.pallas import tpu as pltpu

F32 = jnp.float32
BF16 = jnp.bfloat16

D_MODEL = 2048
PLE_DIM = 256
SSM_WIDTH = D_MODEL // 2
SSM_GROUP = 16
SSM_GROUPS = SSM_WIDTH // SSM_GROUP
SSM_STATE = 64
RET_HEADS = 8
RET_HEAD_DIM = 128
RET_WIDTH = RET_HEADS * RET_HEAD_DIM
RET_CHUNK = 128
ROPE_BASE = 10000.0
N_EXPERTS = 32
TOP_K = 4
EXPERT_FF = D_MODEL
SWIGLU_LIMIT = 7.0
SWIGLU_ALPHA = 1.702
NORM_EPS = 1e-6
IN_COLS = SSM_WIDTH + 4 * RET_WIDTH + 2 * D_MODEL

LANES = 128
SUBLANES = 8
MXU_DIM = 256
ROUTER_PAD = LANES
NEG_BIG = -1e30

MIB = 1024 * 1024


def _cparams(sem, vmem_mib):
    return pltpu.CompilerParams(dimension_semantics=sem, vmem_limit_bytes=vmem_mib * MIB)


def _rms(xf, g):
    ms = jnp.mean(xf * xf, axis=-1, keepdims=True)
    return xf * lax.rsqrt(ms + NORM_EPS) * g


def _sigmoid(x):
    return 1.0 / (1.0 + jnp.exp(-x))


GATHER_UNROLL = 4


def _gather_rows(src2, dst3, sem, index_of, n_groups, dst_group0, inline=False, first_group=0):
    def body(j, c):
        for u in range(SUBLANES):
            row = index_of(j * SUBLANES + u)
            pltpu.make_async_copy(src2.at[pl.ds(row, 1), :],
                                  dst3.at[dst_group0 + j, pl.ds(u, 1), :], sem).start()
        return c

    if inline:
        for j in range(first_group, first_group + n_groups):
            body(j, 0)
    else:
        lax.fori_loop(first_group, first_group + n_groups, body, 0, unroll=GATHER_UNROLL)


def _inproj_kernel(x_ref, g_ref, w_ref, o_ref, h_scr):
    @pl.when(pl.program_id(1) == 0)
    def _():
        h_scr[...] = _rms(x_ref[...], g_ref[...]).astype(BF16)

    o_ref[...] = jnp.dot(h_scr[...], w_ref[...], preferred_element_type=F32).astype(o_ref.dtype)


def _inproj(x2, g1, w_in_bf, tm, tn):
    L = x2.shape[0]
    n_cols = w_in_bf.shape[1]
    return pl.pallas_call(
        _inproj_kernel,
        name="inproj",
        grid=(L // tm, n_cols // tn),
        in_specs=[
            pl.BlockSpec((tm, D_MODEL), lambda i, j: (i, 0)),
            pl.BlockSpec((1, D_MODEL), lambda i, j: (0, 0)),
            pl.BlockSpec((D_MODEL, tn), lambda i, j: (0, j)),
        ],
        out_specs=pl.BlockSpec((tm, tn), lambda i, j: (i, j)),
        out_shape=jax.ShapeDtypeStruct((L, n_cols), BF16),
        scratch_shapes=[pltpu.VMEM((tm, D_MODEL), BF16)],
        compiler_params=_cparams(("arbitrary", "arbitrary"), 48),
    )(x2, g1, w_in_bf)


def _s5_prep_kernel(lr_ref, li_ref, ldt_ref, bre_ref, bim_ref, abre_ref, abim_ref, bbre_ref, bbim_ref):
    lr, li = lr_ref[...], li_ref[...]
    dt = jnp.exp(ldt_ref[...])
    mag = jnp.exp(lr * dt)
    ab_re, ab_im = mag * jnp.cos(li * dt), mag * jnp.sin(li * dt)
    den = lr * lr + li * li
    nr, ni = ab_re - 1.0, ab_im
    f_re = (nr * lr + ni * li) / den
    f_im = (ni * lr - nr * li) / den
    bre, bim = bre_ref[...], bim_ref[...]
    abre_ref[...] = ab_re
    abim_ref[...] = ab_im
    bbre_ref[...] = f_re * bre - f_im * bim
    bbim_ref[...] = f_re * bim + f_im * bre


def _s5_prep(lam_re, lam_im, log_dt, b_re, b_im):
    rep = lambda a: jnp.repeat(a, SSM_GROUP, axis=0)
    bt = lambda b: jnp.transpose(b, (0, 2, 1)).reshape(SSM_WIDTH, SSM_STATE)
    shp = jax.ShapeDtypeStruct((SSM_WIDTH, SSM_STATE), F32)
    return pl.pallas_call(_s5_prep_kernel, name="s5_prep", out_shape=(shp, shp, shp, shp))(
        rep(lam_re), rep(lam_im), rep(log_dt[:, None]), bt(b_re), bt(b_im))


def _gelu_tanh(x):
    return 0.5 * x * (1.0 + jnp.tanh(math.sqrt(2.0 / math.pi) * (x + 0.044715 * (x * x * x))))


N_UBLK = SSM_WIDTH // LANES
ST_PER_UBLK = (LANES // SSM_GROUP) * SSM_STATE
TILES_PER_UBLK = ST_PER_UBLK // LANES
N_SVREG = SSM_GROUPS * SSM_STATE // (LANES * SUBLANES)


def _s5_kernel(u_ref, bblk_ref, cblk_ref, are_ref, aim_ref, d_ref, wglu_ref, bglu_ref, o_ref,
               sre, sim, carry):
    tm = u_ref.shape[0]

    @pl.when(pl.program_id(0) == 0)
    def _():
        carry[...] = jnp.zeros_like(carry)

    u = u_ref[...]
    for b in range(N_UBLK):
        bu = jnp.dot(u[:, b * LANES:(b + 1) * LANES], bblk_ref[b], preferred_element_type=F32)
        v = (b * TILES_PER_UBLK) // SUBLANES
        for q in range(TILES_PER_UBLK):
            k = (b * TILES_PER_UBLK + q) % SUBLANES
            sre[v, pl.ds(k, tm, stride=SUBLANES), :] = bu[:, q * LANES:(q + 1) * LANES]
            sim[v, pl.ds(k, tm, stride=SUBLANES), :] = bu[:, ST_PER_UBLK + q * LANES:ST_PER_UBLK + (q + 1) * LANES]

    are, aim = are_ref[...], aim_ref[...]

    def step(t, c):
        cre, cim = c
        r0 = pl.multiple_of(t * SUBLANES, SUBLANES)
        bre = sre[:, pl.ds(r0, SUBLANES), :]
        bim = sim[:, pl.ds(r0, SUBLANES), :]
        nre = are * cre - aim * cim + bre
        nim = are * cim + aim * cre + bim
        sre[:, pl.ds(r0, SUBLANES), :] = nre
        sim[:, pl.ds(r0, SUBLANES), :] = nim
        return nre, nim

    cre, cim = lax.fori_loop(0, tm, step, (carry[0], carry[1]), unroll=4)
    carry[0] = cre
    carry[1] = cim

    ys = []
    for b in range(N_UBLK):
        v = (b * TILES_PER_UBLK) // SUBLANES
        k0 = (b * TILES_PER_UBLK) % SUBLANES
        parts = [sre[v, pl.ds(k0 + q, tm, stride=SUBLANES), :] for q in range(TILES_PER_UBLK)]
        parts += [sim[v, pl.ds(k0 + q, tm, stride=SUBLANES), :] for q in range(TILES_PER_UBLK)]
        sb = jnp.concatenate(parts, axis=-1).astype(BF16)
        ys.append(jnp.dot(sb, cblk_ref[b], preferred_element_type=F32))
    y = jnp.concatenate(ys, axis=-1) + d_ref[...] * u.astype(F32)
    y = _gelu_tanh(y)
    gl = jnp.dot(y.astype(BF16), wglu_ref[...], preferred_element_type=F32) + bglu_ref[...]
    o_ref[...] = (y * _sigmoid(gl)).astype(o_ref.dtype)


def _s5(z, u_blk, bblk, cblk, are, aim, d_skip, wglu, bglu, tm):
    L = z.shape[0]
    const = lambda *shape: pl.BlockSpec(shape, lambda i: (0,) * len(shape))
    return pl.pallas_call(
        _s5_kernel,
        name="s5_branch",
        grid=(L // tm,),
        in_specs=[
            pl.BlockSpec((tm, SSM_WIDTH), lambda i: (i, u_blk)),
            const(N_UBLK, LANES, 2 * ST_PER_UBLK),
            const(N_UBLK, 2 * ST_PER_UBLK, LANES),
            const(N_SVREG, SUBLANES, LANES),
            const(N_SVREG, SUBLANES, LANES),
            const(1, SSM_WIDTH),
            const(SSM_WIDTH, SSM_WIDTH),
            const(1, SSM_WIDTH),
        ],
        out_specs=pl.BlockSpec((tm, SSM_WIDTH), lambda i: (i, 0)),
        out_shape=jax.ShapeDtypeStruct((L, SSM_WIDTH), BF16),
        scratch_shapes=[
            pltpu.VMEM((N_SVREG, tm * SUBLANES, LANES), F32),
            pltpu.VMEM((N_SVREG, tm * SUBLANES, LANES), F32),
            pltpu.VMEM((2, N_SVREG, SUBLANES, LANES), F32),
        ],
        compiler_params=_cparams(("arbitrary",), 48),
    )(z, bblk, cblk, are, aim, d_skip, wglu, bglu)


def _ret_kernel(q_ref, k_ref, v_ref, g_ref, pos_ref, inv_ref, decay_ref, zeta_ref, xi_ref, cd_ref,
                o_ref, r_scr):
    tm = q_ref.shape[0]
    C, dk = RET_CHUNK, RET_HEAD_DIM

    @pl.when(pl.program_id(0) == 0)
    def _():
        r_scr[...] = jnp.zeros_like(r_scr)

    ang = pos_ref[...].astype(F32) * inv_ref[...]
    cs, sn = jnp.cos(ang), jnp.sin(ang)
    cf = jnp.concatenate([cs, cs], axis=-1)
    sf = jnp.concatenate([-sn, sn], axis=-1)

    def rot(xh):
        return xh * cf + pltpu.roll(xh, dk // 2, axis=1) * sf

    nt = (((1,), (1,)), ((), ()))
    for h in range(RET_HEADS):
        cols = slice(h * dk, (h + 1) * dk)
        qr = rot(q_ref[:, cols].astype(F32))
        kr = rot(k_ref[:, cols].astype(F32))
        for n in range(tm // C):
            rows = slice(n * C, (n + 1) * C)
            qc = qr[rows].astype(BF16)
            kcf = kr[rows]
            vc = v_ref[rows, cols]
            sc = lax.dot_general(qc, kcf.astype(BF16), nt, preferred_element_type=F32) * decay_ref[h]
            inner = jnp.dot(sc.astype(BF16), vc, preferred_element_type=F32)
            r_prev = r_scr[h]
            cross = jnp.dot(qc, r_prev.astype(BF16), preferred_element_type=F32) * xi_ref[h]
            o = inner + cross
            kzt = jnp.transpose(kcf * zeta_ref[h]).astype(BF16)
            r_scr[h] = r_prev * cd_ref[h] + jnp.dot(kzt, vc, preferred_element_type=F32)
            mu = jnp.mean(o, axis=-1, keepdims=True)
            oc = o - mu
            on = oc * lax.rsqrt(jnp.mean(oc * oc, axis=-1, keepdims=True) + NORM_EPS)
            gate = g_ref[rows, cols].astype(F32)
            o_ref[rows, cols] = (gate * _sigmoid(gate) * on).astype(o_ref.dtype)


def _retention(z, blks, pos, tm):
    L = z.shape[0]
    H, C, dk = RET_HEADS, RET_CHUNK, RET_HEAD_DIM
    half = dk // 2
    inv = (ROPE_BASE ** (-jnp.arange(half, dtype=F32) / half))[None, :]
    log_gamma = jnp.log(1.0 - jnp.exp2(-5.0 - jnp.arange(H, dtype=F32)))
    idx = jnp.arange(C, dtype=F32)
    rel = idx[:, None] - idx[None, :]
    scale = dk ** -0.5
    decay = jnp.where(rel >= 0, jnp.exp(jnp.maximum(rel, 0.0)[None] * log_gamma[:, None, None]), 0.0) * scale
    zeta = jnp.exp((C - 1.0 - idx)[None, :] * log_gamma[:, None]) * scale
    xi = jnp.exp((idx + 1.0)[None, :] * log_gamma[:, None])
    bc = lambda a: jnp.broadcast_to(a[:, :, None], (H, C, dk))
    cd = jnp.broadcast_to(jnp.exp(C * log_gamma)[:, None, None], (H, 1, dk))
    const = lambda *shape: pl.BlockSpec(shape, lambda i: (0,) * len(shape))
    qb, kb, vb, gb = blks
    return pl.pallas_call(
        _ret_kernel,
        name="retention",
        grid=(L // tm,),
        in_specs=[
            pl.BlockSpec((tm, RET_WIDTH), lambda i: (i, qb)),
            pl.BlockSpec((tm, RET_WIDTH), lambda i: (i, kb)),
            pl.BlockSpec((tm, RET_WIDTH), lambda i: (i, vb)),
            pl.BlockSpec((tm, RET_WIDTH), lambda i: (i, gb)),
            pl.BlockSpec((tm, 1), lambda i: (i, 0)),
            const(1, half),
            const(H, C, C),
            const(H, C, dk),
            const(H, C, dk),
            const(H, 1, dk),
        ],
        out_specs=pl.BlockSpec((tm, RET_WIDTH), lambda i: (i, 0)),
        out_shape=jax.ShapeDtypeStruct((L, RET_WIDTH), BF16),
        scratch_shapes=[pltpu.VMEM((H, dk, dk), F32)],
        compiler_params=_cparams(("arbitrary",), 48),
    )(z, z, z, z, pos, inv, decay, bc(zeta), bc(xi), cd)


def _merge_kernel(ys_ref, yr_ref, ms_ref, mr_ref, x_ref, ps_ref, pr_ref, wo_ref, g2_ref,
                  rwh_ref, rwl_ref, rb_ref, tril_ref,
                  x1_ref, h2_ref, idx_ref, gate_ref, rank_ref, cnt_ref, carry):
    tm = x_ref.shape[0]

    @pl.when(pl.program_id(0) == 0)
    def _():
        carry[...] = jnp.zeros_like(carry)

    a = jnp.dot(ys_ref[...], ps_ref[...], preferred_element_type=F32)
    b = jnp.dot(yr_ref[...], pr_ref[...], preferred_element_type=F32)
    merged = _sigmoid(ms_ref[...].astype(F32)) * a + _sigmoid(mr_ref[...].astype(F32)) * b
    x1 = x_ref[...] + jnp.dot(merged.astype(BF16), wo_ref[...], preferred_element_type=F32)
    x1_ref[...] = x1
    h2 = _rms(x1, g2_ref[...])
    h2_ref[...] = h2

    hh = h2.astype(BF16)
    hl = (h2 - hh.astype(F32)).astype(BF16)
    logits = (jnp.dot(hh, rwh_ref[...], preferred_element_type=F32)
              + jnp.dot(hh, rwl_ref[...], preferred_element_type=F32)
              + jnp.dot(hl, rwh_ref[...], preferred_element_type=F32)) + rb_ref[...]

    lane = lax.broadcasted_iota(jnp.int32, (tm, ROUTER_PAD), 1)
    work = logits
    vals, idxs = [], []
    for _ in range(TOP_K):
        m = jnp.max(work, axis=-1, keepdims=True)
        ix = jnp.min(jnp.where(work == m, lane, ROUTER_PAD), axis=-1, keepdims=True)
        vals.append(m)
        idxs.append(ix)
        work = jnp.where(lane == ix, -jnp.inf, work)
    es = [jnp.exp(v - vals[0]) for v in vals]
    den = es[0] + es[1] + es[2] + es[3]

    onehot = jnp.zeros((tm, ROUTER_PAD), F32)
    for ix in idxs:
        onehot = onehot + (lane == ix).astype(F32)
    cum = jnp.dot(tril_ref[...], onehot.astype(BF16), preferred_element_type=F32) + carry[...]
    carry[...] = carry[...] + jnp.sum(onehot, axis=0, keepdims=True)
    cnt_ref[...] = jnp.broadcast_to(carry[...], cnt_ref.shape).astype(jnp.int32)

    idx_o = jnp.zeros((tm, ROUTER_PAD), jnp.int32)
    gate_o = jnp.zeros((tm, ROUTER_PAD), F32)
    rank_o = jnp.zeros((tm, ROUTER_PAD), jnp.int32)
    for k in range(TOP_K):
        rk = jnp.sum(jnp.where(lane == idxs[k], cum, 0.0), axis=-1, keepdims=True).astype(jnp.int32)
        idx_o = jnp.where(lane == k, idxs[k], idx_o)
        gate_o = jnp.where(lane == k, es[k] / den, gate_o)
        rank_o = jnp.where(lane == k, rk, rank_o)
    idx_ref[...] = idx_o
    gate_ref[...] = gate_o
    rank_ref[...] = rank_o


def _merge(ys, yr, z, ms_blk, mr_blk, x2, ps, pr, wo, g2, rwh, rwl, rb, tm):
    L = x2.shape[0]
    tril = (jnp.arange(tm)[:, None] > jnp.arange(tm)[None, :]).astype(BF16)
    const = lambda *shape: pl.BlockSpec(shape, lambda i: (0,) * len(shape), pipeline_mode=pl.Buffered(1))
    row = lambda w: pl.BlockSpec((tm, w), lambda i: (i, 0))
    return pl.pallas_call(
        _merge_kernel,
        name="merge_router",
        grid=(L // tm,),
        in_specs=[
            row(SSM_WIDTH), row(RET_WIDTH),
            pl.BlockSpec((tm, D_MODEL), lambda i: (i, ms_blk)),
            pl.BlockSpec((tm, D_MODEL), lambda i: (i, mr_blk)),
            row(D_MODEL),
            const(SSM_WIDTH, D_MODEL), const(RET_WIDTH, D_MODEL), const(D_MODEL, D_MODEL),
            const(1, D_MODEL),
            const(D_MODEL, ROUTER_PAD), const(D_MODEL, ROUTER_PAD), const(1, ROUTER_PAD),
            const(tm, tm),
        ],
        out_specs=[
            row(D_MODEL), row(D_MODEL),
            row(ROUTER_PAD), row(ROUTER_PAD), row(ROUTER_PAD),
            pl.BlockSpec((SUBLANES, ROUTER_PAD), lambda i: (0, 0)),
        ],
        out_shape=[
            jax.ShapeDtypeStruct((L, D_MODEL), F32),
            jax.ShapeDtypeStruct((L, D_MODEL), F32),
            jax.ShapeDtypeStruct((L, ROUTER_PAD), jnp.int32),
            jax.ShapeDtypeStruct((L, ROUTER_PAD), F32),
            jax.ShapeDtypeStruct((L, ROUTER_PAD), jnp.int32),
            jax.ShapeDtypeStruct((SUBLANES, ROUTER_PAD), jnp.int32),
        ],
        scratch_shapes=[pltpu.VMEM((1, ROUTER_PAD), F32)],
        compiler_params=_cparams(("arbitrary",), 56),
    )(ys, yr, z, z, x2, ps, pr, wo, g2, rwh, rwl, rb, tril)


def _row_tokens_kernel(dest_ref, out_ref):
    def clear(i, c):
        out_ref[i] = 0
        return c

    def place(t, c):
        for k in range(TOP_K):
            out_ref[dest_ref[t * TOP_K + k]] = t
        return c

    lax.fori_loop(0, out_ref.shape[0], clear, 0, unroll=8)
    lax.fori_loop(0, dest_ref.shape[0] // TOP_K, place, 0, unroll=4)


def _row_tokens(dest, n_rows):
    smem = pl.BlockSpec(memory_space=pltpu.SMEM)
    return pl.pallas_call(
        _row_tokens_kernel, name="row_tokens", in_specs=[smem], out_specs=smem,
        out_shape=jax.ShapeDtypeStruct((n_rows,), jnp.int32))(dest)


SUB_ROWS = 128
ITEM_SUBS = 9
BLOCK_SUBS = 8
N_F = 4
SUB_GROUPS = SUB_ROWS // SUBLANES
ACC_COLS = 512


def _expert_kernel(item_e_ref, item_g0_ref, item_nc_ref, item_no_ref, rowtok_ref,
                   h2_hbm, wgu_ref, wd_ref, bgu_ref, bd_ref, unpack_ref, yb_hbm,
                   xbuf, xb, acc, wgu_bf, wd_bf, gsem, osem):
    w = pl.program_id(0)
    f = pl.program_id(1)
    n_w = pl.num_programs(0)
    n_f = pl.num_programs(1)
    tf = wd_ref.shape[1]
    nc = item_nc_ref[w]
    no = item_no_ref[w]

    def groups(s):
        return pl.ds(s * SUB_GROUPS, SUB_GROUPS)

    def gather_sub(item, s):
        base = item_g0_ref[item] * SUBLANES + s * SUB_ROWS
        _gather_rows(h2_hbm, xbuf, gsem.at[0], lambda r: rowtok_ref[base + r], SUB_GROUPS, s * SUB_GROUPS)

    def gather_wait(s):
        pltpu.make_async_copy(xbuf.at[groups(s)], xbuf.at[groups(s)], gsem.at[0]).wait()

    def out_copy(s, g0):
        row0 = pl.multiple_of((g0 + s * SUB_GROUPS) * SUBLANES, SUBLANES)
        src0 = s * SUB_ROWS if isinstance(s, int) else pl.multiple_of(s * SUB_ROWS, SUB_ROWS)
        return pltpu.make_async_copy(acc.at[pl.ds(src0, SUB_ROWS), :],
                                     yb_hbm.at[pl.ds(row0, SUB_ROWS), :], osem.at[0])

    def for_subs(count, fn):
        for s in range(ITEM_SUBS):
            pl.when(s < count)(functools.partial(fn, s))

    @pl.when(f == 0)
    def _():
        @pl.when(w == 0)
        def _():
            for_subs(nc, lambda s: gather_sub(0, s))

        for_subs(nc, gather_wait)

        def stage(s):
            xb[s * SUB_ROWS:(s + 1) * SUB_ROWS, :] = xbuf[groups(s)].reshape(SUB_ROWS, D_MODEL).astype(BF16)

        for_subs(nc, stage)

        @pl.when(w > 0)
        def _():
            for_subs(item_no_ref[jnp.maximum(w - 1, 0)], lambda s: out_copy(s, 0).wait())

        init = bd_ref[0] * (nc > 0).astype(F32)

        def init_acc(s):
            acc[s * SUB_ROWS:(s + 1) * SUB_ROWS, :] = jnp.broadcast_to(init, (SUB_ROWS, D_MODEL))

        for_subs(no, init_acc)

    nxt_item = jnp.minimum(w + 1, n_w - 1)
    nxt_nc = jnp.where(w + 1 < n_w, item_nc_ref[nxt_item], 0)
    last = f == n_f - 1

    per_step = BLOCK_SUBS // N_F
    for i in range(per_step):
        s_dyn = f * per_step + i
        pl.when(s_dyn < nxt_nc)(functools.partial(gather_sub, nxt_item, s_dyn))

    for s in range(BLOCK_SUBS, ITEM_SUBS):
        pl.when(last & (s < nxt_nc))(functools.partial(gather_sub, nxt_item, s))

    def act(guc):
        g = jnp.minimum(guc, SWIGLU_LIMIT)
        u = jnp.clip(pltpu.roll(guc, LANES - 1, axis=1), -SWIGLU_LIMIT, SWIGLU_LIMIT)
        return g * _sigmoid(SWIGLU_ALPHA * g) * (u + 1.0)

    def rows_block(r0, m, cast):
        rows = pl.ds(r0, m) if isinstance(r0, int) else pl.ds(pl.multiple_of(r0, SUB_ROWS), m)
        x = xb[rows, :]
        n_k = tf // MXU_DIM
        even = (lax.broadcasted_iota(jnp.int32, (m, LANES), 1) & 1) == 0

        def gate_up(k):
            outs = []
            for n in range(2):
                cols = slice((2 * k + n) * MXU_DIM, (2 * k + n + 1) * MXU_DIM)
                if cast:
                    wt = wgu_ref[0, :, cols].astype(BF16)
                    wgu_bf[:, cols] = wt
                else:
                    wt = wgu_bf[:, cols]
                outs.append(jnp.dot(x, wt, preferred_element_type=F32) + bgu_ref[0, :, cols])
            return outs

        def activate(gus):
            parts = [jnp.where(even, act(gu[:, :LANES]), pltpu.roll(act(gu[:, LANES:]), 1, axis=1))
                     for gu in gus]
            packed = jnp.concatenate(parts, axis=-1).astype(BF16)
            return jnp.dot(packed, unpack_ref[...], preferred_element_type=F32).astype(BF16)

        gus = gate_up(0)
        for k in range(n_k):
            nxt = gate_up(k + 1) if k + 1 < n_k else None
            a = activate(gus)
            wrows = slice(k * MXU_DIM, (k + 1) * MXU_DIM)
            if cast:
                wdt = wd_ref[0, wrows, :].astype(BF16)
                wd_bf[wrows, :] = wdt
            else:
                wdt = wd_bf[wrows, :]
            for c in range(D_MODEL // ACC_COLS):
                cs = slice(c * ACC_COLS, (c + 1) * ACC_COLS)
                acc[rows, cs] += jnp.dot(a, wdt[:, cs], preferred_element_type=F32)
            gus = nxt

    g0 = item_g0_ref[w]
    block_rows = BLOCK_SUBS * SUB_ROWS
    big = nc >= BLOCK_SUBS

    def extra_sub(s, start_out):
        rows_block(s * SUB_ROWS, SUB_ROWS, False)
        if start_out:
            out_copy(s, g0).start()

    @pl.when(big & jnp.logical_not(last))
    def _():
        rows_block(0, block_rows, True)
        for s in range(BLOCK_SUBS, ITEM_SUBS):
            pl.when(s < nc)(functools.partial(extra_sub, s, False))

    @pl.when(big & last)
    def _():
        half = BLOCK_SUBS // 2
        rows_block(0, block_rows // 2, True)
        for s in range(half):
            out_copy(s, g0).start()
        rows_block(block_rows // 2, block_rows // 2, False)
        for s in range(half, BLOCK_SUBS):
            out_copy(s, g0).start()
        for s in range(BLOCK_SUBS, ITEM_SUBS):
            pl.when(s < nc)(functools.partial(extra_sub, s, True))

    @pl.when((nc > 0) & jnp.logical_not(big))
    def _():
        def chunk(s0, n_sub, cast):
            rows_block(s0 * SUB_ROWS, n_sub * SUB_ROWS, cast)

            def start_outs():
                for i in range(n_sub):
                    out_copy(s0 + i, g0).start()

            pl.when(last)(start_outs)

        sizes = []
        n_sub = BLOCK_SUBS // 2
        while n_sub >= 1:
            sizes.append(n_sub)
            n_sub //= 2
        for i, n_sub in enumerate(sizes):
            has = (nc & n_sub) != 0
            larger = nc & ~(2 * n_sub - 1)
            pl.when(has & (larger == 0))(functools.partial(chunk, 0, n_sub, True))
            if i > 0:
                pl.when(has & (larger != 0))(functools.partial(chunk, larger, n_sub, False))

    @pl.when((nc == 0) & last)
    def _():
        for_subs(no, lambda s: out_copy(s, g0).start())

    @pl.when(last & (w == n_w - 1))
    def _():
        for_subs(no, lambda s: out_copy(s, 0).wait())


def _unpack_matrix():
    lane = jnp.arange(MXU_DIM)
    n, r = lane // LANES, lane % LANES
    natural = n * LANES + (r % 2) * (LANES // 2) + r // 2
    return (natural[:, None] == jnp.arange(MXU_DIM)[None, :]).astype(BF16)


def _expert_items(counts, n_asg):
    E = N_EXPERTS
    item_rows = SUB_ROWS * ITEM_SUBS
    n_items_max = (n_asg // SUB_ROWS + E) // ITEM_SUBS + E + 1
    n_rows = n_asg + E * SUB_ROWS
    pcounts = (counts + SUB_ROWS - 1) // SUB_ROWS * SUB_ROWS
    pend = jnp.cumsum(pcounts)
    pstart = pend - pcounts
    nsb_e = pcounts // SUB_ROWS
    n_it = (nsb_e + ITEM_SUBS - 1) // ITEM_SUBS
    extra_e = jnp.maximum(nsb_e - BLOCK_SUBS * n_it, 0)
    it_end = jnp.cumsum(n_it)
    it_start = it_end - n_it
    n_items = it_end[-1]
    wv = jnp.arange(n_items_max, dtype=jnp.int32)
    e_w = jnp.minimum(jnp.sum(wv[:, None] >= it_end[None, :], axis=1), E - 1)
    j_w = wv - it_start[e_w]
    first_sub = lambda j: BLOCK_SUBS * j + jnp.minimum(j, extra_e[e_w])
    sb0 = first_sub(j_w)
    row0 = pstart[e_w] + sb0 * SUB_ROWS
    nsb = jnp.clip(jnp.minimum(nsb_e[e_w], first_sub(j_w + 1)) - sb0, 0, ITEM_SUBS)
    used = wv < n_items
    tail0 = pend[-1] + (wv - n_items) * item_rows
    nz = jnp.clip((n_rows - tail0) // SUB_ROWS, 0, ITEM_SUBS)
    item_row0 = jnp.clip(jnp.where(used, row0, tail0), 0, n_rows - SUB_ROWS)
    item_e = jnp.where(used, e_w, e_w[jnp.maximum(n_items - 1, 0)])
    i32 = lambda a: a.astype(jnp.int32)
    return (i32(item_e), i32(item_row0 // SUBLANES), i32(jnp.where(used, nsb, 0)),
            i32(jnp.where(used, nsb, nz)), pstart, n_rows)


def _experts(item_e, item_g0, item_nc, item_no, row_tok, h2, wgu, wd, bgu, bd, tf):
    n_rows = row_tok.shape[0]
    n_items = item_e.shape[0]
    n_f = EXPERT_FF // tf
    item_rows = SUB_ROWS * ITEM_SUBS
    assert tf % MXU_DIM == 0 and n_f == N_F and N_F <= BLOCK_SUBS <= ITEM_SUBS

    def f_eff(w, f, nc):
        return jnp.where(nc[w] > 0, f, n_f - 1)

    grid_spec = pltpu.PrefetchScalarGridSpec(
        num_scalar_prefetch=5,
        grid=(n_items, n_f),
        in_specs=[
            pl.BlockSpec(memory_space=pl.ANY),
            pl.BlockSpec((1, D_MODEL, 2 * tf), lambda w, f, ie, g0, nc, no, rt: (ie[w], 0, f_eff(w, f, nc))),
            pl.BlockSpec((1, tf, D_MODEL), lambda w, f, ie, g0, nc, no, rt: (ie[w], f_eff(w, f, nc), 0)),
            pl.BlockSpec((1, 1, 2 * tf), lambda w, f, ie, g0, nc, no, rt: (ie[w], 0, f_eff(w, f, nc))),
            pl.BlockSpec((1, 1, D_MODEL), lambda w, f, ie, g0, nc, no, rt: (ie[w], 0, 0)),
            pl.BlockSpec((MXU_DIM, MXU_DIM), lambda w, f, ie, g0, nc, no, rt: (0, 0)),
        ],
        out_specs=pl.BlockSpec(memory_space=pl.ANY),
        scratch_shapes=[
            pltpu.VMEM((item_rows // SUBLANES, SUBLANES, D_MODEL), F32),
            pltpu.VMEM((item_rows, D_MODEL), BF16),
            pltpu.VMEM((item_rows, D_MODEL), F32),
            pltpu.VMEM((D_MODEL, 2 * tf), BF16),
            pltpu.VMEM((tf, D_MODEL), BF16),
            pltpu.SemaphoreType.DMA((1,)),
            pltpu.SemaphoreType.DMA((1,)),
        ],
    )
    return pl.pallas_call(
        _expert_kernel,
        name="experts",
        grid_spec=grid_spec,
        out_shape=jax.ShapeDtypeStruct((n_rows, D_MODEL), F32),
        compiler_params=_cparams(("arbitrary", "arbitrary"), 60),
    )(item_e, item_g0, item_nc, item_no, row_tok, h2, wgu, wd, bgu, bd, _unpack_matrix())


def _combine_kernel(dest_ref, yb_hbm, gate_ref, x1_ref, p_ref, wpg_ref, wpp_ref, g3_ref, gf_ref,
                    o_ref, gbuf, sem):
    i = pl.program_id(0)
    n = pl.num_programs(0)
    tm = x1_ref.shape[0]

    def issue(tile):
        base = tile * tm * TOP_K

        def body(j, c):
            for u in range(SUBLANES):
                first = base + (j * SUBLANES + u) * TOP_K
                for k in range(TOP_K):
                    row = dest_ref[first + k]
                    pltpu.make_async_copy(yb_hbm.at[pl.ds(row, 1), :],
                                          gbuf.at[k, j, pl.ds(u, 1), :], sem.at[0]).start()
            return c

        lax.fori_loop(0, tm // SUBLANES, body, 0)

    @pl.when(i == 0)
    def _():
        issue(0)

    for k in range(TOP_K):
        pltpu.make_async_copy(gbuf.at[k], gbuf.at[k], sem.at[0]).wait()
    gates = gate_ref[...]
    moe = gates[:, 0:1] * gbuf[0].reshape(tm, D_MODEL)
    for k in range(1, TOP_K):
        moe = moe + gates[:, k:k + 1] * gbuf[k].reshape(tm, D_MODEL)
    x2 = x1_ref[...] + moe

    @pl.when(i + 1 < n)
    def _():
        issue(i + 1)

    h3 = _rms(x2, g3_ref[...]).astype(BF16)
    pg = _sigmoid(jnp.dot(h3, wpg_ref[...], preferred_element_type=F32))
    pp = jnp.dot(p_ref[...].astype(BF16), wpp_ref[...], preferred_element_type=F32)
    x3 = x2 + pg * pp
    o_ref[...] = _rms(x3, gf_ref[...])


def _combine(dest, yb, gates, x1, p2, wpg, wpp, g3, gf, tm):
    L = x1.shape[0]
    const = lambda *shape: pl.BlockSpec(shape, lambda i, d: (0,) * len(shape), pipeline_mode=pl.Buffered(1))
    row = lambda w: pl.BlockSpec((tm, w), lambda i, d: (i, 0))
    grid_spec = pltpu.PrefetchScalarGridSpec(
        num_scalar_prefetch=1,
        grid=(L // tm,),
        in_specs=[
            pl.BlockSpec(memory_space=pl.ANY),
            row(ROUTER_PAD), row(D_MODEL), row(PLE_DIM),
            const(D_MODEL, D_MODEL), const(PLE_DIM, D_MODEL), const(1, D_MODEL), const(1, D_MODEL),
        ],
        out_specs=row(D_MODEL),
        scratch_shapes=[
            pltpu.VMEM((TOP_K, tm // SUBLANES, SUBLANES, D_MODEL), F32),
            pltpu.SemaphoreType.DMA((1,)),
        ],
    )
    return pl.pallas_call(
        _combine_kernel,
        name="combine_ple",
        grid_spec=grid_spec,
        out_shape=jax.ShapeDtypeStruct((L, D_MODEL), F32),
        compiler_params=_cparams(("arbitrary",), 56),
    )(dest, yb, gates, x1, p2, wpg, wpp, g3, gf)


def _tile(L, pref):
    return min(pref, L)


def _layer(x2, p2, pos, norm1_g, w_in, lam_re, lam_im, log_dt, b_re, b_im, c_re, c_im, d_skip, w_glu,
           b_glu, w_bs, w_br, w_out, norm2_g, router_w, router_b, w_gu, b_gu, w_dn, b_dn, norm3_g,
           w_pg, w_pp, out_g):
    L = x2.shape[0]
    G, N, E = SSM_GROUPS, SSM_STATE, N_EXPERTS

    cuts = SSM_WIDTH + 4 * RET_WIDTH
    w_in_bf = jnp.concatenate([w_in[:, cuts:], w_in[:, :cuts]], axis=1).astype(BF16)
    z = _inproj(x2, norm1_g[None, :], w_in_bf, _tile(L, 1024), 1536)
    ms_blk, mr_blk = 0, 1
    u_blk, q_blk, k_blk, v_blk, g_blk = 4, 5, 6, 7, 8

    ab_re, ab_im, bb_re, bb_im = _s5_prep(lam_re, lam_im, log_dt, b_re, b_im)
    eye = jnp.eye(SUBLANES, dtype=F32)
    gpb = LANES // SSM_GROUP
    def bpack(bb):
        t = bb.reshape(N_UBLK, gpb, SSM_GROUP, N)
        return jnp.einsum('bgcn,gh->bgchn', t, eye).reshape(N_UBLK, LANES, ST_PER_UBLK)
    def cpack(c):
        t = c.reshape(N_UBLK, gpb, SSM_GROUP, N)
        return jnp.einsum('bgcn,gh->bgnhc', t, eye).reshape(N_UBLK, ST_PER_UBLK, LANES)
    bblk = jnp.concatenate([bpack(bb_re), bpack(bb_im)], axis=-1).astype(BF16)
    cblk = jnp.concatenate([cpack(c_re), -cpack(c_im)], axis=1).astype(BF16)
    atile = lambda a: a[::SSM_GROUP].reshape(N_SVREG, SUBLANES, LANES)
    ys = _s5(z, u_blk, bblk, cblk, atile(ab_re), atile(ab_im), d_skip.reshape(1, SSM_WIDTH),
             w_glu.astype(BF16), b_glu[None, :], _tile(L, 512))

    yr = _retention(z, (q_blk, k_blk, v_blk, g_blk), pos, _tile(L, 512))

    rw = jnp.pad(router_w, ((0, 0), (0, ROUTER_PAD - E)))
    rwh = rw.astype(BF16)
    rwl = (rw - rwh.astype(F32)).astype(BF16)
    rb = jnp.pad(router_b, (0, ROUTER_PAD - E), constant_values=NEG_BIG)[None, :]
    x1, h2, idx, gates, rank, cnt = _merge(
        ys, yr, z, ms_blk, mr_blk, x2, w_bs.astype(BF16), w_br.astype(BF16), w_out.astype(BF16),
        norm2_g[None, :], rwh, rwl, rb, _tile(L, 256))

    item_e, item_g0, item_nc, item_no, pstart, n_rows = _expert_items(cnt[0, :E], L * TOP_K)
    idx4, rank4 = idx[:, :TOP_K], rank[:, :TOP_K]
    seg0 = jnp.sum(jnp.where(idx4[..., None] == jnp.arange(E, dtype=jnp.int32), pstart.astype(jnp.int32), 0), axis=-1)
    dest = (seg0 + rank4).astype(jnp.int32).reshape(-1)
    row_tok = _row_tokens(dest, n_rows)

    yb = _experts(item_e, item_g0, item_nc, item_no, row_tok, h2, w_gu, w_dn, b_gu[:, None, :],
                  b_dn[:, None, :], EXPERT_FF // N_F)

    return _combine(dest, yb, gates, x1, p2, w_pg.astype(BF16), w_pp.astype(BF16),
                    norm3_g[None, :], out_g[None, :], _tile(L, 512))


def kernel(x, p, positions, norm1_g, w_in, ssm_lam_re, ssm_lam_im, ssm_log_dt, ssm_b_re, ssm_b_im, ssm_c_re, ssm_c_im, ssm_d, ssm_w_glu, ssm_b_glu, w_branch_ssm, w_branch_ret, w_out, norm2_g, router_w, router_b, exp_w_gate_up, exp_b_gate_up, exp_w_down, exp_b_down, norm3_g, ple_w_gate, ple_w_proj, final_norm_g):
    bsz, L, d = x.shape
    depth = w_in.shape[0]
    assert bsz == 1 and depth == 1 and d == D_MODEL
    out = _layer(
        x[0], p[0, 0], positions.reshape(L, 1), norm1_g[0], w_in[0], ssm_lam_re[0], ssm_lam_im[0],
        ssm_log_dt[0], ssm_b_re[0], ssm_b_im[0], ssm_c_re[0], ssm_c_im[0], ssm_d[0], ssm_w_glu[0],
        ssm_b_glu[0], w_branch_ssm[0], w_branch_ret[0], w_out[0], norm2_g[0], router_w[0], router_b[0],
        exp_w_gate_up[0], exp_b_gate_up[0], exp_w_down[0], exp_b_down[0], norm3_g[0], ple_w_gate[0],
        ple_w_proj[0], final_norm_g)
    return out[None]
```
